```python
import math
import jax
import jax.numpy as jnp
from jax import lax
import numpy as np

D_MODEL = 2048
BATCH = 2
SEQ = 8192
DEPTH = 4

MIXER_ORDER = ("gla", "pool", "diff")
N_MIXERS = 3
NORM_EPS = 1e-6
PLE_DIM = 256

GLA_HEADS = 4
GLA_KEY_DIM = D_MODEL // 2
GLA_VALUE_DIM = D_MODEL
GLA_DK = GLA_KEY_DIM // GLA_HEADS
GLA_DV = GLA_VALUE_DIM // GLA_HEADS
GLA_GATE_RANK = 16
GLA_GATE_NORMALIZER = 16.0
GLA_CHUNK = 64
GLA_IN_WIDTH = 2 * GLA_KEY_DIM + 2 * GLA_VALUE_DIM + GLA_GATE_RANK

POOL_WINDOWS = (2, 4, 8, 16)
POOL_GROUP = D_MODEL // len(POOL_WINDOWS)

DIFF_HEADS = 8
DIFF_HEAD_DIM = D_MODEL // DIFF_HEADS // 2
DIFF_QK_WIDTH = 2 * DIFF_HEADS * DIFF_HEAD_DIM
DIFF_V_WIDTH = DIFF_HEADS * 2 * DIFF_HEAD_DIM
DIFF_IN_WIDTH = 2 * DIFF_QK_WIDTH + DIFF_V_WIDTH
Q_BLOCK = 128
LAMBDA_STD = 0.1

ROPE_THETA = 500000.0
ROPE_DIMS = DIFF_HEAD_DIM // 4

D_FF = 5632
N_EXPERTS = 8
TOP_K = 2

kernel_name = "hybrid_gla_pool_diffattn_moe_ple"


def rmsnorm(x, w):
    xf = x.astype(jnp.float32)
    y = xf * lax.rsqrt(jnp.mean(xf * xf, axis=-1, keepdims=True) + NORM_EPS)
    return (y * w.astype(jnp.float32)).astype(x.dtype)


def gla_recurrence(q, k, v, gk):
    B, T, H, DK = q.shape
    DV = v.shape[-1]
    n = T // GLA_CHUNK

    def chunks(t):
        return t.astype(jnp.float32).reshape(B, n, GLA_CHUNK, H, t.shape[-1]).transpose(1, 0, 3, 2, 4)

    qc, kc, vc, gc = chunks(q), chunks(k), chunks(v), chunks(gk)
    b = jnp.cumsum(gc, axis=3)
    b_last = b[:, :, :, -1:, :]
    q_in = qc * jnp.exp(b)
    k_in = kc * jnp.exp(-b)
    k_out = kc * jnp.exp(b_last - b)
    decay = jnp.exp(b_last[:, :, :, 0, :])
    causal = jnp.tril(jnp.ones((GLA_CHUNK, GLA_CHUNK), dtype=bool))
    scores = jnp.where(causal, jnp.einsum('nbhcd,nbhsd->nbhcs', q_in, k_in), 0.0)
    o_intra = jnp.einsum('nbhcs,nbhse->nbhce', scores, vc)

    def step(state, inp):
        q_n, k_n, v_n, d_n = inp
        o_n = jnp.einsum('bhcd,bhde->bhce', q_n, state)
        state = state * d_n[..., None] + jnp.einsum('bhcd,bhce->bhde', k_n, v_n)
        return state, o_n

    s0 = jnp.zeros((B, H, DK, DV), jnp.float32)
    _, o_inter = lax.scan(step, s0, (q_in, k_out, vc, decay))
    o = (o_intra + o_inter).transpose(1, 0, 3, 2, 4).reshape(B, T, H, DV)
    return o.astype(v.dtype)


def gla_mixer(xn, w_in, w_a2, b_a2, norm_w, w_o):
    B, T, _ = xn.shape
    splits = [GLA_KEY_DIM, 2 * GLA_KEY_DIM, 2 * GLA_KEY_DIM + GLA_VALUE_DIM, 2 * GLA_KEY_DIM + 2 * GLA_VALUE_DIM]
    q, k, v, g, a = jnp.split(xn @ w_in, splits, axis=-1)
    gk = jax.nn.log_sigmoid((a @ w_a2 + b_a2).astype(jnp.float32)) / GLA_GATE_NORMALIZER

    def heads(t, d):
        return t.reshape(B, T, GLA_HEADS, d)

    o = gla_recurrence(heads(q, GLA_DK) * (GLA_DK ** -0.5), heads(k, GLA_DK), heads(v, GLA_DV), heads(gk, GLA_DK))
    o = rmsnorm(o, norm_w) * jax.nn.silu(heads(g, GLA_DV))
    return o.reshape(B, T, GLA_VALUE_DIM) @ w_o


def pool_mixer(xn, w_groups, scale):
    B, T, _ = xn.shape
    xf = xn.astype(jnp.float32)
    cs = jnp.cumsum(xf, axis=1)
    pos = jnp.arange(T)
    parts = []
    for gi, w in enumerate(POOL_WINDOWS):
        sl = slice(gi * POOL_GROUP, (gi + 1) * POOL_GROUP)
        c = cs[..., sl]
        c_shift = jnp.pad(c, ((0, 0), (w, 0), (0, 0)))[:, :T]
        count = jnp.minimum(pos + 1, w).astype(jnp.float32)[None, :, None]
        parts.append((c - c_shift) / count - xf[..., sl])
    pooled = jnp.stack(parts, axis=2).astype(xn.dtype)
    y = jnp.einsum('btgc,gcd->btgd', pooled, w_groups).reshape(B, T, D_MODEL)
    return y * scale


def partial_rotary(x, positions):
    half = ROPE_DIMS // 2
    inv_freq = ROPE_THETA ** (-jnp.arange(half, dtype=jnp.float32) * 2.0 / ROPE_DIMS)
    ang = positions.astype(jnp.float32)[..., None] * inv_freq
    cos = jnp.cos(ang)[:, :, None, :]
    sin = jnp.sin(ang)[:, :, None, :]
    xf = x.astype(jnp.float32)
    x1 = xf[..., :half]
    x2 = xf[..., half:ROPE_DIMS]
    out = jnp.concatenate([x1 * cos - x2 * sin, x2 * cos + x1 * sin, xf[..., ROPE_DIMS:]], axis=-1)
    return out.astype(x.dtype)


def diff_attention(xn, positions, w_in, lq1, lk1, lq2, lk2, norm_w, w_o, lambda_init):
    B, T, _ = xn.shape
    H, DH = DIFF_HEADS, DIFF_HEAD_DIM
    q, k, v = jnp.split(xn @ w_in, [DIFF_QK_WIDTH, 2 * DIFF_QK_WIDTH], axis=-1)
    q = partial_rotary(q.reshape(B, T, 2 * H, DH), positions)
    k = partial_rotary(k.reshape(B, T, 2 * H, DH), positions)
    q = q.reshape(B, T, H, 2, DH).transpose(0, 2, 3, 1, 4)
    k = k.reshape(B, T, H, 2, DH).transpose(0, 2, 3, 1, 4)
    v = v.reshape(B, T, H, 2 * DH).transpose(0, 2, 1, 3)
    f32 = jnp.float32
    lam = (jnp.exp(jnp.sum(lq1.astype(f32) * lk1.astype(f32)))
           - jnp.exp(jnp.sum(lq2.astype(f32) * lk2.astype(f32))) + lambda_init)
    scale = DH ** -0.5
    nb = T // Q_BLOCK
    q_blocks = q.reshape(B, H, 2, nb, Q_BLOCK, DH).transpose(3, 0, 1, 2, 4, 5)
    starts = jnp.arange(nb, dtype=jnp.int32) * Q_BLOCK
    key_pos = jnp.arange(T, dtype=jnp.int32)

    def block(args):
        qb, start = args
        s = jnp.einsum('bhiqd,bhikd->bhiqk', qb, k).astype(f32) * scale
        q_pos = start + jnp.arange(Q_BLOCK, dtype=jnp.int32)
        s = jnp.where(key_pos[None, :] <= q_pos[:, None], s, -jnp.inf)
        prob = jax.nn.softmax(s, axis=-1)
        a = prob[:, :, 0] - lam * prob[:, :, 1]
        return jnp.einsum('bhqk,bhkd->bhqd', a.astype(v.dtype), v)

    o = lax.map(block, (q_blocks, starts))
    o = o.transpose(1, 0, 3, 2, 4).reshape(B, T, H, 2 * DH)
    o = rmsnorm(o, norm_w) * (1.0 - lambda_init)
    return o.reshape(B, T, DIFF_V_WIDTH) @ w_o


def swiglu(xn, w_gu, w_down):
    gate, up = jnp.split(xn @ w_gu, 2, axis=-1)
    return (jax.nn.silu(gate) * up) @ w_down


def moe_swiglu(xn, router, w_gu, w_down):
    B, T, D = xn.shape
    xt = xn.reshape(B * T, D)
    logits = (xt @ router).astype(jnp.float32)
    top_logits, top_idx = lax.top_k(logits, TOP_K)
    gates = jax.nn.softmax(top_logits, axis=-1)
    combine = jnp.sum(jax.nn.one_hot(top_idx, N_EXPERTS, dtype=jnp.float32) * gates[..., None], axis=1)
    out = jnp.zeros_like(xt)
    for e in range(N_EXPERTS):
        out = out + combine[:, e:e + 1].astype(xt.dtype) * swiglu(xt, w_gu[e], w_down[e])
    return out.reshape(B, T, D)


def per_layer_embedding(h, p_i, norm_w, w_gate, b_gate, w_proj):
    gate = jax.nn.sigmoid((rmsnorm(h, norm_w) @ w_gate + b_gate).astype(jnp.float32))
    return h + (gate * (p_i @ w_proj).astype(jnp.float32)).astype(h.dtype)


def diff_lambda_init(layer_idx):
    return 0.8 - 0.6 * math.exp(-0.3 * layer_idx)


def setup_inputs(seed: int = 0) -> dict:
    key = jax.random.key(seed)
    keys = jax.random.split(key, 128)
    counter = [0]

    def nk():
        k = keys[counter[0]]
        counter[0] += 1
        return k

    def dense(shape, fan_in):
        return jax.random.normal(nk(), shape, jnp.float32) * (fan_in ** -0.5)

    def gain(n):
        return 1.0 + 0.02 * jax.random.normal(nk(), (n,), jnp.float32)

    def small(shape, s=0.01):
        return s * jax.random.normal(nk(), shape, jnp.float32)

    inp = {}
    inp["x"] = jax.random.normal(nk(), (BATCH, SEQ, D_MODEL), jnp.float32)
    inp["p"] = jax.random.normal(nk(), (DEPTH, BATCH, SEQ, PLE_DIM), jnp.float32)
    offsets = jax.random.randint(nk(), (BATCH, 1), 0, 1024, dtype=jnp.int32)
    inp["positions"] = offsets + jnp.arange(SEQ, dtype=jnp.int32)[None, :]
    for i in range(DEPTH):
        kind = MIXER_ORDER[i % N_MIXERS]
        pre = "l%d_" % i
        inp[pre + "norm_mix"] = gain(D_MODEL)
        if kind == "gla":
            inp[pre + "gla_w_in"] = dense((D_MODEL, GLA_IN_WIDTH), D_MODEL)
            inp[pre + "gla_w_a2"] = dense((GLA_GATE_RANK, GLA_KEY_DIM), GLA_GATE_RANK)
            inp[pre + "gla_b_a2"] = small((GLA_KEY_DIM,))
            inp[pre + "gla_norm"] = gain(GLA_DV)
            inp[pre + "gla_w_o"] = dense((GLA_VALUE_DIM, D_MODEL), GLA_VALUE_DIM)
        elif kind == "pool":
            inp[pre + "pool_w"] = dense((len(POOL_WINDOWS), POOL_GROUP, POOL_GROUP), POOL_GROUP)
            inp[pre + "pool_scale"] = gain(D_MODEL)
        else:
            inp[pre + "diff_w_in"] = dense((D_MODEL, DIFF_IN_WIDTH), D_MODEL)
            inp[pre + "diff_lq1"] = small((DIFF_HEAD_DIM,), LAMBDA_STD)
            inp[pre + "diff_lk1"] = small((DIFF_HEAD_DIM,), LAMBDA_STD)
            inp[pre + "diff_lq2"] = small((DIFF_HEAD_DIM,), LAMBDA_STD)
            inp[pre + "diff_lk2"] = small((DIFF_HEAD_DIM,), LAMBDA_STD)
            inp[pre + "diff_norm"] = gain(2 * DIFF_HEAD_DIM)
            inp[pre + "diff_w_o"] = dense((DIFF_V_WIDTH, D_MODEL), DIFF_V_WIDTH)
        inp[pre + "norm_ffn"] = gain(D_MODEL)
        if i % 2 == 0:
            inp[pre + "ffn_w_gu"] = dense((D_MODEL, 2 * D_FF), D_MODEL)
            inp[pre + "ffn_w_down"] = dense((D_FF, D_MODEL), D_FF)
        else:
            inp[pre + "moe_router"] = dense((D_MODEL, N_EXPERTS), D_MODEL)
            inp[pre + "moe_w_gu"] = dense((N_EXPERTS, D_MODEL, 2 * D_FF), D_MODEL)
            inp[pre + "moe_w_down"] = dense((N_EXPERTS, D_FF, D_MODEL), D_FF)
        inp[pre + "ple_norm"] = gain(D_MODEL)
        inp[pre + "ple_w_gate"] = dense((D_MODEL, D_MODEL), D_MODEL)
        inp[pre + "ple_b_gate"] = small((D_MODEL,))
        inp[pre + "ple_w_proj"] = dense((PLE_DIM, D_MODEL), PLE_DIM)
    inp["final_norm"] = gain(D_MODEL)
    return inp


def reference(x, p, positions,
              l0_norm_mix, l0_gla_w_in, l0_gla_w_a2, l0_gla_b_a2, l0_gla_norm, l0_gla_w_o,
              l0_norm_ffn, l0_ffn_w_gu, l0_ffn_w_down,
              l0_ple_norm, l0_ple_w_gate, l0_ple_b_gate, l0_ple_w_proj,
              l1_norm_mix, l1_pool_w, l1_pool_scale,
              l1_norm_ffn, l1_moe_router, l1_moe_w_gu, l1_moe_w_down,
              l1_ple_norm, l1_ple_w_gate, l1_ple_b_gate, l1_ple_w_proj,
              l2_norm_mix, l2_diff_w_in, l2_diff_lq1, l2_diff_lk1, l2_diff_lq2, l2_diff_lk2, l2_diff_norm, l2_diff_w_o,
              l2_norm_ffn, l2_ffn_w_gu, l2_ffn_w_down,
              l2_ple_norm, l2_ple_w_gate, l2_ple_b_gate, l2_ple_w_proj,
              l3_norm_mix, l3_gla_w_in, l3_gla_w_a2, l3_gla_b_a2, l3_gla_norm, l3_gla_w_o,
              l3_norm_ffn, l3_moe_router, l3_moe_w_gu, l3_moe_w_down,
              l3_ple_norm, l3_ple_w_gate, l3_ple_b_gate, l3_ple_w_proj,
              final_norm):
    layers = [
        (l0_norm_mix, (l0_gla_w_in, l0_gla_w_a2, l0_gla_b_a2, l0_gla_norm, l0_gla_w_o),
         l0_norm_ffn, (l0_ffn_w_gu, l0_ffn_w_down),
         (l0_ple_norm, l0_ple_w_gate, l0_ple_b_gate, l0_ple_w_proj)),
        (l1_norm_mix, (l1_pool_w, l1_pool_scale),
         l1_norm_ffn, (l1_moe_router, l1_moe_w_gu, l1_moe_w_down),
         (l1_ple_norm, l1_ple_w_gate, l1_ple_b_gate, l1_ple_w_proj)),
        (l2_norm_mix, (l2_diff_w_in, l2_diff_lq1, l2_diff_lk1, l2_diff_lq2, l2_diff_lk2, l2_diff_norm, l2_diff_w_o),
         l2_norm_ffn, (l2_ffn_w_gu, l2_ffn_w_down),
         (l2_ple_norm, l2_ple_w_gate, l2_ple_b_gate, l2_ple_w_proj)),
        (l3_norm_mix, (l3_gla_w_in, l3_gla_w_a2, l3_gla_b_a2, l3_gla_norm, l3_gla_w_o),
         l3_norm_ffn, (l3_moe_router, l3_moe_w_gu, l3_moe_w_down),
         (l3_ple_norm, l3_ple_w_gate, l3_ple_b_gate, l3_ple_w_proj)),
    ]
    h = x
    for i in range(DEPTH):
        norm_mix, mix_params, norm_ffn, ffn_params, ple_params = layers[i]
        kind = MIXER_ORDER[i % N_MIXERS]
        xn = rmsnorm(h, norm_mix)
        if kind == "gla":
            y = gla_mixer(xn, *mix_params)
        elif kind == "pool":
            y = pool_mixer(xn, *mix_params)
        else:
            y = diff_attention(xn, positions, *mix_params, lambda_init=diff_lambda_init(i))
        h = h + y
        xn = rmsnorm(h, norm_ffn)
        h = h + (swiglu(xn, *ffn_params) if i % 2 == 0 else moe_swiglu(xn, *ffn_params))
        h = per_layer_embedding(h, p[i], *ple_params)
    return rmsnorm(h, final_norm)
```

```python
import functools
import math

import jax
import jax.numpy as jnp
from jax import lax
from jax.experimental import pallas as pl
from jax.experimental.pallas import tpu as pltpu

F32 = jnp.float32
BF16 = jnp.bfloat16

NORM_EPS = 1e-6
LANES = 128
V7X_VMEM_BYTES = 64 * 1024 * 1024
VMEM_LIMIT = V7X_VMEM_BYTES * 7 // 8

GLA_HEADS = 4
GLA_GATE_RANK = 16
GLA_GATE_NORMALIZER = 16.0
GLA_CHUNK = 64
POOL_WINDOWS = (2, 4, 8, 16)
POOL_HALO = 16
DIFF_HEADS = 8
DIFF_HEAD_DIM = 128
ROPE_THETA = 500000.0
ROPE_DIMS = 32
N_EXPERTS = 8

NT_DIMS = (((1,), (1,)), ((), ()))
TN_DIMS = (((0,), (0,)), ((), ()))


def _cparams(*sem):
    return pltpu.CompilerParams(dimension_semantics=sem, vmem_limit_bytes=VMEM_LIMIT)


def _rms(x, w):
    ms = jnp.mean(x * x, axis=-1, keepdims=True)
    return x * lax.rsqrt(ms + NORM_EPS) * w


def _split3(x):
    hi = x.astype(BF16)
    r1 = x - hi.astype(F32)
    mid = r1.astype(BF16)
    lo = (r1 - mid.astype(F32)).astype(BF16)
    return hi, mid, lo


def _norm_mm_body(h_ref, nw_ref, w_ref, o_ref, xn_ref):
    @pl.when(pl.program_id(1) == 0)
    def _():
        xn_ref[...] = _rms(h_ref[...], nw_ref[...]).astype(BF16)

    o_ref[...] = jnp.dot(xn_ref[...], w_ref[...], preferred_element_type=F32).astype(o_ref.dtype)


def _norm_mm_extra_body(h_ref, nw_ref, w_ref, wx_ref, o_ref, ox_ref, xn_ref):
    @pl.when(pl.program_id(1) == 0)
    def _():
        xn = _rms(h_ref[...], nw_ref[...]).astype(BF16)
        xn_ref[...] = xn
        ox_ref[...] = jnp.dot(xn, wx_ref[...], preferred_element_type=F32)

    o_ref[...] = jnp.dot(xn_ref[...], w_ref[...], preferred_element_type=F32).astype(o_ref.dtype)


def norm_matmul(h, nw, w, *, tm, tn, w_extra=None):
    n, d = h.shape
    nout = w.shape[1]
    grid = (n // tm, nout // tn)
    h_spec = pl.BlockSpec((tm, d), lambda i, j: (i, 0))
    nw_spec = pl.BlockSpec((1, d), lambda i, j: (0, 0))
    w_spec = pl.BlockSpec((d, tn), lambda i, j: (0, j))
    o_spec = pl.BlockSpec((tm, tn), lambda i, j: (i, j))
    scratch = [pltpu.VMEM((tm, d), BF16)]
    if w_extra is None:
        return pl.pallas_call(
            _norm_mm_body, grid=grid, in_specs=[h_spec, nw_spec, w_spec], out_specs=o_spec,
            out_shape=jax.ShapeDtypeStruct((n, nout), BF16), scratch_shapes=scratch,
            compiler_params=_cparams("parallel", "arbitrary"), name="norm_matmul")(h, nw, w)
    nx = w_extra.shape[1]
    return pl.pallas_call(
        _norm_mm_extra_body, grid=grid,
        in_specs=[h_spec, nw_spec, w_spec, pl.BlockSpec((d, nx), lambda i, j: (0, 0))],
        out_specs=[o_spec, pl.BlockSpec((tm, nx), lambda i, j: (i, 0))],
        out_shape=[jax.ShapeDtypeStruct((n, nout), BF16), jax.ShapeDtypeStruct((n, nx), F32)],
        scratch_shapes=scratch,
        compiler_params=_cparams("parallel", "arbitrary"), name="norm_matmul_extra")(h, nw, w, w_extra)


def _rope_tab_body(pos_ref, invf_ref, cos_ref, sina_ref, sinb_ref):
    ang = pos_ref[...].astype(F32) * invf_ref[...]
    lane = lax.broadcasted_iota(jnp.int32, ang.shape, 1)
    half = ROPE_DIMS // 2
    s = jnp.sin(ang)
    cos_ref[...] = jnp.cos(ang)
    sina_ref[...] = jnp.where((lane >= half) & (lane < ROPE_DIMS), s, 0.0)
    sinb_ref[...] = jnp.where(lane < half, -s, 0.0)


def rope_tables(positions, *, tm):
    n = positions.size
    half = ROPE_DIMS // 2
    inv_freq = ROPE_THETA ** (-jnp.arange(half, dtype=F32) * 2.0 / ROPE_DIMS)
    invf = jnp.zeros((1, LANES), F32).at[0, :half].set(inv_freq).at[0, half:ROPE_DIMS].set(inv_freq)
    pos = positions.reshape(n, 1)
    tab = jax.ShapeDtypeStruct((n, LANES), F32)
    t_spec = pl.BlockSpec((tm, LANES), lambda i: (i, 0))
    return pl.pallas_call(
        _rope_tab_body, grid=(n // tm,),
        in_specs=[pl.BlockSpec((tm, 1), lambda i: (i, 0)), pl.BlockSpec((1, LANES), lambda i: (0, 0))],
        out_specs=[t_spec, t_spec, t_spec], out_shape=[tab, tab, tab],
        compiler_params=_cparams("parallel"), name="rope_tables")(pos, invf)


def _norm_mm_rope_body(h_ref, nw_ref, w_ref, cos_ref, sina_ref, sinb_ref, o_ref, xn_ref, *, q_tiles, k_tiles, q_scale):
    j = pl.program_id(1)

    @pl.when(j == 0)
    def _():
        xn_ref[...] = _rms(h_ref[...], nw_ref[...]).astype(BF16)

    acc = jnp.dot(xn_ref[...], w_ref[...], preferred_element_type=F32)
    half = ROPE_DIMS // 2

    def rotate(scale):
        c, sa, sb = cos_ref[...] * scale, sina_ref[...] * scale, sinb_ref[...] * scale
        for t in range(acc.shape[1] // LANES):
            xj = acc[:, t * LANES:(t + 1) * LANES]
            r = xj * c + pltpu.roll(xj, half, 1) * sa + pltpu.roll(xj, LANES - half, 1) * sb
            o_ref[:, t * LANES:(t + 1) * LANES] = r.astype(o_ref.dtype)

    @pl.when(j < q_tiles)
    def _():
        rotate(q_scale)

    @pl.when((j >= q_tiles) & (j < q_tiles + k_tiles))
    def _():
        rotate(1.0)

    @pl.when(j >= q_tiles + k_tiles)
    def _():
        o_ref[...] = acc.astype(o_ref.dtype)


def norm_matmul_rope(h, nw, w, tabs, *, tm, tn, q_width, k_width, q_scale):
    n, d = h.shape
    nout = w.shape[1]
    body = functools.partial(_norm_mm_rope_body, q_tiles=q_width // tn, k_tiles=k_width // tn, q_scale=q_scale)
    t_spec = pl.BlockSpec((tm, LANES), lambda i, j: (i, 0))
    return pl.pallas_call(
        body, grid=(n // tm, nout // tn),
        in_specs=[pl.BlockSpec((tm, d), lambda i, j: (i, 0)), pl.BlockSpec((1, d), lambda i, j: (0, 0)),
                  pl.BlockSpec((d, tn), lambda i, j: (0, j)), t_spec, t_spec, t_spec],
        out_specs=pl.BlockSpec((tm, tn), lambda i, j: (i, j)),
        out_shape=jax.ShapeDtypeStruct((n, nout), BF16),
        scratch_shapes=[pltpu.VMEM((tm, d), BF16)],
        compiler_params=_cparams("parallel", "arbitrary"), name="norm_matmul_rope")(h, nw, w, *tabs)


def _mm_res_body(a_ref, w_ref, h_ref, o_ref):
    o_ref[...] = h_ref[...] + jnp.dot(a_ref[...], w_ref[...], preferred_element_type=F32)


def matmul_residual(a, w, h, *, tm, tn):
    n, k = a.shape
    d = w.shape[1]
    return pl.pallas_call(
        _mm_res_body, grid=(n // tm, d // tn),
        in_specs=[pl.BlockSpec((tm, k), lambda i, j: (i, 0)), pl.BlockSpec((k, tn), lambda i, j: (0, j)),
                  pl.BlockSpec((tm, tn), lambda i, j: (i, j))],
        out_specs=pl.BlockSpec((tm, tn), lambda i, j: (i, j)),
        out_shape=jax.ShapeDtypeStruct((n, d), F32),
        compiler_params=_cparams("parallel", "arbitrary"), name="matmul_residual")(a, w, h)


def _gla_body(q_ref, k_ref, v_ref, g_ref, a_ref, wa_ref, ba_ref, nw_ref, o_ref, st_ref, gk_ref, *, chunk, q_scale):
    @pl.when(pl.program_id(2) == 0)
    def _():
        st_ref[...] = jnp.zeros_like(st_ref)

    z = jnp.dot(a_ref[...].astype(BF16), wa_ref[...], preferred_element_type=F32) + ba_ref[...]
    gk_ref[...] = (jnp.minimum(z, 0.0) - jnp.log1p(jnp.exp(-jnp.abs(z)))) * (1.0 / GLA_GATE_NORMALIZER)

    row = lax.broadcasted_iota(jnp.int32, (chunk, chunk), 0)
    col = lax.broadcasted_iota(jnp.int32, (chunk, chunk), 1)
    causal = col <= row
    tri = causal.astype(BF16)
    nw = nw_ref[...]

    def step(c, carry):
        r = pl.ds(pl.multiple_of(c * chunk, chunk), chunk)
        g_hi, g_mid, g_lo = _split3(gk_ref[r, :])
        b = (jnp.dot(tri, g_hi, preferred_element_type=F32) + jnp.dot(tri, g_mid, preferred_element_type=F32)
             + jnp.dot(tri, g_lo, preferred_element_type=F32))
        b_last = b[chunk - 1:chunk, :]
        q = q_ref[r, :].astype(F32) * q_scale
        k = k_ref[r, :].astype(F32)
        v = v_ref[r, :]
        q_in = (q * jnp.exp(b)).astype(BF16)
        k_in = (k * jnp.exp(-b)).astype(BF16)
        k_out = (k * jnp.exp(b_last - b)).astype(BF16)
        s = lax.dot_general(q_in, k_in, NT_DIMS, preferred_element_type=F32)
        s = jnp.where(causal, s, 0.0).astype(BF16)
        st = st_ref[...]
        o = (jnp.dot(s, v, preferred_element_type=F32)
             + lax.dot_general(q_in, st.astype(BF16), NT_DIMS, preferred_element_type=F32))
        st_ref[...] = st * jnp.exp(b_last) + lax.dot_general(v, k_out, TN_DIMS, preferred_element_type=F32)
        gate = g_ref[r, :].astype(F32)
        o_ref[r, :] = (_rms(o, nw) * (gate * jax.nn.sigmoid(gate))).astype(o_ref.dtype)
        return carry

    lax.fori_loop(0, q_ref.shape[0] // chunk, step, 0)


def gla_core(qkvg, a, w_a2, b_a2, norm_w, *, batch, seq, tb):
    n = qkvg.shape[0]
    kd_total = w_a2.shape[1]
    dk = kd_total // GLA_HEADS
    vd_total = (qkvg.shape[1] - 2 * kd_total) // 2
    dv = vd_total // GLA_HEADS
    nt = seq // tb
    kq = kd_total // dk
    kv = 2 * kd_total // dv
    kg = kv + vd_total // dv
    body = functools.partial(_gla_body, chunk=GLA_CHUNK, q_scale=dk ** -0.5)
    rows = lambda b, h, t: b * nt + t
    return pl.pallas_call(
        body, grid=(batch, GLA_HEADS, nt),
        in_specs=[pl.BlockSpec((tb, dk), lambda b, h, t: (rows(b, h, t), h)),
                  pl.BlockSpec((tb, dk), lambda b, h, t: (rows(b, h, t), kq + h)),
                  pl.BlockSpec((tb, dv), lambda b, h, t: (rows(b, h, t), kv + h)),
                  pl.BlockSpec((tb, dv), lambda b, h, t: (rows(b, h, t), kg + h)),
                  pl.BlockSpec((tb, LANES), lambda b, h, t: (rows(b, h, t), 0)),
                  pl.BlockSpec((LANES, dk), lambda b, h, t: (0, h)),
                  pl.BlockSpec((1, dk), lambda b, h, t: (0, h)),
                  pl.BlockSpec((1, dv), lambda b, h, t: (0, 0))],
        out_specs=pl.BlockSpec((tb, dv), lambda b, h, t: (rows(b, h, t), h)),
        out_shape=jax.ShapeDtypeStruct((n, vd_total), BF16),
        scratch_shapes=[pltpu.VMEM((dv, dk), F32), pltpu.VMEM((tb, dk), F32)],
        compiler_params=_cparams("parallel", "parallel", "arbitrary"), name="gla_core")(
            qkvg, qkvg, qkvg, qkvg, a, w_a2, b_a2, norm_w)


def _pool_body(h_ref, hp_ref, nw_ref, w_ref, sc_ref, o_ref, x_ref, *, tiles_per_seq, group):
    i = pl.program_id(0)
    tt = h_ref.shape[0]
    h = h_ref[...]
    nw = nw_ref[...]
    first = (i % tiles_per_seq) == 0
    x_ref[0:POOL_HALO, :] = jnp.where(first, 0.0, _rms(hp_ref[...], nw))
    x_ref[POOL_HALO:, :] = _rms(h, nw)
    pos = (i % tiles_per_seq) * tt + lax.broadcasted_iota(jnp.int32, (tt, 1), 0)
    for gi, win in enumerate(POOL_WINDOWS):
        cols = slice(gi * group, (gi + 1) * group)
        acc = x_ref[POOL_HALO:POOL_HALO + tt, cols]
        cur = acc
        for back in range(1, win):
            acc = acc + x_ref[POOL_HALO - back:POOL_HALO - back + tt, cols]
        count = jnp.minimum(pos + 1, win).astype(F32)
        pooled = (acc / count - cur).astype(BF16)
        y = jnp.dot(pooled, w_ref[gi], preferred_element_type=F32)
        o_ref[:, cols] = h[:, cols] + y * sc_ref[:, cols]


def pool_mixer(h, nw, w_groups, scale, *, seq, tt):
    n, d = h.shape
    ng, group, _ = w_groups.shape
    tiles_per_seq = seq // tt
    halo_blocks = tt // POOL_HALO
    body = functools.partial(_pool_body, tiles_per_seq=tiles_per_seq, group=group)
    return pl.pallas_call(
        body, grid=(n // tt,),
        in_specs=[pl.BlockSpec((tt, d), lambda i: (i, 0)),
                  pl.BlockSpec((POOL_HALO, d), lambda i: (jnp.maximum(i * halo_blocks - 1, 0), 0)),
                  pl.BlockSpec((1, d), lambda i: (0, 0)),
                  pl.BlockSpec((ng, group, group), lambda i: (0, 0, 0)),
                  pl.BlockSpec((1, d), lambda i: (0, 0))],
        out_specs=pl.BlockSpec((tt, d), lambda i: (i, 0)),
        out_shape=jax.ShapeDtypeStruct((n, d), F32),
        scratch_shapes=[pltpu.VMEM((tt + POOL_HALO, d), F32)],
        compiler_params=_cparams("parallel"), name="pool_mixer")(h, h, nw, w_groups, scale)


def _diff_body(qt_ref, kt_ref, q_ref, k_ref, v_ref, lq1_ref, lk1_ref, lq2_ref, lk2_ref, nw_ref, o_ref,
               m_ref, l_ref, acc_ref, *, lambda_init):
    step = pl.program_id(2)
    qi = qt_ref[step]
    ki = kt_ref[step]
    tq, tk = q_ref.shape[0], k_ref.shape[0]
    dh = DIFF_HEAD_DIM

    @pl.when(ki == 0)
    def _():
        m_ref[...] = jnp.full_like(m_ref, -jnp.inf)
        l_ref[...] = jnp.zeros_like(l_ref)
        acc_ref[...] = jnp.zeros_like(acc_ref)

    def update(masked):
        v = v_ref[...]
        if masked:
            row = lax.broadcasted_iota(jnp.int32, (tq, tk), 0)
            col = lax.broadcasted_iota(jnp.int32, (tq, tk), 1)
            keep = col <= row
        for i in range(2):
            s = lax.dot_general(q_ref[:, i * dh:(i + 1) * dh], k_ref[:, i * dh:(i + 1) * dh], NT_DIMS,
                                preferred_element_type=F32)
            if masked:
                s = jnp.where(keep, s, -jnp.inf)
            m_old = m_ref[i]
            m_new = jnp.maximum(m_old, jnp.max(s, axis=-1, keepdims=True))
            alpha = jnp.exp(m_old - m_new)
            p = jnp.exp(s - m_new)
            l_ref[i] = alpha * l_ref[i] + jnp.sum(p, axis=-1, keepdims=True)
            acc_ref[i] = alpha * acc_ref[i] + jnp.dot(p.astype(BF16), v, preferred_element_type=F32)
            m_ref[i] = m_new

    @pl.when(ki < qi)
    def _():
        update(False)

    @pl.when(ki == qi)
    def _():
        update(True)
        lam = (jnp.exp(jnp.sum(lq1_ref[...] * lk1_ref[...], axis=-1, keepdims=True))
               - jnp.exp(jnp.sum(lq2_ref[...] * lk2_ref[...], axis=-1, keepdims=True)) + lambda_init)
        o = acc_ref[0] / l_ref[0] - lam * (acc_ref[1] / l_ref[1])
        o_ref[...] = (_rms(o, nw_ref[...]) * (1.0 - lambda_init)).astype(o_ref.dtype)


def diff_attention_core(qkv, lq1, lk1, lq2, lk2, norm_w, *, batch, seq, tq, lambda_init):
    n = qkv.shape[0]
    dh = DIFF_HEAD_DIM
    nq = seq // tq
    pairs = [(a, b) for a in range(nq) for b in range(a + 1)]
    qt = jnp.asarray([a for a, _ in pairs], jnp.int32)
    kt = jnp.asarray([b for _, b in pairs], jnp.int32)
    k_off = DIFF_HEADS
    v_off = 2 * DIFF_HEADS
    vec = pl.BlockSpec((1, dh), lambda b, h, s, qt, kt: (0, 0))
    grid_spec = pltpu.PrefetchScalarGridSpec(
        num_scalar_prefetch=2, grid=(batch, DIFF_HEADS, len(pairs)),
        in_specs=[pl.BlockSpec((tq, 2 * dh), lambda b, h, s, qt, kt: (b * nq + qt[s], h)),
                  pl.BlockSpec((tq, 2 * dh), lambda b, h, s, qt, kt: (b * nq + kt[s], k_off + h)),
                  pl.BlockSpec((tq, 2 * dh), lambda b, h, s, qt, kt: (b * nq + kt[s], v_off + h)),
                  vec, vec, vec, vec,
                  pl.BlockSpec((1, 2 * dh), lambda b, h, s, qt, kt: (0, 0))],
        out_specs=pl.BlockSpec((tq, 2 * dh), lambda b, h, s, qt, kt: (b * nq + qt[s], h)),
        scratch_shapes=[pltpu.VMEM((2, tq, 1), F32), pltpu.VMEM((2, tq, 1), F32), pltpu.VMEM((2, tq, 2 * dh), F32)])
    return pl.pallas_call(
        functools.partial(_diff_body, lambda_init=lambda_init), grid_spec=grid_spec,
        out_shape=jax.ShapeDtypeStruct((n, DIFF_HEADS * 2 * dh), BF16),
        compiler_params=_cparams("parallel", "parallel", "arbitrary"), name="diff_attention")(
            qt, kt, qkv, qkv, qkv, lq1, lk1, lq2, lk2, norm_w)


def _ffn_body(h_ref, nw_ref, wg_ref, wu_ref, wd_ref, o_ref, xn_ref):
    @pl.when(pl.program_id(1) == 0)
    def _():
        h = h_ref[...]
        xn_ref[...] = _rms(h, nw_ref[...]).astype(BF16)
        o_ref[...] = h

    xn = xn_ref[...]
    gate = jnp.dot(xn, wg_ref[...], preferred_element_type=F32)
    up = jnp.dot(xn, wu_ref[...], preferred_element_type=F32)
    act = (gate * jax.nn.sigmoid(gate) * up).astype(BF16)
    o_ref[...] += jnp.dot(act, wd_ref[...], preferred_element_type=F32)


def ffn_residual(h, nw, w_gu, w_down, *, tm, tf):
    n, d = h.shape
    f = w_down.shape[0]
    nf = f // tf
    return pl.pallas_call(
        _ffn_body, grid=(n // tm, nf),
        in_specs=[pl.BlockSpec((tm, d), lambda i, j: (i, 0)), pl.BlockSpec((1, d), lambda i, j: (0, 0)),
                  pl.BlockSpec((d, tf), lambda i, j: (0, j)), pl.BlockSpec((d, tf), lambda i, j: (0, nf + j)),
                  pl.BlockSpec((tf, d), lambda i, j: (j, 0))],
        out_specs=pl.BlockSpec((tm, d), lambda i, j: (i, 0)),
        out_shape=jax.ShapeDtypeStruct((n, d), F32),
        scratch_shapes=[pltpu.VMEM((tm, d), BF16)],
        compiler_params=_cparams("parallel", "arbitrary"), name="ffn_residual")(h, nw, w_gu, w_gu, w_down)


def _router_body(h_ref, nw_ref, r_ref, xn_ref, info_ref, cnt_ref, carry_ref):
    i = pl.program_id(0)

    @pl.when(i == 0)
    def _():
        carry_ref[...] = jnp.zeros_like(carry_ref)

    xn = _rms(h_ref[...], nw_ref[...])
    xn_ref[...] = xn.astype(BF16)
    x_hi, x_mid, x_lo = _split3(xn)
    r_hi, r_mid, r_lo = _split3(r_ref[...])
    dot = functools.partial(jnp.dot, preferred_element_type=F32)
    logits = (dot(x_hi, r_hi) + (dot(x_hi, r_mid) + dot(x_mid, r_hi))
              + (dot(x_hi, r_lo) + dot(x_mid, r_mid) + dot(x_lo, r_hi)))
    tm = logits.shape[0]
    lane = lax.broadcasted_iota(jnp.int32, logits.shape, 1)
    lane_f = lane.astype(F32)
    lg = jnp.where(lane < N_EXPERTS, logits, -jnp.inf)
    m1 = jnp.max(lg, axis=-1, keepdims=True)
    e1 = jnp.min(jnp.where(lg == m1, lane_f, float(LANES)), axis=-1, keepdims=True)
    lg2 = jnp.where(lane_f == e1, -jnp.inf, lg)
    m2 = jnp.max(lg2, axis=-1, keepdims=True)
    e2 = jnp.min(jnp.where(lg2 == m2, lane_f, float(LANES)), axis=-1, keepdims=True)
    ex = jnp.exp(m2 - m1)
    g1 = 1.0 / (1.0 + ex)
    g2 = ex / (1.0 + ex)
    hot1 = lane_f == e1
    hot2 = lane_f == e2
    onehot = jnp.where(hot1 | hot2, 1.0, 0.0)
    row = lax.broadcasted_iota(jnp.int32, (tm, tm), 0)
    col = lax.broadcasted_iota(jnp.int32, (tm, tm), 1)
    before = jnp.where(col < row, 1.0, 0.0).astype(BF16)
    cum = dot(before, onehot.astype(BF16)) + carry_ref[...]
    rank1 = jnp.sum(jnp.where(hot1, cum, 0.0), axis=-1, keepdims=True)
    rank2 = jnp.sum(jnp.where(hot2, cum, 0.0), axis=-1, keepdims=True)
    info = jnp.zeros(logits.shape, F32)
    for idx, val in enumerate((e1, e2, rank1, rank2, g1, g2)):
        info = jnp.where(lane == idx, val, info)
    info_ref[...] = info
    carry_ref[...] += jnp.sum(onehot, axis=0, keepdims=True)
    cnt_ref[...] = jnp.broadcast_to(carry_ref[...], cnt_ref.shape)


def moe_router(h, nw, router_pad, *, tm):
    n, d = h.shape
    return pl.pallas_call(
        _router_body, grid=(n // tm,),
        in_specs=[pl.BlockSpec((tm, d), lambda i: (i, 0)), pl.BlockSpec((1, d), lambda i: (0, 0)),
                  pl.BlockSpec((d, LANES), lambda i: (0, 0))],
        out_specs=[pl.BlockSpec((tm, d), lambda i: (i, 0)), pl.BlockSpec((tm, LANES), lambda i: (i, 0)),
                   pl.BlockSpec((8, LANES), lambda i: (0, 0))],
        out_shape=[jax.ShapeDtypeStruct((n, d), BF16), jax.ShapeDtypeStruct((n, LANES), F32),
                   jax.ShapeDtypeStruct((8, LANES), F32)],
        scratch_shapes=[pltpu.VMEM((1, LANES), F32)],
        compiler_params=_cparams("arbitrary"), name="moe_router")(h, nw, router_pad)


def _dispatch_body(p1_ref, p2_ref, x_ref, xs_in_ref, xs_ref, sem):
    del xs_in_ref
    tr = x_ref.shape[0]
    base = pl.program_id(0) * tr

    def row_copy(r, dst):
        return pltpu.make_async_copy(x_ref.at[pl.ds(r, 1)], xs_ref.at[pl.ds(dst, 1)], sem)

    def issue(r, carry):
        row_copy(r, p1_ref[base + r]).start()
        row_copy(r, p2_ref[base + r]).start()
        return carry

    def drain(r, carry):
        row_copy(r, p1_ref[base + r]).wait()
        row_copy(r, p2_ref[base + r]).wait()
        return carry

    lax.fori_loop(0, tr, issue, 0)
    lax.fori_loop(0, tr, drain, 0)


def moe_dispatch(x32, pos1, pos2, xs_init, *, tr):
    n, w = x32.shape
    grid_spec = pltpu.PrefetchScalarGridSpec(
        num_scalar_prefetch=2, grid=(n // tr,),
        in_specs=[pl.BlockSpec((tr, w), lambda i, p1, p2: (i, 0)), pl.BlockSpec(memory_space=pl.ANY)],
        out_specs=pl.BlockSpec(memory_space=pl.ANY),
        scratch_shapes=[pltpu.SemaphoreType.DMA])
    return pl.pallas_call(
        _dispatch_body, grid_spec=grid_spec, out_shape=jax.ShapeDtypeStruct(xs_init.shape, xs_init.dtype),
        input_output_aliases={3: 0},
        compiler_params=_cparams("arbitrary"), name="moe_dispatch")(pos1, pos2, x32, xs_init)


def _moe_ffn_body(te_ref, na_ref, x_ref, wg_ref, wu_ref, wd_ref, o_ref):
    i = pl.program_id(0)
    j = pl.program_id(1)

    @pl.when(j == 0)
    def _():
        o_ref[...] = jnp.zeros_like(o_ref)

    @pl.when(i < na_ref[0])
    def _():
        x = x_ref[...]
        gate = jnp.dot(x, wg_ref[0], preferred_element_type=F32)
        up = jnp.dot(x, wu_ref[0], preferred_element_type=F32)
        act = (gate * jax.nn.sigmoid(gate) * up).astype(BF16)
        o_ref[...] += jnp.dot(act, wd_ref[0], preferred_element_type=F32)


def moe_ffn(xs, tile_expert, n_active, w_gu, w_down, *, tm, tf):
    p, d = xs.shape
    f = w_down.shape[1]
    nf = f // tf

    def frozen(i, j, na):
        return jnp.where(i < na[0], j, nf - 1)

    grid_spec = pltpu.PrefetchScalarGridSpec(
        num_scalar_prefetch=2, grid=(p // tm, nf),
        in_specs=[pl.BlockSpec((tm, d), lambda i, j, te, na: (jnp.minimum(i, na[0] - 1), 0)),
                  pl.BlockSpec((1, d, tf), lambda i, j, te, na: (te[i], 0, frozen(i, j, na))),
                  pl.BlockSpec((1, d, tf), lambda i, j, te, na: (te[i], 0, nf + frozen(i, j, na))),
                  pl.BlockSpec((1, tf, d), lambda i, j, te, na: (te[i], frozen(i, j, na), 0))],
        out_specs=pl.BlockSpec((tm, d), lambda i, j, te, na: (i, 0)))
    return pl.pallas_call(
        _moe_ffn_body, grid_spec=grid_spec, out_shape=jax.ShapeDtypeStruct((p, d), F32),
        compiler_params=_cparams("parallel", "arbitrary"), name="moe_ffn")(
            tile_expert, n_active, xs, w_gu, w_gu, w_down)


def _combine_body(p1_ref, p2_ref, h_ref, info_ref, y_ref, o_ref, buf1, buf2, sem):
    tr = h_ref.shape[0]
    base = pl.program_id(0) * tr

    def row_copy(src, buf, r):
        return pltpu.make_async_copy(y_ref.at[pl.ds(src, 1)], buf.at[pl.ds(r, 1)], sem)

    def issue(r, carry):
        row_copy(p1_ref[base + r], buf1, r).start()
        row_copy(p2_ref[base + r], buf2, r).start()
        return carry

    def drain(r, carry):
        row_copy(p1_ref[base + r], buf1, r).wait()
        row_copy(p2_ref[base + r], buf2, r).wait()
        return carry

    lax.fori_loop(0, tr, issue, 0)
    lax.fori_loop(0, tr, drain, 0)
    info = info_ref[...]
    g1 = info[:, 4:5]
    g2 = info[:, 5:6]
    o_ref[...] = h_ref[...] + (g1 * buf1[...] + g2 * buf2[...])


def moe_combine(h, info, y, pos1, pos2, *, tr):
    n, d = h.shape
    grid_spec = pltpu.PrefetchScalarGridSpec(
        num_scalar_prefetch=2, grid=(n // tr,),
        in_specs=[pl.BlockSpec((tr, d), lambda i, p1, p2: (i, 0)), pl.BlockSpec((tr, LANES), lambda i, p1, p2: (i, 0)),
                  pl.BlockSpec(memory_space=pl.ANY)],
        out_specs=pl.BlockSpec((tr, d), lambda i, p1, p2: (i, 0)),
        scratch_shapes=[pltpu.VMEM((tr, d), F32), pltpu.VMEM((tr, d), F32), pltpu.SemaphoreType.DMA])
    return pl.pallas_call(
        _combine_body, grid_spec=grid_spec, out_shape=jax.ShapeDtypeStruct((n, d), F32),
        compiler_params=_cparams("arbitrary"), name="moe_combine")(pos1, pos2, h, info, y)


def moe_residual(h, nw, router, w_gu, w_down, *, tm_route, tr, tm, tf):
    n, d = h.shape
    router_pad = jnp.zeros((d, LANES), F32).at[:, :N_EXPERTS].set(router)
    xn, info, counts = moe_router(h, nw, router_pad, tm=tm_route)
    cnt = counts[0, :N_EXPERTS].astype(jnp.int32)
    padded = (cnt + tm - 1) // tm * tm
    ends = jnp.cumsum(padded)
    offs = ends - padded
    e1 = info[:, 0].astype(jnp.int32)
    e2 = info[:, 1].astype(jnp.int32)
    pos1 = offs[e1] + info[:, 2].astype(jnp.int32)
    pos2 = offs[e2] + info[:, 3].astype(jnp.int32)
    p_rows = 2 * n + N_EXPERTS * tm
    n_tiles = p_rows // tm
    tile_start = jnp.arange(n_tiles, dtype=jnp.int32) * tm
    tile_expert = jnp.minimum(jnp.sum(tile_start[:, None] >= ends[None, :], axis=1), N_EXPERTS - 1).astype(jnp.int32)
    n_active = (ends[-1:] // tm).astype(jnp.int32)
    x32 = lax.bitcast_convert_type(xn.reshape(n, d // 2, 2), jnp.uint32)
    xs32 = moe_dispatch(x32, pos1, pos2, jnp.zeros((p_rows, d // 2), jnp.uint32), tr=tr)
    xs = lax.bitcast_convert_type(xs32, BF16).reshape(p_rows, d)
    y = moe_ffn(xs, tile_expert, n_active, w_gu, w_down, tm=tm, tf=tf)
    return moe_combine(h, info, y, pos1, pos2, tr=tr)


def _ple_body(h_ref, hres_ref, nw_ref, wg_ref, bg_ref, p_ref, wp_ref, o_ref, hn_ref):
    @pl.when(pl.program_id(1) == 0)
    def _():
        hn_ref[...] = _rms(h_ref[...], nw_ref[...]).astype(BF16)

    z = jnp.dot(hn_ref[...], wg_ref[...], preferred_element_type=F32) + bg_ref[...]
    proj = jnp.dot(p_ref[...].astype(BF16), wp_ref[...], preferred_element_type=F32)
    o_ref[...] = hres_ref[...] + jax.nn.sigmoid(z) * proj


def ple_residual(h, nw, w_gate, b_gate, p_i, w_proj, *, tm, tn):
    n, d = h.shape
    pd = p_i.shape[1]
    return pl.pallas_call(
        _ple_body, grid=(n // tm, d // tn),
        in_specs=[pl.BlockSpec((tm, d), lambda i, j: (i, 0)), pl.BlockSpec((tm, tn), lambda i, j: (i, j)),
                  pl.BlockSpec((1, d), lambda i, j: (0, 0)), pl.BlockSpec((d, tn), lambda i, j: (0, j)),
                  pl.BlockSpec((1, tn), lambda i, j: (0, j)), pl.BlockSpec((tm, pd), lambda i, j: (i, 0)),
                  pl.BlockSpec((pd, tn), lambda i, j: (0, j))],
        out_specs=pl.BlockSpec((tm, tn), lambda i, j: (i, j)),
        out_shape=jax.ShapeDtypeStruct((n, d), F32),
        scratch_shapes=[pltpu.VMEM((tm, d), BF16)],
        compiler_params=_cparams("parallel", "arbitrary"), name="ple_residual")(h, h, nw, w_gate, b_gate, p_i, w_proj)


def _final_norm_body(h_ref, nw_ref, o_ref):
    o_ref[...] = _rms(h_ref[...], nw_ref[...])


def final_rmsnorm(h, nw, *, tm):
    n, d = h.shape
    return pl.pallas_call(
        _final_norm_body, grid=(n // tm,),
        in_specs=[pl.BlockSpec((tm, d), lambda i: (i, 0)), pl.BlockSpec((1, d), lambda i: (0, 0))],
        out_specs=pl.BlockSpec((tm, d), lambda i: (i, 0)),
        out_shape=jax.ShapeDtypeStruct((n, d), F32),
        compiler_params=_cparams("parallel"), name="final_rmsnorm")(h, nw)


def _row(v):
    return v.reshape(1, -1).astype(F32)


def _tile(n, want):
    return min(n, want)


def _gla_layer(h, norm_mix, w_in, w_a2, b_a2, gla_norm, w_o, *, batch, seq):
    n = h.shape[0]
    main = w_in.shape[1] - GLA_GATE_RANK
    w_main = w_in[:, :main].astype(BF16)
    w_a = jnp.zeros((w_in.shape[0], LANES), BF16).at[:, :GLA_GATE_RANK].set(w_in[:, main:].astype(BF16))
    w_a2_pad = jnp.zeros((LANES, w_a2.shape[1]), BF16).at[:GLA_GATE_RANK].set(w_a2.astype(BF16))
    qkvg, a = norm_matmul(h, _row(norm_mix), w_main, tm=_tile(n, 1024), tn=1024, w_extra=w_a)
    o = gla_core(qkvg, a, w_a2_pad, _row(b_a2), _row(gla_norm), batch=batch, seq=seq, tb=_tile(seq, 512))
    return matmul_residual(o, w_o.astype(BF16), h, tm=_tile(n, 1024), tn=1024)


def _diff_layer(h, positions, norm_mix, w_in, lq1, lk1, lq2, lk2, diff_norm, w_o, *, batch, seq, lambda_init):
    n = h.shape[0]
    tabs = rope_tables(positions, tm=_tile(n, 1024))
    qk_width = 2 * DIFF_HEADS * DIFF_HEAD_DIM
    qkv = norm_matmul_rope(h, _row(norm_mix), w_in.astype(BF16), tabs, tm=_tile(n, 1024), tn=1024,
                           q_width=qk_width, k_width=qk_width, q_scale=DIFF_HEAD_DIM ** -0.5)
    o = diff_attention_core(qkv, _row(lq1), _row(lk1), _row(lq2), _row(lk2), _row(diff_norm),
                            batch=batch, seq=seq, tq=_tile(seq, 512), lambda_init=lambda_init)
    return matmul_residual(o, w_o.astype(BF16), h, tm=_tile(n, 1024), tn=1024)


def _diff_lambda_init(layer_idx):
    return 0.8 - 0.6 * math.exp(-0.3 * layer_idx)


def kernel(x, p, positions, l0_norm_mix, l0_gla_w_in, l0_gla_w_a2, l0_gla_b_a2, l0_gla_norm, l0_gla_w_o, l0_norm_ffn, l0_ffn_w_gu, l0_ffn_w_down, l0_ple_norm, l0_ple_w_gate, l0_ple_b_gate, l0_ple_w_proj, l1_norm_mix, l1_pool_w, l1_pool_scale, l1_norm_ffn, l1_moe_router, l1_moe_w_gu, l1_moe_w_down, l1_ple_norm, l1_ple_w_gate, l1_ple_b_gate, l1_ple_w_proj, l2_norm_mix, l2_diff_w_in, l2_diff_lq1, l2_diff_lk1, l2_diff_lq2, l2_diff_lk2, l2_diff_norm, l2_diff_w_o, l2_norm_ffn, l2_ffn_w_gu, l2_ffn_w_down, l2_ple_norm, l2_ple_w_gate, l2_ple_b_gate, l2_ple_w_proj, l3_norm_mix, l3_gla_w_in, l3_gla_w_a2, l3_gla_b_a2, l3_gla_norm, l3_gla_w_o, l3_norm_ffn, l3_moe_router, l3_moe_w_gu, l3_moe_w_down, l3_ple_norm, l3_ple_w_gate, l3_ple_b_gate, l3_ple_w_proj, final_norm):
    batch, seq, d = x.shape
    n = batch * seq
    h = x.reshape(n, d)
    pf = p.reshape(p.shape[0], n, p.shape[-1])
    tm = _tile(n, 512)

    def ffn(h, nw, w_gu, w_down):
        return ffn_residual(h, _row(nw), w_gu.astype(BF16), w_down.astype(BF16), tm=tm, tf=512)

    def moe(h, nw, router, w_gu, w_down):
        return moe_residual(h, _row(nw), router, w_gu.astype(BF16), w_down.astype(BF16),
                            tm_route=tm, tr=_tile(n, 256), tm=tm, tf=512)

    def ple(h, i, nw, w_gate, b_gate, w_proj):
        return ple_residual(h, _row(nw), w_gate.astype(BF16), _row(b_gate), pf[i], w_proj.astype(BF16),
                            tm=_tile(n, 1024), tn=1024)

    h = _gla_layer(h, l0_norm_mix, l0_gla_w_in, l0_gla_w_a2, l0_gla_b_a2, l0_gla_norm, l0_gla_w_o, batch=batch, seq=seq)
    h = ffn(h, l0_norm_ffn, l0_ffn_w_gu, l0_ffn_w_down)
    h = ple(h, 0, l0_ple_norm, l0_ple_w_gate, l0_ple_b_gate, l0_ple_w_proj)

    h = pool_mixer(h, _row(l1_norm_mix), l1_pool_w.astype(BF16), _row(l1_pool_scale), seq=seq, tt=_tile(seq, 512))
    h = moe(h, l1_norm_ffn, l1_moe_router, l1_moe_w_gu, l1_moe_w_down)
    h = ple(h, 1, l1_ple_norm, l1_ple_w_gate, l1_ple_b_gate, l1_ple_w_proj)

    h = _diff_layer(h, positions, l2_norm_mix, l2_diff_w_in, l2_diff_lq1, l2_diff_lk1, l2_diff_lq2, l2_diff_lk2,
                    l2_diff_norm, l2_diff_w_o, batch=batch, seq=seq, lambda_init=_diff_lambda_init(2))
    h = ffn(h, l2_norm_ffn, l2_ffn_w_gu, l2_ffn_w_down)
    h = ple(h, 2, l2_ple_norm, l2_ple_w_gate, l2_ple_b_gate, l2_ple_w_proj)

    h = _gla_layer(h, l3_norm_mix, l3_gla_w_in, l3_gla_w_a2, l3_gla_b_a2, l3_gla_norm, l3_gla_w_o, batch=batch, seq=seq)
    h = moe(h, l3_norm_ffn, l3_moe_router, l3_moe_w_gu, l3_moe_w_down)
    h = ple(h, 3, l3_ple_norm, l3_ple_w_gate, l3_ple_b_gate, l3_ple_w_proj)

    return final_rmsnorm(h, _row(final_norm), tm=tm).reshape(batch, seq, d)
```

```python
import functools
import math

import jax
import jax.numpy as jnp
from jax import lax
from jax.experimental import pallas as pl
from jax.experimental.pallas import tpu as pltpu

F32 = jnp.float32
BF16 = jnp.bfloat16

NORM_EPS = 1e-6
LANES = 128
V7X_VMEM_BYTES = 64 * 1024 * 1024
VMEM_LIMIT = V7X_VMEM_BYTES * 7 // 8

GLA_HEADS = 4
GLA_GATE_RANK = 16
GLA_GATE_NORMALIZER = 16.0
GLA_CHUNK = 64
POOL_WINDOWS = (2, 4, 8, 16)
POOL_HALO = 16
DIFF_HEADS = 8
DIFF_HEAD_DIM = 128
ROPE_THETA = 500000.0
ROPE_DIMS = 32
N_EXPERTS = 8

NT_DIMS = (((1,), (1,)), ((), ()))
TN_DIMS = (((0,), (0,)), ((), ()))


def _cparams(*sem):
    return pltpu.CompilerParams(dimension_semantics=sem, vmem_limit_bytes=VMEM_LIMIT)


def _rms(x, w):
    ms = jnp.mean(x * x, axis=-1, keepdims=True)
    return x * lax.rsqrt(ms + NORM_EPS) * w


def _split3(x):
    hi = x.astype(BF16)
    r1 = x - hi.astype(F32)
    mid = r1.astype(BF16)
    lo = (r1 - mid.astype(F32)).astype(BF16)
    return hi, mid, lo


def _norm_mm_body(h_ref, nw_ref, w_ref, o_ref, xn_ref):
    @pl.when(pl.program_id(1) == 0)
    def _():
        xn_ref[...] = _rms(h_ref[...], nw_ref[...]).astype(BF16)

    o_ref[...] = jnp.dot(xn_ref[...], w_ref[...], preferred_element_type=F32).astype(o_ref.dtype)


def _norm_mm_extra_body(h_ref, nw_ref, w_ref, wx_ref, o_ref, ox_ref, xn_ref):
    @pl.when(pl.program_id(1) == 0)
    def _():
        xn = _rms(h_ref[...], nw_ref[...]).astype(BF16)
        xn_ref[...] = xn
        ox_ref[...] = jnp.dot(xn, wx_ref[...], preferred_element_type=F32)

    o_ref[...] = jnp.dot(xn_ref[...], w_ref[...], preferred_element_type=F32).astype(o_ref.dtype)


def norm_matmul(h, nw, w, *, tm, tn, w_extra=None):
    n, d = h.shape
    nout = w.shape[1]
    grid = (n // tm, nout // tn)
    h_spec = pl.BlockSpec((tm, d), lambda i, j: (i, 0))
    nw_spec = pl.BlockSpec((1, d), lambda i, j: (0, 0))
    w_spec = pl.BlockSpec((d, tn), lambda i, j: (0, j))
    o_spec = pl.BlockSpec((tm, tn), lambda i, j: (i, j))
    scratch = [pltpu.VMEM((tm, d), BF16)]
    if w_extra is None:
        return pl.pallas_call(
            _norm_mm_body, grid=grid, in_specs=[h_spec, nw_spec, w_spec], out_specs=o_spec,
            out_shape=jax.ShapeDtypeStruct((n, nout), BF16), scratch_shapes=scratch,
            compiler_params=_cparams("parallel", "arbitrary"), name="norm_matmul")(h, nw, w)
    nx = w_extra.shape[1]
    return pl.pallas_call(
        _norm_mm_extra_body, grid=grid,
        in_specs=[h_spec, nw_spec, w_spec, pl.BlockSpec((d, nx), lambda i, j: (0, 0))],
        out_specs=[o_spec, pl.BlockSpec((tm, nx), lambda i, j: (i, 0))],
        out_shape=[jax.ShapeDtypeStruct((n, nout), BF16), jax.ShapeDtypeStruct((n, nx), F32)],
        scratch_shapes=scratch,
        compiler_params=_cparams("parallel", "arbitrary"), name="norm_matmul_extra")(h, nw, w, w_extra)


def _rope_tab_body(pos_ref, invf_ref, cos_ref, sina_ref, sinb_ref):
    ang = pos_ref[...].astype(F32) * invf_ref[...]
    lane = lax.broadcasted_iota(jnp.int32, ang.shape, 1)
    half = ROPE_DIMS // 2
    s = jnp.sin(ang)
    cos_ref[...] = jnp.cos(ang)
    sina_ref[...] = jnp.where((lane >= half) & (lane < ROPE_DIMS), s, 0.0)
    sinb_ref[...] = jnp.where(lane < half, -s, 0.0)


def rope_tables(positions, *, tm):
    n = positions.size
    half = ROPE_DIMS // 2
    inv_freq = ROPE_THETA ** (-jnp.arange(half, dtype=F32) * 2.0 / ROPE_DIMS)
    invf = jnp.zeros((1, LANES), F32).at[0, :half].set(inv_freq).at[0, half:ROPE_DIMS].set(inv_freq)
    pos = positions.reshape(n, 1)
    tab = jax.ShapeDtypeStruct((n, LANES), F32)
    t_spec = pl.BlockSpec((tm, LANES), lambda i: (i, 0))
    return pl.pallas_call(
        _rope_tab_body, grid=(n // tm,),
        in_specs=[pl.BlockSpec((tm, 1), lambda i: (i, 0)), pl.BlockSpec((1, LANES), lambda i: (0, 0))],
        out_specs=[t_spec, t_spec, t_spec], out_shape=[tab, tab, tab],
        compiler_params=_cparams("parallel"), name="rope_tables")(pos, invf)


def _norm_mm_rope_body(h_ref, nw_ref, w_ref, cos_ref, sina_ref, sinb_ref, o_ref, xn_ref, *, q_tiles, k_tiles, q_scale):
    j = pl.program_id(1)

    @pl.when(j == 0)
    def _():
        xn_ref[...] = _rms(h_ref[...], nw_ref[...]).astype(BF16)

    acc = jnp.dot(xn_ref[...], w_ref[...], preferred_element_type=F32)
    half = ROPE_DIMS // 2

    def rotate(scale):
        c, sa, sb = cos_ref[...] * scale, sina_ref[...] * scale, sinb_ref[...] * scale
        for t in range(acc.shape[1] // LANES):
            xj = acc[:, t * LANES:(t + 1) * LANES]
            r = xj * c + pltpu.roll(xj, half, 1) * sa + pltpu.roll(xj, LANES - half, 1) * sb
            o_ref[:, t * LANES:(t + 1) * LANES] = r.astype(o_ref.dtype)

    @pl.when(j < q_tiles)
    def _():
        rotate(q_scale)

    @pl.when((j >= q_tiles) & (j < q_tiles + k_tiles))
    def _():
        rotate(1.0)

    @pl.when(j >= q_tiles + k_tiles)
    def _():
        o_ref[...] = acc.astype(o_ref.dtype)


def norm_matmul_rope(h, nw, w, tabs, *, tm, tn, q_width, k_width, q_scale):
    n, d = h.shape
    nout = w.shape[1]
    body = functools.partial(_norm_mm_rope_body, q_tiles=q_width // tn, k_tiles=k_width // tn, q_scale=q_scale)
    t_spec = pl.BlockSpec((tm, LANES), lambda i, j: (i, 0))
    return pl.pallas_call(
        body, grid=(n // tm, nout // tn),
        in_specs=[pl.BlockSpec((tm, d), lambda i, j: (i, 0)), pl.BlockSpec((1, d), lambda i, j: (0, 0)),
                  pl.BlockSpec((d, tn), lambda i, j: (0, j)), t_spec, t_spec, t_spec],
        out_specs=pl.BlockSpec((tm, tn), lambda i, j: (i, j)),
        out_shape=jax.ShapeDtypeStruct((n, nout), BF16),
        scratch_shapes=[pltpu.VMEM((tm, d), BF16)],
        compiler_params=_cparams("parallel", "arbitrary"), name="norm_matmul_rope")(h, nw, w, *tabs)


def _mm_res_body(a_ref, w_ref, h_ref, o_ref):
    o_ref[...] = h_ref[...] + jnp.dot(a_ref[...], w_ref[...], preferred_element_type=F32)


def matmul_residual(a, w, h, *, tm, tn):
    n, k = a.shape
    d = w.shape[1]
    return pl.pallas_call(
        _mm_res_body, grid=(n // tm, d // tn),
        in_specs=[pl.BlockSpec((tm, k), lambda i, j: (i, 0)), pl.BlockSpec((k, tn), lambda i, j: (0, j)),
                  pl.BlockSpec((tm, tn), lambda i, j: (i, j))],
        out_specs=pl.BlockSpec((tm, tn), lambda i, j: (i, j)),
        out_shape=jax.ShapeDtypeStruct((n, d), F32),
        compiler_params=_cparams("parallel", "arbitrary"), name="matmul_residual")(a, w, h)


def _gla_body(q_ref, k_ref, v_ref, g_ref, a_ref, wa_ref, ba_ref, nw_ref, o_ref, st_ref, gk_ref, *, chunk, q_scale):
    @pl.when(pl.program_id(2) == 0)
    def _():
        st_ref[...] = jnp.zeros_like(st_ref)

    z = jnp.dot(a_ref[...].astype(BF16), wa_ref[...], preferred_element_type=F32) + ba_ref[...]
    gk_ref[...] = (jnp.minimum(z, 0.0) - jnp.log1p(jnp.exp(-jnp.abs(z)))) * (1.0 / GLA_GATE_NORMALIZER)

    row = lax.broadcasted_iota(jnp.int32, (chunk, chunk), 0)
    col = lax.broadcasted_iota(jnp.int32, (chunk, chunk), 1)
    causal = col <= row
    tri = causal.astype(BF16)
    nw = nw_ref[...]

    def step(c, carry):
        r = pl.ds(pl.multiple_of(c * chunk, chunk), chunk)
        g_hi, g_mid, g_lo = _split3(gk_ref[r, :])
        b = (jnp.dot(tri, g_hi, preferred_element_type=F32) + jnp.dot(tri, g_mid, preferred_element_type=F32)
             + jnp.dot(tri, g_lo, preferred_element_type=F32))
        b_last = b[chunk - 1:chunk, :]
        q = q_ref[r, :].astype(F32) * q_scale
        k = k_ref[r, :].astype(F32)
        v = v_ref[r, :]
        q_in = (q * jnp.exp(b)).astype(BF16)
        k_in = (k * jnp.exp(-b)).astype(BF16)
        k_out = (k * jnp.exp(b_last - b)).astype(BF16)
        s = lax.dot_general(q_in, k_in, NT_DIMS, preferred_element_type=F32)
        s = jnp.where(causal, s, 0.0).astype(BF16)
        st = st_ref[...]
        o = (jnp.dot(s, v, preferred_element_type=F32)
             + lax.dot_general(q_in, st.astype(BF16), NT_DIMS, preferred_element_type=F32))
        st_ref[...] = st * jnp.exp(b_last) + lax.dot_general(v, k_out, TN_DIMS, preferred_element_type=F32)
        gate = g_ref[r, :].astype(F32)
        o_ref[r, :] = (_rms(o, nw) * (gate * jax.nn.sigmoid(gate))).astype(o_ref.dtype)
        return carry

    lax.fori_loop(0, q_ref.shape[0] // chunk, step, 0)


def gla_core(qkvg, a, w_a2, b_a2, norm_w, *, batch, seq, tb):
    n = qkvg.shape[0]
    kd_total = w_a2.shape[1]
    dk = kd_total // GLA_HEADS
    vd_total = (qkvg.shape[1] - 2 * kd_total) // 2
    dv = vd_total // GLA_HEADS
    nt = seq // tb
    kq = kd_total // dk
    kv = 2 * kd_total // dv
    kg = kv + vd_total // dv
    body = functools.partial(_gla_body, chunk=GLA_CHUNK, q_scale=dk ** -0.5)
    rows = lambda b, h, t: b * nt + t
    return pl.pallas_call(
        body, grid=(batch, GLA_HEADS, nt),
        in_specs=[pl.BlockSpec((tb, dk), lambda b, h, t: (rows(b, h, t), h)),
                  pl.BlockSpec((tb, dk), lambda b, h, t: (rows(b, h, t), kq + h)),
                  pl.BlockSpec((tb, dv), lambda b, h, t: (rows(b, h, t), kv + h)),
                  pl.BlockSpec((tb, dv), lambda b, h, t: (rows(b, h, t), kg + h)),
                  pl.BlockSpec((tb, LANES), lambda b, h, t: (rows(b, h, t), 0)),
                  pl.BlockSpec((LANES, dk), lambda b, h, t: (0, h)),
                  pl.BlockSpec((1, dk), lambda b, h, t: (0, h)),
                  pl.BlockSpec((1, dv), lambda b, h, t: (0, 0))],
        out_specs=pl.BlockSpec((tb, dv), lambda b, h, t: (rows(b, h, t), h)),
        out_shape=jax.ShapeDtypeStruct((n, vd_total), BF16),
        scratch_shapes=[pltpu.VMEM((dv, dk), F32), pltpu.VMEM((tb, dk), F32)],
        compiler_params=_cparams("parallel", "parallel", "arbitrary"), name="gla_core")(
            qkvg, qkvg, qkvg, qkvg, a, w_a2, b_a2, norm_w)


def _pool_body(h_ref, hp_ref, nw_ref, w_ref, sc_ref, o_ref, x_ref, *, tiles_per_seq, group):
    i = pl.program_id(0)
    tt = h_ref.shape[0]
    h = h_ref[...]
    nw = nw_ref[...]
    first = (i % tiles_per_seq) == 0
    x_ref[0:POOL_HALO, :] = jnp.where(first, 0.0, _rms(hp_ref[...], nw))
    x_ref[POOL_HALO:, :] = _rms(h, nw)
    pos = (i % tiles_per_seq) * tt + lax.broadcasted_iota(jnp.int32, (tt, 1), 0)
    for gi, win in enumerate(POOL_WINDOWS):
        cols = slice(gi * group, (gi + 1) * group)
        acc = x_ref[POOL_HALO:POOL_HALO + tt, cols]
        cur = acc
        for back in range(1, win):
            acc = acc + x_ref[POOL_HALO - back:POOL_HALO - back + tt, cols]
        count = jnp.minimum(pos + 1, win).astype(F32)
        pooled = (acc / count - cur).astype(BF16)
        y = jnp.dot(pooled, w_ref[gi], preferred_element_type=F32)
        o_ref[:, cols] = h[:, cols] + y * sc_ref[:, cols]


def pool_mixer(h, nw, w_groups, scale, *, seq, tt):
    n, d = h.shape
    ng, group, _ = w_groups.shape
    tiles_per_seq = seq // tt
    halo_blocks = tt // POOL_HALO
    body = functools.partial(_pool_body, tiles_per_seq=tiles_per_seq, group=group)
    return pl.pallas_call(
        body, grid=(n // tt,),
        in_specs=[pl.BlockSpec((tt, d), lambda i: (i, 0)),
                  pl.BlockSpec((POOL_HALO, d), lambda i: (jnp.maximum(i * halo_blocks - 1, 0), 0)),
                  pl.BlockSpec((1, d), lambda i: (0, 0)),
                  pl.BlockSpec((ng, group, group), lambda i: (0, 0, 0)),
                  pl.BlockSpec((1, d), lambda i: (0, 0))],
        out_specs=pl.BlockSpec((tt, d), lambda i: (i, 0)),
        out_shape=jax.ShapeDtypeStruct((n, d), F32),
        scratch_shapes=[pltpu.VMEM((tt + POOL_HALO, d), F32)],
        compiler_params=_cparams("parallel"), name="pool_mixer")(h, h, nw, w_groups, scale)


def _diff_body(q_ref, k_ref, v_ref, lq1_ref, lk1_ref, lq2_ref, lk2_ref, nw_ref, o_ref,
               qt_ref, vt_ref, sa_ref, sb_ref, m_ref, l_ref, acc_ref, *, lambda_init):
    qi = pl.program_id(2)
    tq = q_ref.shape[0]
    dh = DIFF_HEAD_DIM
    nk = vt_ref.shape[0]

    @pl.when(qi == 0)
    def _():
        def transpose_v(j, carry):
            r = pl.ds(pl.multiple_of(j * tq, tq), tq)
            vt_ref[j] = v_ref[r, :].astype(F32).T.astype(BF16)
            return carry

        lax.fori_loop(0, nk, transpose_v, 0)

    qt_ref[...] = q_ref[...].astype(F32).T.astype(BF16)
    m_ref[...] = jnp.full_like(m_ref, -jnp.inf)
    l_ref[...] = jnp.zeros_like(l_ref)
    acc_ref[...] = jnp.zeros_like(acc_ref)

    def scores(j, s_ref):
        k = k_ref[pl.ds(pl.multiple_of(j * tq, tq), tq), :]
        for i in range(2):
            s_ref[i] = jnp.dot(k[:, i * dh:(i + 1) * dh], qt_ref[i * dh:(i + 1) * dh, :],
                               preferred_element_type=F32)

    def consume(j, s_ref, masked):
        vt = vt_ref[j]
        if masked:
            key = lax.broadcasted_iota(jnp.int32, (tq, tq), 0)
            qry = lax.broadcasted_iota(jnp.int32, (tq, tq), 1)
            keep = key <= qry
        for i in range(2):
            s = s_ref[i]
            if masked:
                s = jnp.where(keep, s, -jnp.inf)
            m_old = m_ref[i]
            m_new = jnp.maximum(m_old, jnp.max(s, axis=0, keepdims=True))
            alpha = jnp.exp2(m_old - m_new)
            p = jnp.exp2(s - m_new)
            l_ref[i] = alpha * l_ref[i] + jnp.sum(p, axis=0, keepdims=True)
            acc_ref[i] = alpha * acc_ref[i] + jnp.dot(vt, p.astype(BF16), preferred_element_type=F32)
            m_ref[i] = m_new

    def pipelined_pair(t, carry):
        j = 2 * t
        scores(j + 1, sb_ref)
        consume(j, sa_ref, False)
        scores(j + 2, sa_ref)
        consume(j + 1, sb_ref, False)
        return carry

    scores(0, sa_ref)
    lax.fori_loop(0, qi // 2, pipelined_pair, 0)

    @pl.when(qi % 2 == 0)
    def _():
        consume(qi, sa_ref, True)

    @pl.when(qi % 2 == 1)
    def _():
        scores(qi, sb_ref)
        consume(qi - 1, sa_ref, False)
        consume(qi, sb_ref, True)

    lam = (jnp.exp(jnp.sum(lq1_ref[...] * lk1_ref[...], axis=-1, keepdims=True))
           - jnp.exp(jnp.sum(lq2_ref[...] * lk2_ref[...], axis=-1, keepdims=True)) + lambda_init)
    ot = acc_ref[0] * (1.0 / l_ref[0]) - lam * (acc_ref[1] * (1.0 / l_ref[1]))
    ms = jnp.mean(ot * ot, axis=0, keepdims=True)
    ot = ot * lax.rsqrt(ms + NORM_EPS) * (nw_ref[...] * (1.0 - lambda_init))
    o_ref[...] = ot.T.astype(o_ref.dtype)


def diff_attention_core(qkv, lq1, lk1, lq2, lk2, norm_col, *, batch, seq, tq, lambda_init):
    n = qkv.shape[0]
    dh = DIFF_HEAD_DIM
    nq = seq // tq
    k_off = DIFF_HEADS
    v_off = 2 * DIFF_HEADS
    vec = pl.BlockSpec((1, dh), lambda b, h, q: (0, 0))
    return pl.pallas_call(
        functools.partial(_diff_body, lambda_init=lambda_init), grid=(batch, DIFF_HEADS, nq),
        in_specs=[pl.BlockSpec((tq, 2 * dh), lambda b, h, q: (b * nq + q, h)),
                  pl.BlockSpec((seq, 2 * dh), lambda b, h, q: (b, k_off + h)),
                  pl.BlockSpec((seq, 2 * dh), lambda b, h, q: (b, v_off + h)),
                  vec, vec, vec, vec,
                  pl.BlockSpec((2 * dh, 1), lambda b, h, q: (0, 0))],
        out_specs=pl.BlockSpec((tq, 2 * dh), lambda b, h, q: (b * nq + q, h)),
        out_shape=jax.ShapeDtypeStruct((n, DIFF_HEADS * 2 * dh), BF16),
        scratch_shapes=[pltpu.VMEM((2 * dh, tq), BF16), pltpu.VMEM((nq, 2 * dh, tq), BF16),
                        pltpu.VMEM((2, tq, tq), F32), pltpu.VMEM((2, tq, tq), F32), pltpu.VMEM((2, 1, tq), F32), pltpu.VMEM((2, 1, tq), F32), pltpu.VMEM((2, 2 * dh, tq), F32)],
        compiler_params=_cparams("parallel", "parallel", "arbitrary"), name="diff_attention")(
            qkv, qkv, qkv, lq1, lk1, lq2, lk2, norm_col)


def _ffn_body(h_ref, nw_ref, wg_ref, wu_ref, wd_ref, o_ref, xn_ref):
    @pl.when(pl.program_id(1) == 0)
    def _():
        h = h_ref[...]
        xn_ref[...] = _rms(h, nw_ref[...]).astype(BF16)
        o_ref[...] = h

    xn = xn_ref[...]
    gate = jnp.dot(xn, wg_ref[...], preferred_element_type=F32)
    up = jnp.dot(xn, wu_ref[...], preferred_element_type=F32)
    act = (gate * jax.nn.sigmoid(gate) * up).astype(BF16)
    o_ref[...] += jnp.dot(act, wd_ref[...], preferred_element_type=F32)


def ffn_residual(h, nw, w_gu, w_down, *, tm, tf):
    n, d = h.shape
    f = w_down.shape[0]
    nf = f // tf
    return pl.pallas_call(
        _ffn_body, grid=(n // tm, nf),
        in_specs=[pl.BlockSpec((tm, d), lambda i, j: (i, 0)), pl.BlockSpec((1, d), lambda i, j: (0, 0)),
                  pl.BlockSpec((d, tf), lambda i, j: (0, j)), pl.BlockSpec((d, tf), lambda i, j: (0, nf + j)),
                  pl.BlockSpec((tf, d), lambda i, j: (j, 0))],
        out_specs=pl.BlockSpec((tm, d), lambda i, j: (i, 0)),
        out_shape=jax.ShapeDtypeStruct((n, d), F32),
        scratch_shapes=[pltpu.VMEM((tm, d), BF16)],
        compiler_params=_cparams("parallel", "arbitrary"), name="ffn_residual")(h, nw, w_gu, w_gu, w_down)


def _pack_bf16_pairs(x):
    w = x.shape[1] // 2
    lo = pltpu.bitcast(x[:, :w].astype(BF16).astype(F32), jnp.uint32) >> 16
    hi = pltpu.bitcast(x[:, w:].astype(BF16).astype(F32), jnp.uint32)
    return hi | lo


def _unpack_bf16_pairs(u):
    lo = pltpu.bitcast(u << 16, F32).astype(BF16)
    hi = pltpu.bitcast(u & jnp.uint32(0xFFFF0000), F32).astype(BF16)
    return jnp.concatenate([lo, hi], axis=1)


def _router_body(h_ref, nw_ref, r_ref, xn_ref, info_ref, cnt_ref, carry_ref):
    i = pl.program_id(0)

    @pl.when(i == 0)
    def _():
        carry_ref[...] = jnp.zeros_like(carry_ref)

    xn = _rms(h_ref[...], nw_ref[...])
    xn_ref[...] = _pack_bf16_pairs(xn)
    x_hi, x_mid, x_lo = _split3(xn)
    r_hi, r_mid, r_lo = _split3(r_ref[...])
    dot = functools.partial(jnp.dot, preferred_element_type=F32)
    logits = (dot(x_hi, r_hi) + (dot(x_hi, r_mid) + dot(x_mid, r_hi))
              + (dot(x_hi, r_lo) + dot(x_mid, r_mid) + dot(x_lo, r_hi)))
    tm = logits.shape[0]
    lane = lax.broadcasted_iota(jnp.int32, logits.shape, 1)
    lane_f = lane.astype(F32)
    lg = jnp.where(lane < N_EXPERTS, logits, -jnp.inf)
    m1 = jnp.max(lg, axis=-1, keepdims=True)
    e1 = jnp.min(jnp.where(lg == m1, lane_f, float(LANES)), axis=-1, keepdims=True)
    lg2 = jnp.where(lane_f == e1, -jnp.inf, lg)
    m2 = jnp.max(lg2, axis=-1, keepdims=True)
    e2 = jnp.min(jnp.where(lg2 == m2, lane_f, float(LANES)), axis=-1, keepdims=True)
    ex = jnp.exp(m2 - m1)
    g1 = 1.0 / (1.0 + ex)
    g2 = ex / (1.0 + ex)
    hot1 = lane_f == e1
    hot2 = lane_f == e2
    onehot = jnp.where(hot1 | hot2, 1.0, 0.0)
    row = lax.broadcasted_iota(jnp.int32, (tm, tm), 0)
    col = lax.broadcasted_iota(jnp.int32, (tm, tm), 1)
    before = jnp.where(col < row, 1.0, 0.0).astype(BF16)
    cum = dot(before, onehot.astype(BF16)) + carry_ref[...]
    rank1 = jnp.sum(jnp.where(hot1, cum, 0.0), axis=-1, keepdims=True)
    rank2 = jnp.sum(jnp.where(hot2, cum, 0.0), axis=-1, keepdims=True)
    info = jnp.zeros(logits.shape, F32)
    for idx, val in enumerate((e1, e2, rank1, rank2, g1, g2)):
        info = jnp.where(lane == idx, val, info)
    info_ref[...] = info
    carry_ref[...] += jnp.sum(onehot, axis=0, keepdims=True)
    cnt_ref[...] = jnp.broadcast_to(carry_ref[...], cnt_ref.shape)


def moe_router(h, nw, router_pad, *, tm):
    n, d = h.shape
    return pl.pallas_call(
        _router_body, grid=(n // tm,),
        in_specs=[pl.BlockSpec((tm, d), lambda i: (i, 0)), pl.BlockSpec((1, d), lambda i: (0, 0)),
                  pl.BlockSpec((d, LANES), lambda i: (0, 0))],
        out_specs=[pl.BlockSpec((tm, d // 2), lambda i: (i, 0)), pl.BlockSpec((tm, LANES), lambda i: (i, 0)),
                   pl.BlockSpec((8, LANES), lambda i: (0, 0))],
        out_shape=[jax.ShapeDtypeStruct((n, d // 2), jnp.uint32), jax.ShapeDtypeStruct((n, LANES), F32),
                   jax.ShapeDtypeStruct((8, LANES), F32)],
        scratch_shapes=[pltpu.VMEM((1, LANES), F32)],
        compiler_params=_cparams("arbitrary"), name="moe_router")(h, nw, router_pad)


def _dispatch_body(p1_ref, p2_ref, ps_ref, pn_ref, ts_ref, tn_ref, x_ref, xs_ref, zero_ref, sem, zsem):
    tr = x_ref.shape[0]
    zr = zero_ref.shape[0]
    base = pl.program_id(0) * tr

    @pl.when(pl.program_id(0) == 0)
    def _():
        zero_ref[...] = jnp.zeros_like(zero_ref)

        def pad_copy(dst):
            return pltpu.make_async_copy(zero_ref.at[pl.ds(0, 1)], xs_ref.at[pl.ds(dst, 1)], zsem)

        def tail_copy(c):
            return pltpu.make_async_copy(zero_ref, xs_ref.at[pl.ds(pl.multiple_of(ts_ref[0] + c * zr, zr), zr)], zsem)

        for e in range(N_EXPERTS):
            lax.fori_loop(0, pn_ref[e], lambda r, c, e=e: (pad_copy(ps_ref[e] + r).start(), c)[1], 0)
        lax.fori_loop(0, tn_ref[0], lambda c, carry: (tail_copy(c).start(), carry)[1], 0)
        for e in range(N_EXPERTS):
            lax.fori_loop(0, pn_ref[e], lambda r, c, e=e: (pad_copy(ps_ref[e] + r).wait(), c)[1], 0)
        lax.fori_loop(0, tn_ref[0], lambda c, carry: (tail_copy(c).wait(), carry)[1], 0)

    def row_copy(r, dst):
        return pltpu.make_async_copy(x_ref.at[pl.ds(r, 1)], xs_ref.at[pl.ds(dst, 1)], sem)

    def issue(r, carry):
        row_copy(r, p1_ref[base + r]).start()
        row_copy(r, p2_ref[base + r]).start()
        return carry

    def drain(r, carry):
        row_copy(r, p1_ref[base + r]).wait()
        row_copy(r, p2_ref[base + r]).wait()
        return carry

    lax.fori_loop(0, tr, issue, 0)
    lax.fori_loop(0, tr, drain, 0)


def moe_dispatch(x32, pos1, pos2, pad_start, pad_len, tail_start, tail_chunks, *, p_rows, tr, zr):
    n, w = x32.shape
    grid_spec = pltpu.PrefetchScalarGridSpec(
        num_scalar_prefetch=6, grid=(n // tr,),
        in_specs=[pl.BlockSpec((tr, w), lambda i, *_: (i, 0))],
        out_specs=pl.BlockSpec(memory_space=pl.ANY),
        scratch_shapes=[pltpu.VMEM((zr, w), x32.dtype), pltpu.SemaphoreType.DMA, pltpu.SemaphoreType.DMA])
    return pl.pallas_call(
        _dispatch_body, grid_spec=grid_spec, out_shape=jax.ShapeDtypeStruct((p_rows, w), x32.dtype),
        compiler_params=_cparams("arbitrary"), name="moe_dispatch")(
            pos1, pos2, pad_start, pad_len, tail_start, tail_chunks, x32)


def _moe_ffn_body(te_ref, nv_ref, na_ref, x_ref, wg_ref, wu_ref, wd_ref, o_ref, xb_ref):
    i = pl.program_id(0)
    j = pl.program_id(1)
    half_rows = x_ref.shape[0] // 2

    @pl.when(j == 0)
    def _():
        o_ref[...] = jnp.zeros_like(o_ref)
        xb_ref[...] = _unpack_bf16_pairs(x_ref[...])

    @pl.when(i < na_ref[0])
    def _():
        wg = wg_ref[0].astype(BF16)
        wu = wu_ref[0].astype(BF16)
        wd = wd_ref[0].astype(BF16)

        def half(r0):
            x = xb_ref[r0:r0 + half_rows, :]
            gate = jnp.dot(x, wg, preferred_element_type=F32)
            up = jnp.dot(x, wu, preferred_element_type=F32)
            act = (gate * jax.nn.sigmoid(gate) * up).astype(BF16)
            o_ref[r0:r0 + half_rows, :] += jnp.dot(act, wd, preferred_element_type=F32)

        half(0)

        @pl.when(nv_ref[i] > half_rows)
        def _():
            half(half_rows)


def moe_ffn(xs32, tile_expert, tile_valid, n_active, w_gu, w_down, *, tm, tf):
    p, w = xs32.shape
    d = 2 * w
    f = w_down.shape[1]
    nf = f // tf

    def frozen(i, j, na):
        return jnp.where(i < na[0], j, nf - 1)

    grid_spec = pltpu.PrefetchScalarGridSpec(
        num_scalar_prefetch=3, grid=(p // tm, nf),
        in_specs=[pl.BlockSpec((tm, w), lambda i, j, te, nv, na: (jnp.minimum(i, na[0] - 1), 0)),
                  pl.BlockSpec((1, d, tf), lambda i, j, te, nv, na: (te[i], 0, frozen(i, j, na))),
                  pl.BlockSpec((1, d, tf), lambda i, j, te, nv, na: (te[i], 0, nf + frozen(i, j, na))),
                  pl.BlockSpec((1, tf, d), lambda i, j, te, nv, na: (te[i], frozen(i, j, na), 0))],
        out_specs=pl.BlockSpec((tm, d), lambda i, j, te, nv, na: (i, 0)),
        scratch_shapes=[pltpu.VMEM((tm, d), BF16)])
    return pl.pallas_call(
        _moe_ffn_body, grid_spec=grid_spec, out_shape=jax.ShapeDtypeStruct((p, d), F32),
        compiler_params=_cparams("parallel", "arbitrary"), name="moe_ffn")(
            tile_expert, tile_valid, n_active, xs32, w_gu, w_gu, w_down)


def _combine_body(p1_ref, p2_ref, h_ref, info_ref, y_ref, o_ref, buf1, buf2, sem):
    tr = h_ref.shape[0]
    base = pl.program_id(0) * tr

    def row_copy(src, buf, r):
        return pltpu.make_async_copy(y_ref.at[pl.ds(src, 1)], buf.at[pl.ds(r, 1)], sem)

    def issue(r, carry):
        row_copy(p1_ref[base + r], buf1, r).start()
        row_copy(p2_ref[base + r], buf2, r).start()
        return carry

    def drain(r, carry):
        row_copy(p1_ref[base + r], buf1, r).wait()
        row_copy(p2_ref[base + r], buf2, r).wait()
        return carry

    lax.fori_loop(0, tr, issue, 0)
    lax.fori_loop(0, tr, drain, 0)
    info = info_ref[...]
    g1 = info[:, 4:5]
    g2 = info[:, 5:6]
    o_ref[...] = h_ref[...] + (g1 * buf1[...] + g2 * buf2[...])


def moe_combine(h, info, y, pos1, pos2, *, tr):
    n, d = h.shape
    grid_spec = pltpu.PrefetchScalarGridSpec(
        num_scalar_prefetch=2, grid=(n // tr,),
        in_specs=[pl.BlockSpec((tr, d), lambda i, p1, p2: (i, 0)), pl.BlockSpec((tr, LANES), lambda i, p1, p2: (i, 0)),
                  pl.BlockSpec(memory_space=pl.ANY)],
        out_specs=pl.BlockSpec((tr, d), lambda i, p1, p2: (i, 0)),
        scratch_shapes=[pltpu.VMEM((tr, d), F32), pltpu.VMEM((tr, d), F32), pltpu.SemaphoreType.DMA])
    return pl.pallas_call(
        _combine_body, grid_spec=grid_spec, out_shape=jax.ShapeDtypeStruct((n, d), F32),
        compiler_params=_cparams("arbitrary"), name="moe_combine")(pos1, pos2, h, info, y)


def moe_residual(h, nw, router, w_gu, w_down, *, tm_route, tr, tm, tf):
    n, d = h.shape
    router_pad = jnp.zeros((d, LANES), F32).at[:, :N_EXPERTS].set(router)
    x32, info, counts = moe_router(h, nw, router_pad, tm=tm_route)
    cnt = counts[0, :N_EXPERTS].astype(jnp.int32)
    padded = (cnt + tm - 1) // tm * tm
    ends = jnp.cumsum(padded)
    offs = ends - padded
    e1 = info[:, 0].astype(jnp.int32)
    e2 = info[:, 1].astype(jnp.int32)
    pos1 = offs[e1] + info[:, 2].astype(jnp.int32)
    pos2 = offs[e2] + info[:, 3].astype(jnp.int32)
    p_rows = 2 * n + N_EXPERTS * tm
    n_tiles = p_rows // tm
    tile_start = jnp.arange(n_tiles, dtype=jnp.int32) * tm
    tile_expert = jnp.minimum(jnp.sum(tile_start[:, None] >= ends[None, :], axis=1), N_EXPERTS - 1).astype(jnp.int32)
    tile_valid = jnp.clip((offs + cnt)[tile_expert] - tile_start, 0, tm).astype(jnp.int32)
    n_active = (ends[-1:] // tm).astype(jnp.int32)
    zr = min(tm, 256)
    xs32 = moe_dispatch(x32, pos1, pos2, offs + cnt, padded - cnt, ends[-1:], (p_rows - ends[-1:]) // zr,
                        p_rows=p_rows, tr=tr, zr=zr)
    y = moe_ffn(xs32, tile_expert, tile_valid, n_active, w_gu, w_down, tm=tm, tf=tf)
    return moe_combine(h, info, y, pos1, pos2, tr=tr)


def _ple_body(h_ref, hres_ref, nw_ref, wg_ref, bg_ref, p_ref, wp_ref, o_ref, hn_ref):
    @pl.when(pl.program_id(1) == 0)
    def _():
        hn_ref[...] = _rms(h_ref[...], nw_ref[...]).astype(BF16)

    z = jnp.dot(hn_ref[...], wg_ref[...], preferred_element_type=F32) + bg_ref[...]
    proj = jnp.dot(p_ref[...].astype(BF16), wp_ref[...], preferred_element_type=F32)
    o_ref[...] = hres_ref[...] + jax.nn.sigmoid(z) * proj


def ple_residual(h, nw, w_gate, b_gate, p_i, w_proj, *, tm, tn):
    n, d = h.shape
    pd = p_i.shape[1]
    return pl.pallas_call(
        _ple_body, grid=(n // tm, d // tn),
        in_specs=[pl.BlockSpec((tm, d), lambda i, j: (i, 0)), pl.BlockSpec((tm, tn), lambda i, j: (i, j)),
                  pl.BlockSpec((1, d), lambda i, j: (0, 0)), pl.BlockSpec((d, tn), lambda i, j: (0, j)),
                  pl.BlockSpec((1, tn), lambda i, j: (0, j)), pl.BlockSpec((tm, pd), lambda i, j: (i, 0)),
                  pl.BlockSpec((pd, tn), lambda i, j: (0, j))],
        out_specs=pl.BlockSpec((tm, tn), lambda i, j: (i, j)),
        out_shape=jax.ShapeDtypeStruct((n, d), F32),
        scratch_shapes=[pltpu.VMEM((tm, d), BF16)],
        compiler_params=_cparams("parallel", "arbitrary"), name="ple_residual")(h, h, nw, w_gate, b_gate, p_i, w_proj)


def _final_norm_body(h_ref, nw_ref, o_ref):
    o_ref[...] = _rms(h_ref[...], nw_ref[...])


def final_rmsnorm(h, nw, *, tm):
    n, d = h.shape
    return pl.pallas_call(
        _final_norm_body, grid=(n // tm,),
        in_specs=[pl.BlockSpec((tm, d), lambda i: (i, 0)), pl.BlockSpec((1, d), lambda i: (0, 0))],
        out_specs=pl.BlockSpec((tm, d), lambda i: (i, 0)),
        out_shape=jax.ShapeDtypeStruct((n, d), F32),
        compiler_params=_cparams("parallel"), name="final_rmsnorm")(h, nw)


def _row(v):
    return v.reshape(1, -1).astype(F32)


def _tile(n, want):
    return min(n, want)


def _gla_layer(h, norm_mix, w_in, w_a2, b_a2, gla_norm, w_o, *, batch, seq):
    n = h.shape[0]
    main = w_in.shape[1] - GLA_GATE_RANK
    w_main = w_in[:, :main].astype(BF16)
    w_a = jnp.zeros((w_in.shape[0], LANES), BF16).at[:, :GLA_GATE_RANK].set(w_in[:, main:].astype(BF16))
    w_a2_pad = jnp.zeros((LANES, w_a2.shape[1]), BF16).at[:GLA_GATE_RANK].set(w_a2.astype(BF16))
    qkvg, a = norm_matmul(h, _row(norm_mix), w_main, tm=_tile(n, 1024), tn=1024, w_extra=w_a)
    o = gla_core(qkvg, a, w_a2_pad, _row(b_a2), _row(gla_norm), batch=batch, seq=seq, tb=_tile(seq, 512))
    return matmul_residual(o, w_o.astype(BF16), h, tm=_tile(n, 1024), tn=1024)


def _diff_layer(h, positions, norm_mix, w_in, lq1, lk1, lq2, lk2, diff_norm, w_o, *, batch, seq, lambda_init):
    n = h.shape[0]
    tabs = rope_tables(positions, tm=_tile(n, 1024))
    qk_width = 2 * DIFF_HEADS * DIFF_HEAD_DIM
    qkv = norm_matmul_rope(h, _row(norm_mix), w_in.astype(BF16), tabs, tm=_tile(n, 1024), tn=1024,
                           q_width=qk_width, k_width=qk_width, q_scale=DIFF_HEAD_DIM ** -0.5 * math.log2(math.e))
    o = diff_attention_core(qkv, _row(lq1), _row(lk1), _row(lq2), _row(lk2), diff_norm.reshape(-1, 1).astype(F32),
                            batch=batch, seq=seq, tq=_tile(seq, 512), lambda_init=lambda_init)
    return matmul_residual(o, w_o.astype(BF16), h, tm=_tile(n, 1024), tn=1024)


def _diff_lambda_init(layer_idx):
    return 0.8 - 0.6 * math.exp(-0.3 * layer_idx)


def kernel(x, p, positions, l0_norm_mix, l0_gla_w_in, l0_gla_w_a2, l0_gla_b_a2, l0_gla_norm, l0_gla_w_o, l0_norm_ffn, l0_ffn_w_gu, l0_ffn_w_down, l0_ple_norm, l0_ple_w_gate, l0_ple_b_gate, l0_ple_w_proj, l1_norm_mix, l1_pool_w, l1_pool_scale, l1_norm_ffn, l1_moe_router, l1_moe_w_gu, l1_moe_w_down, l1_ple_norm, l1_ple_w_gate, l1_ple_b_gate, l1_ple_w_proj, l2_norm_mix, l2_diff_w_in, l2_diff_lq1, l2_diff_lk1, l2_diff_lq2, l2_diff_lk2, l2_diff_norm, l2_diff_w_o, l2_norm_ffn, l2_ffn_w_gu, l2_ffn_w_down, l2_ple_norm, l2_ple_w_gate, l2_ple_b_gate, l2_ple_w_proj, l3_norm_mix, l3_gla_w_in, l3_gla_w_a2, l3_gla_b_a2, l3_gla_norm, l3_gla_w_o, l3_norm_ffn, l3_moe_router, l3_moe_w_gu, l3_moe_w_down, l3_ple_norm, l3_ple_w_gate, l3_ple_b_gate, l3_ple_w_proj, final_norm):
    batch, seq, d = x.shape
    n = batch * seq
    h = x.reshape(n, d)
    pf = p.reshape(p.shape[0], n, p.shape[-1])
    tm = _tile(n, 512)

    def ffn(h, nw, w_gu, w_down):
        return ffn_residual(h, _row(nw), w_gu.astype(BF16), w_down.astype(BF16), tm=tm, tf=512)

    def moe(h, nw, router, w_gu, w_down):
        return moe_residual(h, _row(nw), router, w_gu, w_down, tm_route=tm, tr=_tile(n, 256), tm=_tile(n, 1024), tf=256)

    def ple(h, i, nw, w_gate, b_gate, w_proj):
        return ple_residual(h, _row(nw), w_gate.astype(BF16), _row(b_gate), pf[i], w_proj.astype(BF16),
                            tm=_tile(n, 1024), tn=1024)

    h = _gla_layer(h, l0_norm_mix, l0_gla_w_in, l0_gla_w_a2, l0_gla_b_a2, l0_gla_norm, l0_gla_w_o, batch=batch, seq=seq)
    h = ffn(h, l0_norm_ffn, l0_ffn_w_gu, l0_ffn_w_down)
    h = ple(h, 0, l0_ple_norm, l0_ple_w_gate, l0_ple_b_gate, l0_ple_w_proj)

    h = pool_mixer(h, _row(l1_norm_mix), l1_pool_w.astype(BF16), _row(l1_pool_scale), seq=seq, tt=_tile(seq, 512))
    h = moe(h, l1_norm_ffn, l1_moe_router, l1_moe_w_gu, l1_moe_w_down)
    h = ple(h, 1, l1_ple_norm, l1_ple_w_gate, l1_ple_b_gate, l1_ple_w_proj)

    h = _diff_layer(h, positions, l2_norm_mix, l2_diff_w_in, l2_diff_lq1, l2_diff_lk1, l2_diff_lq2, l2_diff_lk2,
                    l2_diff_norm, l2_diff_w_o, batch=batch, seq=seq, lambda_init=_diff_lambda_init(2))
    h = ffn(h, l2_norm_ffn, l2_ffn_w_gu, l2_ffn_w_down)
    h = ple(h, 2, l2_ple_norm, l2_ple_w_gate, l2_ple_b_gate, l2_ple_w_proj)

    h = _gla_layer(h, l3_norm_mix, l3_gla_w_in, l3_gla_w_a2, l3_gla_b_a2, l3_gla_norm, l3_gla_w_o, batch=batch, seq=seq)
    h = moe(h, l3_norm_ffn, l3_moe_router, l3_moe_w_gu, l3_moe_w_down)
    h = ple(h, 3, l3_ple_norm, l3_ple_w_gate, l3_ple_b_gate, l3_ple_w_proj)

    return final_rmsnorm(h, _row(final_norm), tm=tm).reshape(batch, seq, d)
```

```python
import functools
import math

import jax
import jax.numpy as jnp
from jax import lax
from jax.experimental import pallas as pl
from jax.experimental.pallas import tpu as pltpu

F32 = jnp.float32
BF16 = jnp.bfloat16

NORM_EPS = 1e-6
LANES = 128
V7X_VMEM_BYTES = 64 * 1024 * 1024
VMEM_LIMIT = V7X_VMEM_BYTES * 7 // 8

GLA_HEADS = 4
GLA_GATE_RANK = 16
GLA_GATE_NORMALIZER = 16.0
GLA_CHUNK = 64
GLA_BLOCK = 256
POOL_WINDOWS = (2, 4, 8, 16)
POOL_HALO = 16
DIFF_HEADS = 8
DIFF_HEAD_DIM = 128
ROPE_THETA = 500000.0
ROPE_DIMS = 32
N_EXPERTS = 8
DMA_LOOP_UNROLL = 8

NT_DIMS = (((1,), (1,)), ((), ()))
TN_DIMS = (((0,), (0,)), ((), ()))


def _cparams(*sem):
    return pltpu.CompilerParams(dimension_semantics=sem, vmem_limit_bytes=VMEM_LIMIT)


def _rms(x, w):
    ms = jnp.mean(x * x, axis=-1, keepdims=True)
    return x * lax.rsqrt(ms + NORM_EPS) * w


def _split3(x):
    hi = x.astype(BF16)
    r1 = x - hi.astype(F32)
    mid = r1.astype(BF16)
    lo = (r1 - mid.astype(F32)).astype(BF16)
    return hi, mid, lo


def _resident(shape):
    return pl.BlockSpec(shape, lambda i: (0,) * len(shape), pipeline_mode=pl.Buffered(1))


def _norm_mm_extra_body(h_ref, nw_ref, w_ref, wx_ref, o_ref, ox_ref):
    xn = _rms(h_ref[...], nw_ref[...]).astype(BF16)
    ox_ref[...] = jnp.dot(xn, wx_ref[...], preferred_element_type=F32)
    o_ref[...] = jnp.dot(xn, w_ref[...], preferred_element_type=F32).astype(o_ref.dtype)


def norm_matmul_extra(h, nw, w, w_extra, *, tm):
    n, d = h.shape
    nout = w.shape[1]
    nx = w_extra.shape[1]
    return pl.pallas_call(
        _norm_mm_extra_body, grid=(n // tm,),
        in_specs=[pl.BlockSpec((tm, d), lambda i: (i, 0)), _resident((1, d)), _resident((d, nout)), _resident((d, nx))],
        out_specs=[pl.BlockSpec((tm, nout), lambda i: (i, 0)), pl.BlockSpec((tm, nx), lambda i: (i, 0))],
        out_shape=[jax.ShapeDtypeStruct((n, nout), BF16), jax.ShapeDtypeStruct((n, nx), F32)],
        compiler_params=_cparams("parallel"), name="norm_matmul_extra")(h, nw, w, w_extra)


def _rope_tab_body(pos_ref, invf_ref, cos_ref, sina_ref, sinb_ref):
    ang = pos_ref[...].astype(F32) * invf_ref[...]
    lane = lax.broadcasted_iota(jnp.int32, ang.shape, 1)
    half = ROPE_DIMS // 2
    s = jnp.sin(ang)
    cos_ref[...] = jnp.cos(ang)
    sina_ref[...] = jnp.where((lane >= half) & (lane < ROPE_DIMS), s, 0.0)
    sinb_ref[...] = jnp.where(lane < half, -s, 0.0)


def rope_tables(positions, *, tm):
    n = positions.size
    half = ROPE_DIMS // 2
    inv_freq = ROPE_THETA ** (-jnp.arange(half, dtype=F32) * 2.0 / ROPE_DIMS)
    invf = jnp.zeros((1, LANES), F32).at[0, :half].set(inv_freq).at[0, half:ROPE_DIMS].set(inv_freq)
    pos = positions.reshape(n, 1)
    tab = jax.ShapeDtypeStruct((n, LANES), F32)
    t_spec = pl.BlockSpec((tm, LANES), lambda i: (i, 0))
    return pl.pallas_call(
        _rope_tab_body, grid=(n // tm,),
        in_specs=[pl.BlockSpec((tm, 1), lambda i: (i, 0)), pl.BlockSpec((1, LANES), lambda i: (0, 0))],
        out_specs=[t_spec, t_spec, t_spec], out_shape=[tab, tab, tab],
        compiler_params=_cparams("parallel"), name="rope_tables")(pos, invf)


def _norm_mm_rope_body(h_ref, nw_ref, w_ref, cos_ref, sina_ref, sinb_ref, o_ref, *, q_heads, k_heads, q_scale):
    xn = _rms(h_ref[...], nw_ref[...]).astype(BF16)
    acc = jnp.dot(xn, w_ref[...], preferred_element_type=F32)
    half = ROPE_DIMS // 2
    c, sa, sb = cos_ref[...], sina_ref[...], sinb_ref[...]
    cq, saq, sbq = c * q_scale, sa * q_scale, sb * q_scale
    for t in range(q_heads + k_heads):
        xj = acc[:, t * LANES:(t + 1) * LANES]
        tc, tsa, tsb = (cq, saq, sbq) if t < q_heads else (c, sa, sb)
        r = xj * tc + pltpu.roll(xj, half, 1) * tsa + pltpu.roll(xj, LANES - half, 1) * tsb
        o_ref[:, t * LANES:(t + 1) * LANES] = r.astype(o_ref.dtype)
    rest = (q_heads + k_heads) * LANES
    o_ref[:, rest:] = acc[:, rest:].astype(o_ref.dtype)


def norm_matmul_rope(h, nw, w, tabs, *, tm, q_width, k_width, q_scale):
    n, d = h.shape
    nout = w.shape[1]
    body = functools.partial(_norm_mm_rope_body, q_heads=q_width // LANES, k_heads=k_width // LANES, q_scale=q_scale)
    t_spec = pl.BlockSpec((tm, LANES), lambda i: (i, 0))
    return pl.pallas_call(
        body, grid=(n // tm,),
        in_specs=[pl.BlockSpec((tm, d), lambda i: (i, 0)), _resident((1, d)), _resident((d, nout)),
                  t_spec, t_spec, t_spec],
        out_specs=pl.BlockSpec((tm, nout), lambda i: (i, 0)),
        out_shape=jax.ShapeDtypeStruct((n, nout), BF16),
        compiler_params=_cparams("parallel"), name="norm_matmul_rope")(h, nw, w, *tabs)


def _mm_res_body(a_ref, w_ref, h_ref, o_ref):
    o_ref[...] = h_ref[...] + jnp.dot(a_ref[...], w_ref[...], preferred_element_type=F32)


def matmul_residual(a, w, h, *, tm):
    n, k = a.shape
    d = w.shape[1]
    return pl.pallas_call(
        _mm_res_body, grid=(n // tm,),
        in_specs=[pl.BlockSpec((tm, k), lambda i: (i, 0)), _resident((k, d)), pl.BlockSpec((tm, d), lambda i: (i, 0))],
        out_specs=pl.BlockSpec((tm, d), lambda i: (i, 0)),
        out_shape=jax.ShapeDtypeStruct((n, d), F32),
        compiler_params=_cparams("parallel"), name="matmul_residual")(a, w, h)


def _gla_body(q_ref, k_ref, v_ref, g_ref, a_ref, wa_ref, ba_ref, nw_ref, o_ref, st_ref, gk_ref, *, block, sub, q_scale):
    @pl.when(pl.program_id(2) == 0)
    def _():
        st_ref[...] = jnp.zeros_like(st_ref)

    z = jnp.dot(a_ref[...].astype(BF16), wa_ref[...], preferred_element_type=F32) + ba_ref[...]
    gk_ref[...] = (jnp.minimum(z, 0.0) - jnp.log1p(jnp.exp(-jnp.abs(z)))) * (1.0 / GLA_GATE_NORMALIZER)

    row = lax.broadcasted_iota(jnp.int32, (block, block), 0)
    col = lax.broadcasted_iota(jnp.int32, (block, block), 1)
    tri = (col <= row).astype(BF16)
    nw = nw_ref[...]

    def step(c, carry):
        r = pl.ds(pl.multiple_of(c * block, block), block)
        g_hi, g_mid, g_lo = _split3(gk_ref[r, :])
        b = (jnp.dot(tri, g_hi, preferred_element_type=F32) + jnp.dot(tri, g_mid, preferred_element_type=F32)
             + jnp.dot(tri, g_lo, preferred_element_type=F32))
        b_last = b[block - 1:block, :]
        q = q_ref[r, :].astype(F32) * q_scale
        k = k_ref[r, :].astype(F32)
        v = v_ref[r, :]
        st = st_ref[...]
        q_in = (q * jnp.exp(b)).astype(BF16)
        k_out = (k * jnp.exp(b_last - b)).astype(BF16)
        o_inter = lax.dot_general(q_in, st.astype(BF16), NT_DIMS, preferred_element_type=F32)
        st_ref[...] = st * jnp.exp(b_last) + lax.dot_general(v, k_out, TN_DIMS, preferred_element_type=F32)
        o_intra = []
        for i in range(block // sub):
            lo, hi = i * sub, (i + 1) * sub
            base = b[lo - 1:lo, :] if i else jnp.zeros_like(b_last)
            q_rel = (q[lo:hi] * jnp.exp(b[lo:hi] - base)).astype(BF16)
            k_rel = (k[:hi] * jnp.exp(base - b[:hi])).astype(BF16)
            s = lax.dot_general(q_rel, k_rel, NT_DIMS, preferred_element_type=F32)
            key = lax.broadcasted_iota(jnp.int32, (sub, hi), 1)
            qry = lax.broadcasted_iota(jnp.int32, (sub, hi), 0) + lo
            s = jnp.where(key <= qry, s, 0.0).astype(BF16)
            o_intra.append(jnp.dot(s, v[:hi], preferred_element_type=F32))
        o = o_inter + jnp.concatenate(o_intra, axis=0)
        gate = g_ref[r, :].astype(F32)
        o_ref[r, :] = (_rms(o, nw) * (gate * jax.nn.sigmoid(gate))).astype(o_ref.dtype)
        return carry

    lax.fori_loop(0, q_ref.shape[0] // block, step, 0)


def gla_core(qkvg, a, w_a2, b_a2, norm_w, *, batch, seq, tb):
    n = qkvg.shape[0]
    kd_total = w_a2.shape[1]
    dk = kd_total // GLA_HEADS
    vd_total = (qkvg.shape[1] - 2 * kd_total) // 2
    dv = vd_total // GLA_HEADS
    nt = seq // tb
    kq = kd_total // dk
    kv = 2 * kd_total // dv
    kg = kv + vd_total // dv
    body = functools.partial(_gla_body, block=min(tb, GLA_BLOCK), sub=GLA_CHUNK, q_scale=dk ** -0.5)
    rows = lambda b, h, t: b * nt + t
    return pl.pallas_call(
        body, grid=(batch, GLA_HEADS, nt),
        in_specs=[pl.BlockSpec((tb, dk), lambda b, h, t: (rows(b, h, t), h)),
                  pl.BlockSpec((tb, dk), lambda b, h, t: (rows(b, h, t), kq + h)),
                  pl.BlockSpec((tb, dv), lambda b, h, t: (rows(b, h, t), kv + h)),
                  pl.BlockSpec((tb, dv), lambda b, h, t: (rows(b, h, t), kg + h)),
                  pl.BlockSpec((tb, LANES), lambda b, h, t: (rows(b, h, t), 0)),
                  pl.BlockSpec((LANES, dk), lambda b, h, t: (0, h)),
                  pl.BlockSpec((1, dk), lambda b, h, t: (0, h)),
                  pl.BlockSpec((1, dv), lambda b, h, t: (0, 0))],
        out_specs=pl.BlockSpec((tb, dv), lambda b, h, t: (rows(b, h, t), h)),
        out_shape=jax.ShapeDtypeStruct((n, vd_total), BF16),
        scratch_shapes=[pltpu.VMEM((dv, dk), F32), pltpu.VMEM((tb, dk), F32)],
        compiler_params=_cparams("parallel", "parallel", "arbitrary"), name="gla_core")(
            qkvg, qkvg, qkvg, qkvg, a, w_a2, b_a2, norm_w)


def _pool_body(h_ref, hp_ref, nw_ref, w_ref, sc_ref, o_ref, x_ref, *, tiles_per_seq, group):
    i = pl.program_id(0)
    tt = h_ref.shape[0]
    h = h_ref[...]
    nw = nw_ref[...]
    first = (i % tiles_per_seq) == 0
    x_ref[0:POOL_HALO, :] = jnp.where(first, 0.0, _rms(hp_ref[...], nw))
    x_ref[POOL_HALO:, :] = _rms(h, nw)
    pos = (i % tiles_per_seq) * tt + lax.broadcasted_iota(jnp.int32, (tt, 1), 0)
    for gi, win in enumerate(POOL_WINDOWS):
        cols = slice(gi * group, (gi + 1) * group)
        acc = x_ref[POOL_HALO:POOL_HALO + tt, cols]
        cur = acc
        for back in range(1, win):
            acc = acc + x_ref[POOL_HALO - back:POOL_HALO - back + tt, cols]
        count = jnp.minimum(pos + 1, win).astype(F32)
        pooled = (acc / count - cur).astype(BF16)
        y = jnp.dot(pooled, w_ref[gi], preferred_element_type=F32)
        o_ref[:, cols] = h[:, cols] + y * sc_ref[:, cols]


def pool_mixer(h, nw, w_groups, scale, *, seq, tt):
    n, d = h.shape
    ng, group, _ = w_groups.shape
    tiles_per_seq = seq // tt
    halo_blocks = tt // POOL_HALO
    body = functools.partial(_pool_body, tiles_per_seq=tiles_per_seq, group=group)
    return pl.pallas_call(
        body, grid=(n // tt,),
        in_specs=[pl.BlockSpec((tt, d), lambda i: (i, 0)),
                  pl.BlockSpec((POOL_HALO, d), lambda i: (jnp.maximum(i * halo_blocks - 1, 0), 0)),
                  pl.BlockSpec((1, d), lambda i: (0, 0)),
                  pl.BlockSpec((ng, group, group), lambda i: (0, 0, 0)),
                  pl.BlockSpec((1, d), lambda i: (0, 0))],
        out_specs=pl.BlockSpec((tt, d), lambda i: (i, 0)),
        out_shape=jax.ShapeDtypeStruct((n, d), F32),
        scratch_shapes=[pltpu.VMEM((tt + POOL_HALO, d), F32)],
        compiler_params=_cparams("parallel"), name="pool_mixer")(h, h, nw, w_groups, scale)


def _diff_body(q_ref, k_ref, v_ref, lq1_ref, lk1_ref, lq2_ref, lk2_ref, nw_ref, o_ref,
               qt_ref, vt_ref, sa_ref, sb_ref, m_ref, l_ref, acc_ref, *, lambda_init):
    qi = pl.program_id(2)
    tq = q_ref.shape[0]
    dh = DIFF_HEAD_DIM
    nk = vt_ref.shape[0]

    @pl.when(qi == 0)
    def _():
        def transpose_v(j, carry):
            r = pl.ds(pl.multiple_of(j * tq, tq), tq)
            vt_ref[j] = v_ref[r, :].astype(F32).T.astype(BF16)
            return carry

        lax.fori_loop(0, nk, transpose_v, 0)

    qt_ref[...] = q_ref[...].astype(F32).T.astype(BF16)
    m_ref[...] = jnp.full_like(m_ref, -jnp.inf)
    l_ref[...] = jnp.zeros_like(l_ref)
    acc_ref[...] = jnp.zeros_like(acc_ref)

    def scores(j, s_ref):
        k = k_ref[pl.ds(pl.multiple_of(j * tq, tq), tq), :]
        for i in range(2):
            s_ref[i] = jnp.dot(k[:, i * dh:(i + 1) * dh], qt_ref[i * dh:(i + 1) * dh, :],
                               preferred_element_type=F32)

    def consume(j, s_ref, masked):
        vt = vt_ref[j]
        if masked:
            key = lax.broadcasted_iota(jnp.int32, (tq, tq), 0)
            qry = lax.broadcasted_iota(jnp.int32, (tq, tq), 1)
            keep = key <= qry
        for i in range(2):
            s = s_ref[i]
            if masked:
                s = jnp.where(keep, s, -jnp.inf)
            m_old = m_ref[i]
            m_new = jnp.maximum(m_old, jnp.max(s, axis=0, keepdims=True))
            alpha = jnp.exp2(m_old - m_new)
            p = jnp.exp2(s - m_new)
            l_ref[i] = alpha * l_ref[i] + jnp.sum(p, axis=0, keepdims=True)
            acc_ref[i] = alpha * acc_ref[i] + jnp.dot(vt, p.astype(BF16), preferred_element_type=F32)
            m_ref[i] = m_new

    def pipelined_pair(t, carry):
        j = 2 * t
        scores(j + 1, sb_ref)
        consume(j, sa_ref, False)
        scores(j + 2, sa_ref)
        consume(j + 1, sb_ref, False)
        return carry

    scores(0, sa_ref)
    lax.fori_loop(0, qi // 2, pipelined_pair, 0)

    @pl.when(qi % 2 == 0)
    def _():
        consume(qi, sa_ref, True)

    @pl.when(qi % 2 == 1)
    def _():
        scores(qi, sb_ref)
        consume(qi - 1, sa_ref, False)
        consume(qi, sb_ref, True)

    lam = (jnp.exp(jnp.sum(lq1_ref[...] * lk1_ref[...], axis=-1, keepdims=True))
           - jnp.exp(jnp.sum(lq2_ref[...] * lk2_ref[...], axis=-1, keepdims=True)) + lambda_init)
    ot = acc_ref[0] * (1.0 / l_ref[0]) - lam * (acc_ref[1] * (1.0 / l_ref[1]))
    ms = jnp.mean(ot * ot, axis=0, keepdims=True)
    ot = ot * lax.rsqrt(ms + NORM_EPS) * (nw_ref[...] * (1.0 - lambda_init))
    o_ref[...] = ot.T.astype(o_ref.dtype)


def diff_attention_core(qkv, lq1, lk1, lq2, lk2, norm_col, *, batch, seq, tq, lambda_init):
    n = qkv.shape[0]
    dh = DIFF_HEAD_DIM
    nq = seq // tq
    k_off = DIFF_HEADS
    v_off = 2 * DIFF_HEADS
    vec = pl.BlockSpec((1, dh), lambda b, h, q: (0, 0))
    return pl.pallas_call(
        functools.partial(_diff_body, lambda_init=lambda_init), grid=(batch, DIFF_HEADS, nq),
        in_specs=[pl.BlockSpec((tq, 2 * dh), lambda b, h, q: (b * nq + q, h)),
                  pl.BlockSpec((seq, 2 * dh), lambda b, h, q: (b, k_off + h)),
                  pl.BlockSpec((seq, 2 * dh), lambda b, h, q: (b, v_off + h)),
                  vec, vec, vec, vec,
                  pl.BlockSpec((2 * dh, 1), lambda b, h, q: (0, 0))],
        out_specs=pl.BlockSpec((tq, 2 * dh), lambda b, h, q: (b * nq + q, h)),
        out_shape=jax.ShapeDtypeStruct((n, DIFF_HEADS * 2 * dh), BF16),
        scratch_shapes=[pltpu.VMEM((2 * dh, tq), BF16), pltpu.VMEM((nq, 2 * dh, tq), BF16),
                        pltpu.VMEM((2, tq, tq), F32), pltpu.VMEM((2, tq, tq), F32), pltpu.VMEM((2, 1, tq), F32), pltpu.VMEM((2, 1, tq), F32), pltpu.VMEM((2, 2 * dh, tq), F32)],
        compiler_params=_cparams("parallel", "parallel", "arbitrary"), name="diff_attention")(
            qkv, qkv, qkv, lq1, lk1, lq2, lk2, norm_col)


def _ffn_body(h_ref, nw_ref, wg_ref, wu_ref, wd_ref, o_ref, xn_ref):
    @pl.when(pl.program_id(1) == 0)
    def _():
        h = h_ref[...]
        xn_ref[...] = _rms(h, nw_ref[...]).astype(BF16)
        o_ref[...] = h

    xn = xn_ref[...]
    gate = jnp.dot(xn, wg_ref[...], preferred_element_type=F32)
    up = jnp.dot(xn, wu_ref[...], preferred_element_type=F32)
    act = (gate * jax.nn.sigmoid(gate) * up).astype(BF16)
    o_ref[...] += jnp.dot(act, wd_ref[...], preferred_element_type=F32)


def ffn_residual(h, nw, w_gu, w_down, *, tm, tf):
    n, d = h.shape
    f = w_down.shape[0]
    nf = f // tf
    return pl.pallas_call(
        _ffn_body, grid=(n // tm, nf),
        in_specs=[pl.BlockSpec((tm, d), lambda i, j: (i, 0)), pl.BlockSpec((1, d), lambda i, j: (0, 0)),
                  pl.BlockSpec((d, tf), lambda i, j: (0, j)), pl.BlockSpec((d, tf), lambda i, j: (0, nf + j)),
                  pl.BlockSpec((tf, d), lambda i, j: (j, 0))],
        out_specs=pl.BlockSpec((tm, d), lambda i, j: (i, 0)),
        out_shape=jax.ShapeDtypeStruct((n, d), F32),
        scratch_shapes=[pltpu.VMEM((tm, d), BF16)],
        compiler_params=_cparams("parallel", "arbitrary"), name="ffn_residual")(h, nw, w_gu, w_gu, w_down)


def _pack_bf16_pairs(x):
    w = x.shape[1] // 2
    lo = pltpu.bitcast(x[:, :w].astype(BF16).astype(F32), jnp.uint32) >> 16
    hi = pltpu.bitcast(x[:, w:].astype(BF16).astype(F32), jnp.uint32)
    return hi | lo


def _unpack_bf16_pairs(u):
    lo = pltpu.bitcast(u << 16, F32).astype(BF16)
    hi = pltpu.bitcast(u & jnp.uint32(0xFFFF0000), F32).astype(BF16)
    return jnp.concatenate([lo, hi], axis=1)


def _router_body(h_ref, nw_ref, r_ref, xn_ref, info_ref, cnt_ref, carry_ref):
    i = pl.program_id(0)

    @pl.when(i == 0)
    def _():
        carry_ref[...] = jnp.zeros_like(carry_ref)

    xn = _rms(h_ref[...], nw_ref[...])
    xn_ref[...] = _pack_bf16_pairs(xn)
    x_hi, x_mid, x_lo = _split3(xn)
    r_hi, r_mid, r_lo = _split3(r_ref[...])
    dot = functools.partial(jnp.dot, preferred_element_type=F32)
    logits = (dot(x_hi, r_hi) + (dot(x_hi, r_mid) + dot(x_mid, r_hi))
              + (dot(x_hi, r_lo) + dot(x_mid, r_mid) + dot(x_lo, r_hi)))
    tm = logits.shape[0]
    lane = lax.broadcasted_iota(jnp.int32, logits.shape, 1)
    lane_f = lane.astype(F32)
    lg = jnp.where(lane < N_EXPERTS, logits, -jnp.inf)
    m1 = jnp.max(lg, axis=-1, keepdims=True)
    e1 = jnp.min(jnp.where(lg == m1, lane_f, float(LANES)), axis=-1, keepdims=True)
    lg2 = jnp.where(lane_f == e1, -jnp.inf, lg)
    m2 = jnp.max(lg2, axis=-1, keepdims=True)
    e2 = jnp.min(jnp.where(lg2 == m2, lane_f, float(LANES)), axis=-1, keepdims=True)
    ex = jnp.exp(m2 - m1)
    g1 = 1.0 / (1.0 + ex)
    g2 = ex / (1.0 + ex)
    hot1 = lane_f == e1
    hot2 = lane_f == e2
    onehot = jnp.where(hot1 | hot2, 1.0, 0.0)
    row = lax.broadcasted_iota(jnp.int32, (tm, tm), 0)
    col = lax.broadcasted_iota(jnp.int32, (tm, tm), 1)
    before = jnp.where(col < row, 1.0, 0.0).astype(BF16)
    cum = dot(before, onehot.astype(BF16)) + carry_ref[...]
    rank1 = jnp.sum(jnp.where(hot1, cum, 0.0), axis=-1, keepdims=True)
    rank2 = jnp.sum(jnp.where(hot2, cum, 0.0), axis=-1, keepdims=True)
    info = jnp.zeros(logits.shape, F32)
    for idx, val in enumerate((e1, e2, rank1, rank2, g1, g2)):
        info = jnp.where(lane == idx, val, info)
    info_ref[...] = info
    carry_ref[...] += jnp.sum(onehot, axis=0, keepdims=True)
    cnt_ref[...] = jnp.broadcast_to(carry_ref[...], cnt_ref.shape)


def moe_router(h, nw, router_pad, *, tm):
    n, d = h.shape
    return pl.pallas_call(
        _router_body, grid=(n // tm,),
        in_specs=[pl.BlockSpec((tm, d), lambda i: (i, 0)), pl.BlockSpec((1, d), lambda i: (0, 0)),
                  pl.BlockSpec((d, LANES), lambda i: (0, 0))],
        out_specs=[pl.BlockSpec((tm, d // 2), lambda i: (i, 0)), pl.BlockSpec((tm, LANES), lambda i: (i, 0)),
                   pl.BlockSpec((8, LANES), lambda i: (0, 0))],
        out_shape=[jax.ShapeDtypeStruct((n, d // 2), jnp.uint32), jax.ShapeDtypeStruct((n, LANES), F32),
                   jax.ShapeDtypeStruct((8, LANES), F32)],
        scratch_shapes=[pltpu.VMEM((1, LANES), F32)],
        compiler_params=_cparams("arbitrary"), name="moe_router")(h, nw, router_pad)


def _dispatch_body(p1_ref, p2_ref, ps_ref, pn_ref, ts_ref, tn_ref, x_ref, xs_ref, zero_ref, sem, zsem):
    tr = x_ref.shape[0]
    zr = zero_ref.shape[0]
    base = pl.program_id(0) * tr

    @pl.when(pl.program_id(0) == 0)
    def _():
        zero_ref[...] = jnp.zeros_like(zero_ref)

        def pad_copy(dst):
            return pltpu.make_async_copy(zero_ref.at[pl.ds(0, 1)], xs_ref.at[pl.ds(dst, 1)], zsem)

        def tail_copy(c):
            return pltpu.make_async_copy(zero_ref, xs_ref.at[pl.ds(pl.multiple_of(ts_ref[0] + c * zr, zr), zr)], zsem)

        for e in range(N_EXPERTS):
            lax.fori_loop(0, pn_ref[e], lambda r, c, e=e: (pad_copy(ps_ref[e] + r).start(), c)[1], 0)
        lax.fori_loop(0, tn_ref[0], lambda c, carry: (tail_copy(c).start(), carry)[1], 0)
        for e in range(N_EXPERTS):
            lax.fori_loop(0, pn_ref[e], lambda r, c, e=e: (pad_copy(ps_ref[e] + r).wait(), c)[1], 0)
        lax.fori_loop(0, tn_ref[0], lambda c, carry: (tail_copy(c).wait(), carry)[1], 0)

    def row_copy(r, dst):
        return pltpu.make_async_copy(x_ref.at[pl.ds(r, 1)], xs_ref.at[pl.ds(dst, 1)], sem)

    def issue(r, carry):
        row_copy(r, p1_ref[base + r]).start()
        row_copy(r, p2_ref[base + r]).start()
        return carry

    def drain(r, carry):
        row_copy(r, p1_ref[base + r]).wait()
        row_copy(r, p2_ref[base + r]).wait()
        return carry

    lax.fori_loop(0, tr, issue, 0, unroll=DMA_LOOP_UNROLL)
    lax.fori_loop(0, tr, drain, 0, unroll=DMA_LOOP_UNROLL)


def moe_dispatch(x32, pos1, pos2, pad_start, pad_len, tail_start, tail_chunks, *, p_rows, tr, zr):
    n, w = x32.shape
    grid_spec = pltpu.PrefetchScalarGridSpec(
        num_scalar_prefetch=6, grid=(n // tr,),
        in_specs=[pl.BlockSpec((tr, w), lambda i, *_: (i, 0))],
        out_specs=pl.BlockSpec(memory_space=pl.ANY),
        scratch_shapes=[pltpu.VMEM((zr, w), x32.dtype), pltpu.SemaphoreType.DMA, pltpu.SemaphoreType.DMA])
    return pl.pallas_call(
        _dispatch_body, grid_spec=grid_spec, out_shape=jax.ShapeDtypeStruct((p_rows, w), x32.dtype),
        compiler_params=_cparams("arbitrary"), name="moe_dispatch")(
            pos1, pos2, pad_start, pad_len, tail_start, tail_chunks, x32)


def _moe_ffn_body(te_ref, nv_ref, na_ref, x_ref, wg_ref, wu_ref, wd_ref, o_ref, xb_ref):
    i = pl.program_id(0)
    j = pl.program_id(1)
    half_rows = x_ref.shape[0] // 2

    @pl.when(j == 0)
    def _():
        o_ref[...] = jnp.zeros_like(o_ref)
        xb_ref[...] = _unpack_bf16_pairs(x_ref[...])

    @pl.when(i < na_ref[0])
    def _():
        wg = wg_ref[0].astype(BF16)
        wu = wu_ref[0].astype(BF16)
        wd = wd_ref[0].astype(BF16)

        def half(r0):
            x = xb_ref[r0:r0 + half_rows, :]
            gate = jnp.dot(x, wg, preferred_element_type=F32)
            up = jnp.dot(x, wu, preferred_element_type=F32)
            act = (gate * jax.nn.sigmoid(gate) * up).astype(BF16)
            o_ref[r0:r0 + half_rows, :] += jnp.dot(act, wd, preferred_element_type=F32)

        half(0)

        @pl.when(nv_ref[i] > half_rows)
        def _():
            half(half_rows)


def moe_ffn(xs32, tile_expert, tile_valid, n_active, w_gu, w_down, *, tm, tf):
    p, w = xs32.shape
    d = 2 * w
    f = w_down.shape[1]
    nf = f // tf

    def frozen(i, j, na):
        return jnp.where(i < na[0], j, nf - 1)

    grid_spec = pltpu.PrefetchScalarGridSpec(
        num_scalar_prefetch=3, grid=(p // tm, nf),
        in_specs=[pl.BlockSpec((tm, w), lambda i, j, te, nv, na: (jnp.minimum(i, na[0] - 1), 0)),
                  pl.BlockSpec((1, d, tf), lambda i, j, te, nv, na: (te[i], 0, frozen(i, j, na))),
                  pl.BlockSpec((1, d, tf), lambda i, j, te, nv, na: (te[i], 0, nf + frozen(i, j, na))),
                  pl.BlockSpec((1, tf, d), lambda i, j, te, nv, na: (te[i], frozen(i, j, na), 0))],
        out_specs=pl.BlockSpec((tm, d), lambda i, j, te, nv, na: (i, 0)),
        scratch_shapes=[pltpu.VMEM((tm, d), BF16)])
    return pl.pallas_call(
        _moe_ffn_body, grid_spec=grid_spec, out_shape=jax.ShapeDtypeStruct((p, d), F32),
        compiler_params=_cparams("parallel", "arbitrary"), name="moe_ffn")(
            tile_expert, tile_valid, n_active, xs32, w_gu, w_gu, w_down)


def _combine_body(p1_ref, p2_ref, h_ref, info_ref, y_ref, o_ref, buf1, buf2, sem):
    tr = h_ref.shape[0]
    base = pl.program_id(0) * tr

    def row_copy(src, buf, r):
        return pltpu.make_async_copy(y_ref.at[pl.ds(src, 1)], buf.at[pl.ds(r, 1)], sem)

    def issue(r, carry):
        row_copy(p1_ref[base + r], buf1, r).start()
        row_copy(p2_ref[base + r], buf2, r).start()
        return carry

    def drain(r, carry):
        row_copy(p1_ref[base + r], buf1, r).wait()
        row_copy(p2_ref[base + r], buf2, r).wait()
        return carry

    lax.fori_loop(0, tr, issue, 0, unroll=DMA_LOOP_UNROLL)
    lax.fori_loop(0, tr, drain, 0, unroll=DMA_LOOP_UNROLL)
    info = info_ref[...]
    g1 = info[:, 4:5]
    g2 = info[:, 5:6]
    o_ref[...] = h_ref[...] + (g1 * buf1[...] + g2 * buf2[...])


def moe_combine(h, info, y, pos1, pos2, *, tr):
    n, d = h.shape
    grid_spec = pltpu.PrefetchScalarGridSpec(
        num_scalar_prefetch=2, grid=(n // tr,),
        in_specs=[pl.BlockSpec((tr, d), lambda i, p1, p2: (i, 0)), pl.BlockSpec((tr, LANES), lambda i, p1, p2: (i, 0)),
                  pl.BlockSpec(memory_space=pl.ANY)],
        out_specs=pl.BlockSpec((tr, d), lambda i, p1, p2: (i, 0)),
        scratch_shapes=[pltpu.VMEM((tr, d), F32), pltpu.VMEM((tr, d), F32), pltpu.SemaphoreType.DMA])
    return pl.pallas_call(
        _combine_body, grid_spec=grid_spec, out_shape=jax.ShapeDtypeStruct((n, d), F32),
        compiler_params=_cparams("arbitrary"), name="moe_combine")(pos1, pos2, h, info, y)


def moe_residual(h, nw, router, w_gu, w_down, *, tm_route, tr, tm, tf):
    n, d = h.shape
    router_pad = jnp.zeros((d, LANES), F32).at[:, :N_EXPERTS].set(router)
    x32, info, counts = moe_router(h, nw, router_pad, tm=tm_route)
    cnt = counts[0, :N_EXPERTS].astype(jnp.int32)
    padded = (cnt + tm - 1) // tm * tm
    ends = jnp.cumsum(padded)
    offs = ends - padded
    e1 = info[:, 0].astype(jnp.int32)
    e2 = info[:, 1].astype(jnp.int32)
    pos1 = offs[e1] + info[:, 2].astype(jnp.int32)
    pos2 = offs[e2] + info[:, 3].astype(jnp.int32)
    p_rows = 2 * n + N_EXPERTS * tm
    n_tiles = p_rows // tm
    tile_start = jnp.arange(n_tiles, dtype=jnp.int32) * tm
    tile_expert = jnp.minimum(jnp.sum(tile_start[:, None] >= ends[None, :], axis=1), N_EXPERTS - 1).astype(jnp.int32)
    tile_valid = jnp.clip((offs + cnt)[tile_expert] - tile_start, 0, tm).astype(jnp.int32)
    n_active = (ends[-1:] // tm).astype(jnp.int32)
    zr = min(tm, 256)
    xs32 = moe_dispatch(x32, pos1, pos2, offs + cnt, padded - cnt, ends[-1:], (p_rows - ends[-1:]) // zr,
                        p_rows=p_rows, tr=tr, zr=zr)
    y = moe_ffn(xs32, tile_expert, tile_valid, n_active, w_gu, w_down, tm=tm, tf=tf)
    return moe_combine(h, info, y, pos1, pos2, tr=tr)


def _ple_body(h_ref, nw_ref, wg_ref, bg_ref, p_ref, wp_ref, o_ref):
    h = h_ref[...]
    hn = _rms(h, nw_ref[...]).astype(BF16)
    z = jnp.dot(hn, wg_ref[...], preferred_element_type=F32) + bg_ref[...]
    proj = jnp.dot(p_ref[...].astype(BF16), wp_ref[...], preferred_element_type=F32)
    o_ref[...] = h + jax.nn.sigmoid(z) * proj


def ple_residual(h, nw, w_gate, b_gate, p_i, w_proj, *, tm):
    n, d = h.shape
    pd = p_i.shape[1]
    return pl.pallas_call(
        _ple_body, grid=(n // tm,),
        in_specs=[pl.BlockSpec((tm, d), lambda i: (i, 0)), _resident((1, d)), _resident((d, d)), _resident((1, d)),
                  pl.BlockSpec((tm, pd), lambda i: (i, 0)), _resident((pd, d))],
        out_specs=pl.BlockSpec((tm, d), lambda i: (i, 0)),
        out_shape=jax.ShapeDtypeStruct((n, d), F32),
        compiler_params=_cparams("parallel"), name="ple_residual")(h, nw, w_gate, b_gate, p_i, w_proj)


def _final_norm_body(h_ref, nw_ref, o_ref):
    o_ref[...] = _rms(h_ref[...], nw_ref[...])


def final_rmsnorm(h, nw, *, tm):
    n, d = h.shape
    return pl.pallas_call(
        _final_norm_body, grid=(n // tm,),
        in_specs=[pl.BlockSpec((tm, d), lambda i: (i, 0)), pl.BlockSpec((1, d), lambda i: (0, 0))],
        out_specs=pl.BlockSpec((tm, d), lambda i: (i, 0)),
        out_shape=jax.ShapeDtypeStruct((n, d), F32),
        compiler_params=_cparams("parallel"), name="final_rmsnorm")(h, nw)


def _row(v):
    return v.reshape(1, -1).astype(F32)


def _tile(n, want):
    return min(n, want)


def _gla_layer(h, norm_mix, w_in, w_a2, b_a2, gla_norm, w_o, *, batch, seq):
    n = h.shape[0]
    main = w_in.shape[1] - GLA_GATE_RANK
    w_main = w_in[:, :main].astype(BF16)
    w_a = jnp.zeros((w_in.shape[0], LANES), BF16).at[:, :GLA_GATE_RANK].set(w_in[:, main:].astype(BF16))
    w_a2_pad = jnp.zeros((LANES, w_a2.shape[1]), BF16).at[:GLA_GATE_RANK].set(w_a2.astype(BF16))
    qkvg, a = norm_matmul_extra(h, _row(norm_mix), w_main, w_a, tm=_tile(n, 256))
    o = gla_core(qkvg, a, w_a2_pad, _row(b_a2), _row(gla_norm), batch=batch, seq=seq, tb=_tile(seq, 1024))
    return matmul_residual(o, w_o.astype(BF16), h, tm=_tile(n, 512))


def _diff_layer(h, positions, norm_mix, w_in, lq1, lk1, lq2, lk2, diff_norm, w_o, *, batch, seq, lambda_init):
    n = h.shape[0]
    tabs = rope_tables(positions, tm=_tile(n, 1024))
    qk_width = 2 * DIFF_HEADS * DIFF_HEAD_DIM
    qkv = norm_matmul_rope(h, _row(norm_mix), w_in.astype(BF16), tabs, tm=_tile(n, 256),
                           q_width=qk_width, k_width=qk_width, q_scale=DIFF_HEAD_DIM ** -0.5 * math.log2(math.e))
    o = diff_attention_core(qkv, _row(lq1), _row(lk1), _row(lq2), _row(lk2), diff_norm.reshape(-1, 1).astype(F32),
                            batch=batch, seq=seq, tq=_tile(seq, 512), lambda_init=lambda_init)
    return matmul_residual(o, w_o.astype(BF16), h, tm=_tile(n, 512))


def _diff_lambda_init(layer_idx):
    return 0.8 - 0.6 * math.exp(-0.3 * layer_idx)


def kernel(x, p, positions, l0_norm_mix, l0_gla_w_in, l0_gla_w_a2, l0_gla_b_a2, l0_gla_norm, l0_gla_w_o, l0_norm_ffn, l0_ffn_w_gu, l0_ffn_w_down, l0_ple_norm, l0_ple_w_gate, l0_ple_b_gate, l0_ple_w_proj, l1_norm_mix, l1_pool_w, l1_pool_scale, l1_norm_ffn, l1_moe_router, l1_moe_w_gu, l1_moe_w_down, l1_ple_norm, l1_ple_w_gate, l1_ple_b_gate, l1_ple_w_proj, l2_norm_mix, l2_diff_w_in, l2_diff_lq1, l2_diff_lk1, l2_diff_lq2, l2_diff_lk2, l2_diff_norm, l2_diff_w_o, l2_norm_ffn, l2_ffn_w_gu, l2_ffn_w_down, l2_ple_norm, l2_ple_w_gate, l2_ple_b_gate, l2_ple_w_proj, l3_norm_mix, l3_gla_w_in, l3_gla_w_a2, l3_gla_b_a2, l3_gla_norm, l3_gla_w_o, l3_norm_ffn, l3_moe_router, l3_moe_w_gu, l3_moe_w_down, l3_ple_norm, l3_ple_w_gate, l3_ple_b_gate, l3_ple_w_proj, final_norm):
    batch, seq, d = x.shape
    n = batch * seq
    h = x.reshape(n, d)
    pf = p.reshape(p.shape[0], n, p.shape[-1])
    tm = _tile(n, 512)

    def ffn(h, nw, w_gu, w_down):
        return ffn_residual(h, _row(nw), w_gu.astype(BF16), w_down.astype(BF16), tm=tm, tf=512)

    def moe(h, nw, router, w_gu, w_down):
        return moe_residual(h, _row(nw), router, w_gu, w_down, tm_route=tm, tr=_tile(n, 256), tm=_tile(n, 1024), tf=256)

    def ple(h, i, nw, w_gate, b_gate, w_proj):
        return ple_residual(h, _row(nw), w_gate.astype(BF16), _row(b_gate), pf[i], w_proj.astype(BF16), tm=tm)

    h = _gla_layer(h, l0_norm_mix, l0_gla_w_in, l0_gla_w_a2, l0_gla_b_a2, l0_gla_norm, l0_gla_w_o, batch=batch, seq=seq)
    h = ffn(h, l0_norm_ffn, l0_ffn_w_gu, l0_ffn_w_down)
    h = ple(h, 0, l0_ple_norm, l0_ple_w_gate, l0_ple_b_gate, l0_ple_w_proj)

    h = pool_mixer(h, _row(l1_norm_mix), l1_pool_w.astype(BF16), _row(l1_pool_scale), seq=seq, tt=_tile(seq, 512))
    h = moe(h, l1_norm_ffn, l1_moe_router, l1_moe_w_gu, l1_moe_w_down)
    h = ple(h, 1, l1_ple_norm, l1_ple_w_gate, l1_ple_b_gate, l1_ple_w_proj)

    h = _diff_layer(h, positions, l2_norm_mix, l2_diff_w_in, l2_diff_lq1, l2_diff_lk1, l2_diff_lq2, l2_diff_lk2,
                    l2_diff_norm, l2_diff_w_o, batch=batch, seq=seq, lambda_init=_diff_lambda_init(2))
    h = ffn(h, l2_norm_ffn, l2_ffn_w_gu, l2_ffn_w_down)
    h = ple(h, 2, l2_ple_norm, l2_ple_w_gate, l2_ple_b_gate, l2_ple_w_proj)

    h = _gla_layer(h, l3_norm_mix, l3_gla_w_in, l3_gla_w_a2, l3_gla_b_a2, l3_gla_norm, l3_gla_w_o, batch=batch, seq=seq)
    h = moe(h, l3_norm_ffn, l3_moe_router, l3_moe_w_gu, l3_moe_w_down)
    h = ple(h, 3, l3_ple_norm, l3_ple_w_gate, l3_ple_b_gate, l3_ple_w_proj)

    return final_rmsnorm(h, _row(final_norm), tm=tm).reshape(batch, seq, d)
```

```python
import functools
import math

import jax
import jax.numpy as jnp
from jax import lax
from jax.experimental import pallas as pl
from jax.experimental.pallas import tpu as pltpu

F32 = jnp.float32
BF16 = jnp.bfloat16

NORM_EPS = 1e-6
LANES = 128
V7X_VMEM_BYTES = 64 * 1024 * 1024
VMEM_LIMIT = V7X_VMEM_BYTES * 7 // 8

GLA_HEADS = 4
GLA_GATE_RANK = 16
GLA_GATE_NORMALIZER = 16.0
GLA_CHUNK = 64
GLA_BLOCK = 256
POOL_WINDOWS = (2, 4, 8, 16)
POOL_HALO = 16
DIFF_HEADS = 8
DIFF_HEAD_DIM = 128
ROPE_THETA = 500000.0
ROPE_DIMS = 32
N_EXPERTS = 8
MOE_TILE_PARTS = 2
DMA_LOOP_UNROLL = 8

NT_DIMS = (((1,), (1,)), ((), ()))
TN_DIMS = (((0,), (0,)), ((), ()))


def _cparams(*sem):
    return pltpu.CompilerParams(dimension_semantics=sem, vmem_limit_bytes=VMEM_LIMIT)


def _rms(x, w):
    ms = jnp.mean(x * x, axis=-1, keepdims=True)
    return x * lax.rsqrt(ms + NORM_EPS) * w


def _split3(x):
    hi = x.astype(BF16)
    r1 = x - hi.astype(F32)
    mid = r1.astype(BF16)
    lo = (r1 - mid.astype(F32)).astype(BF16)
    return hi, mid, lo


def _resident(shape):
    return pl.BlockSpec(shape, lambda i: (0,) * len(shape), pipeline_mode=pl.Buffered(1))


def _norm_mm_extra_body(h_ref, nw_ref, w_ref, wx_ref, o_ref, ox_ref):
    xn = _rms(h_ref[...], nw_ref[...]).astype(BF16)
    ox_ref[...] = jnp.dot(xn, wx_ref[...], preferred_element_type=F32)
    o_ref[...] = jnp.dot(xn, w_ref[...], preferred_element_type=F32).astype(o_ref.dtype)


def norm_matmul_extra(h, nw, w, w_extra, *, tm):
    n, d = h.shape
    nout = w.shape[1]
    nx = w_extra.shape[1]
    return pl.pallas_call(
        _norm_mm_extra_body, grid=(n // tm,),
        in_specs=[pl.BlockSpec((tm, d), lambda i: (i, 0)), _resident((1, d)), _resident((d, nout)), _resident((d, nx))],
        out_specs=[pl.BlockSpec((tm, nout), lambda i: (i, 0)), pl.BlockSpec((tm, nx), lambda i: (i, 0))],
        out_shape=[jax.ShapeDtypeStruct((n, nout), BF16), jax.ShapeDtypeStruct((n, nx), F32)],
        compiler_params=_cparams("parallel"), name="norm_matmul_extra")(h, nw, w, w_extra)


def _rope_tab_body(pos_ref, invf_ref, cos_ref, sina_ref, sinb_ref):
    ang = pos_ref[...].astype(F32) * invf_ref[...]
    lane = lax.broadcasted_iota(jnp.int32, ang.shape, 1)
    half = ROPE_DIMS // 2
    s = jnp.sin(ang)
    cos_ref[...] = jnp.cos(ang)
    sina_ref[...] = jnp.where((lane >= half) & (lane < ROPE_DIMS), s, 0.0)
    sinb_ref[...] = jnp.where(lane < half, -s, 0.0)


def rope_tables(positions, *, tm):
    n = positions.size
    half = ROPE_DIMS // 2
    inv_freq = ROPE_THETA ** (-jnp.arange(half, dtype=F32) * 2.0 / ROPE_DIMS)
    invf = jnp.zeros((1, LANES), F32).at[0, :half].set(inv_freq).at[0, half:ROPE_DIMS].set(inv_freq)
    pos = positions.reshape(n, 1)
    tab = jax.ShapeDtypeStruct((n, LANES), F32)
    t_spec = pl.BlockSpec((tm, LANES), lambda i: (i, 0))
    return pl.pallas_call(
        _rope_tab_body, grid=(n // tm,),
        in_specs=[pl.BlockSpec((tm, 1), lambda i: (i, 0)), pl.BlockSpec((1, LANES), lambda i: (0, 0))],
        out_specs=[t_spec, t_spec, t_spec], out_shape=[tab, tab, tab],
        compiler_params=_cparams("parallel"), name="rope_tables")(pos, invf)


def _norm_mm_rope_body(h_ref, nw_ref, w_ref, cos_ref, sina_ref, sinb_ref, o_ref, *, q_heads, k_heads, q_scale):
    xn = _rms(h_ref[...], nw_ref[...]).astype(BF16)
    acc = jnp.dot(xn, w_ref[...], preferred_element_type=F32)
    half = ROPE_DIMS // 2
    c, sa, sb = cos_ref[...], sina_ref[...], sinb_ref[...]
    cq, saq, sbq = c * q_scale, sa * q_scale, sb * q_scale
    for t in range(q_heads + k_heads):
        xj = acc[:, t * LANES:(t + 1) * LANES]
        tc, tsa, tsb = (cq, saq, sbq) if t < q_heads else (c, sa, sb)
        r = xj * tc + pltpu.roll(xj, half, 1) * tsa + pltpu.roll(xj, LANES - half, 1) * tsb
        o_ref[:, t * LANES:(t + 1) * LANES] = r.astype(o_ref.dtype)
    rest = (q_heads + k_heads) * LANES
    o_ref[:, rest:] = acc[:, rest:].astype(o_ref.dtype)


def norm_matmul_rope(h, nw, w, tabs, *, tm, q_width, k_width, q_scale):
    n, d = h.shape
    nout = w.shape[1]
    body = functools.partial(_norm_mm_rope_body, q_heads=q_width // LANES, k_heads=k_width // LANES, q_scale=q_scale)
    t_spec = pl.BlockSpec((tm, LANES), lambda i: (i, 0))
    return pl.pallas_call(
        body, grid=(n // tm,),
        in_specs=[pl.BlockSpec((tm, d), lambda i: (i, 0)), _resident((1, d)), _resident((d, nout)),
                  t_spec, t_spec, t_spec],
        out_specs=pl.BlockSpec((tm, nout), lambda i: (i, 0)),
        out_shape=jax.ShapeDtypeStruct((n, nout), BF16),
        compiler_params=_cparams("parallel"), name="norm_matmul_rope")(h, nw, w, *tabs)


def _mm_res_body(a_ref, w_ref, h_ref, o_ref):
    o_ref[...] = h_ref[...] + jnp.dot(a_ref[...], w_ref[...], preferred_element_type=F32)


def matmul_residual(a, w, h, *, tm):
    n, k = a.shape
    d = w.shape[1]
    return pl.pallas_call(
        _mm_res_body, grid=(n // tm,),
        in_specs=[pl.BlockSpec((tm, k), lambda i: (i, 0)), _resident((k, d)), pl.BlockSpec((tm, d), lambda i: (i, 0))],
        out_specs=pl.BlockSpec((tm, d), lambda i: (i, 0)),
        out_shape=jax.ShapeDtypeStruct((n, d), F32),
        compiler_params=_cparams("parallel"), name="matmul_residual")(a, w, h)


def _gla_body(q_ref, k_ref, v_ref, g_ref, a_ref, wa_ref, ba_ref, nw_ref, o_ref, st_ref, gk_ref, *, block, sub, q_scale):
    @pl.when(pl.program_id(2) == 0)
    def _():
        st_ref[...] = jnp.zeros_like(st_ref)

    z = jnp.dot(a_ref[...].astype(BF16), wa_ref[...], preferred_element_type=F32) + ba_ref[...]
    gk_ref[...] = (jnp.minimum(z, 0.0) - jnp.log1p(jnp.exp(-jnp.abs(z)))) * (1.0 / GLA_GATE_NORMALIZER)

    row = lax.broadcasted_iota(jnp.int32, (block, block), 0)
    col = lax.broadcasted_iota(jnp.int32, (block, block), 1)
    tri = (col <= row).astype(BF16)
    nw = nw_ref[...]

    def step(c, carry):
        r = pl.ds(pl.multiple_of(c * block, block), block)
        g_hi, g_mid, g_lo = _split3(gk_ref[r, :])
        b = (jnp.dot(tri, g_hi, preferred_element_type=F32) + jnp.dot(tri, g_mid, preferred_element_type=F32)
             + jnp.dot(tri, g_lo, preferred_element_type=F32))
        b_last = b[block - 1:block, :]
        q = q_ref[r, :].astype(F32) * q_scale
        k = k_ref[r, :].astype(F32)
        v = v_ref[r, :]
        st = st_ref[...]
        q_in = (q * jnp.exp(b)).astype(BF16)
        k_out = (k * jnp.exp(b_last - b)).astype(BF16)
        o_inter = lax.dot_general(q_in, st.astype(BF16), NT_DIMS, preferred_element_type=F32)
        st_ref[...] = st * jnp.exp(b_last) + lax.dot_general(v, k_out, TN_DIMS, preferred_element_type=F32)
        o_intra = []
        for i in range(block // sub):
            lo, hi = i * sub, (i + 1) * sub
            base = b[lo - 1:lo, :] if i else jnp.zeros_like(b_last)
            q_rel = (q[lo:hi] * jnp.exp(b[lo:hi] - base)).astype(BF16)
            k_rel = (k[:hi] * jnp.exp(base - b[:hi])).astype(BF16)
            s = lax.dot_general(q_rel, k_rel, NT_DIMS, preferred_element_type=F32)
            key = lax.broadcasted_iota(jnp.int32, (sub, hi), 1)
            qry = lax.broadcasted_iota(jnp.int32, (sub, hi), 0) + lo
            s = jnp.where(key <= qry, s, 0.0).astype(BF16)
            o_intra.append(jnp.dot(s, v[:hi], preferred_element_type=F32))
        o = o_inter + jnp.concatenate(o_intra, axis=0)
        gate = g_ref[r, :].astype(F32)
        o_ref[r, :] = (_rms(o, nw) * (gate * jax.nn.sigmoid(gate))).astype(o_ref.dtype)
        return carry

    lax.fori_loop(0, q_ref.shape[0] // block, step, 0)


def gla_core(qkvg, a, w_a2, b_a2, norm_w, *, batch, seq, tb):
    n = qkvg.shape[0]
    kd_total = w_a2.shape[1]
    dk = kd_total // GLA_HEADS
    vd_total = (qkvg.shape[1] - 2 * kd_total) // 2
    dv = vd_total // GLA_HEADS
    nt = seq // tb
    kq = kd_total // dk
    kv = 2 * kd_total // dv
    kg = kv + vd_total // dv
    body = functools.partial(_gla_body, block=min(tb, GLA_BLOCK), sub=GLA_CHUNK, q_scale=dk ** -0.5)
    rows = lambda b, h, t: b * nt + t
    return pl.pallas_call(
        body, grid=(batch, GLA_HEADS, nt),
        in_specs=[pl.BlockSpec((tb, dk), lambda b, h, t: (rows(b, h, t), h)),
                  pl.BlockSpec((tb, dk), lambda b, h, t: (rows(b, h, t), kq + h)),
                  pl.BlockSpec((tb, dv), lambda b, h, t: (rows(b, h, t), kv + h)),
                  pl.BlockSpec((tb, dv), lambda b, h, t: (rows(b, h, t), kg + h)),
                  pl.BlockSpec((tb, LANES), lambda b, h, t: (rows(b, h, t), 0)),
                  pl.BlockSpec((LANES, dk), lambda b, h, t: (0, h)),
                  pl.BlockSpec((1, dk), lambda b, h, t: (0, h)),
                  pl.BlockSpec((1, dv), lambda b, h, t: (0, 0))],
        out_specs=pl.BlockSpec((tb, dv), lambda b, h, t: (rows(b, h, t), h)),
        out_shape=jax.ShapeDtypeStruct((n, vd_total), BF16),
        scratch_shapes=[pltpu.VMEM((dv, dk), F32), pltpu.VMEM((tb, dk), F32)],
        compiler_params=_cparams("parallel", "parallel", "arbitrary"), name="gla_core")(
            qkvg, qkvg, qkvg, qkvg, a, w_a2, b_a2, norm_w)


def _pool_body(h_ref, hp_ref, nw_ref, w_ref, sc_ref, o_ref, x_ref, *, tiles_per_seq, group):
    i = pl.program_id(0)
    tt = h_ref.shape[0]
    h = h_ref[...]
    nw = nw_ref[...]
    first = (i % tiles_per_seq) == 0
    x_ref[0:POOL_HALO, :] = jnp.where(first, 0.0, _rms(hp_ref[...], nw))
    x_ref[POOL_HALO:, :] = _rms(h, nw)
    pos = (i % tiles_per_seq) * tt + lax.broadcasted_iota(jnp.int32, (tt, 1), 0)
    for gi, win in enumerate(POOL_WINDOWS):
        cols = slice(gi * group, (gi + 1) * group)
        acc = x_ref[POOL_HALO:POOL_HALO + tt, cols]
        cur = acc
        for back in range(1, win):
            acc = acc + x_ref[POOL_HALO - back:POOL_HALO - back + tt, cols]
        count = jnp.minimum(pos + 1, win).astype(F32)
        pooled = (acc / count - cur).astype(BF16)
        y = jnp.dot(pooled, w_ref[gi], preferred_element_type=F32)
        o_ref[:, cols] = h[:, cols] + y * sc_ref[:, cols]


def pool_mixer(h, nw, w_groups, scale, *, seq, tt):
    n, d = h.shape
    ng, group, _ = w_groups.shape
    tiles_per_seq = seq // tt
    halo_blocks = tt // POOL_HALO
    body = functools.partial(_pool_body, tiles_per_seq=tiles_per_seq, group=group)
    return pl.pallas_call(
        body, grid=(n // tt,),
        in_specs=[pl.BlockSpec((tt, d), lambda i: (i, 0)),
                  pl.BlockSpec((POOL_HALO, d), lambda i: (jnp.maximum(i * halo_blocks - 1, 0), 0)),
                  pl.BlockSpec((1, d), lambda i: (0, 0)),
                  pl.BlockSpec((ng, group, group), lambda i: (0, 0, 0)),
                  pl.BlockSpec((1, d), lambda i: (0, 0))],
        out_specs=pl.BlockSpec((tt, d), lambda i: (i, 0)),
        out_shape=jax.ShapeDtypeStruct((n, d), F32),
        scratch_shapes=[pltpu.VMEM((tt + POOL_HALO, d), F32)],
        compiler_params=_cparams("parallel"), name="pool_mixer")(h, h, nw, w_groups, scale)


def _diff_body(q_ref, k_ref, v_ref, lq1_ref, lk1_ref, lq2_ref, lk2_ref, nw_ref, o_ref,
               qt_ref, vt_ref, sa_ref, sb_ref, m_ref, l_ref, acc_ref, *, lambda_init):
    qi = pl.program_id(2)
    tq = q_ref.shape[0]
    dh = DIFF_HEAD_DIM
    nk = vt_ref.shape[0]

    @pl.when(qi == 0)
    def _():
        def transpose_v(j, carry):
            r = pl.ds(pl.multiple_of(j * tq, tq), tq)
            vt_ref[j] = v_ref[r, :].astype(F32).T.astype(BF16)
            return carry

        lax.fori_loop(0, nk, transpose_v, 0)

    qt_ref[...] = q_ref[...].astype(F32).T.astype(BF16)
    m_ref[...] = jnp.full_like(m_ref, -jnp.inf)
    l_ref[...] = jnp.zeros_like(l_ref)
    acc_ref[...] = jnp.zeros_like(acc_ref)

    def scores(j, s_ref):
        k = k_ref[pl.ds(pl.multiple_of(j * tq, tq), tq), :]
        for i in range(2):
            s_ref[i] = jnp.dot(k[:, i * dh:(i + 1) * dh], qt_ref[i * dh:(i + 1) * dh, :],
                               preferred_element_type=F32)

    def consume(j, s_ref, masked):
        vt = vt_ref[j]
        if masked:
            key = lax.broadcasted_iota(jnp.int32, (tq, tq), 0)
            qry = lax.broadcasted_iota(jnp.int32, (tq, tq), 1)
            keep = key <= qry
        for i in range(2):
            s = s_ref[i]
            if masked:
                s = jnp.where(keep, s, -jnp.inf)
            m_old = m_ref[i]
            m_new = jnp.maximum(m_old, jnp.max(s, axis=0, keepdims=True))
            alpha = jnp.exp2(m_old - m_new)
            p = jnp.exp2(s - m_new)
            l_ref[i] = alpha * l_ref[i] + jnp.sum(p, axis=0, keepdims=True)
            acc_ref[i] = alpha * acc_ref[i] + jnp.dot(vt, p.astype(BF16), preferred_element_type=F32)
            m_ref[i] = m_new

    def pipelined_pair(t, carry):
        j = 2 * t
        scores(j + 1, sb_ref)
        consume(j, sa_ref, False)
        scores(j + 2, sa_ref)
        consume(j + 1, sb_ref, False)
        return carry

    scores(0, sa_ref)
    lax.fori_loop(0, qi // 2, pipelined_pair, 0)

    @pl.when(qi % 2 == 0)
    def _():
        consume(qi, sa_ref, True)

    @pl.when(qi % 2 == 1)
    def _():
        scores(qi, sb_ref)
        consume(qi - 1, sa_ref, False)
        consume(qi, sb_ref, True)

    lam = (jnp.exp(jnp.sum(lq1_ref[...] * lk1_ref[...], axis=-1, keepdims=True))
           - jnp.exp(jnp.sum(lq2_ref[...] * lk2_ref[...], axis=-1, keepdims=True)) + lambda_init)
    ot = acc_ref[0] * (1.0 / l_ref[0]) - lam * (acc_ref[1] * (1.0 / l_ref[1]))
    ms = jnp.mean(ot * ot, axis=0, keepdims=True)
    ot = ot * lax.rsqrt(ms + NORM_EPS) * (nw_ref[...] * (1.0 - lambda_init))
    o_ref[...] = ot.T.astype(o_ref.dtype)


def diff_attention_core(qkv, lq1, lk1, lq2, lk2, norm_col, *, batch, seq, tq, lambda_init):
    n = qkv.shape[0]
    dh = DIFF_HEAD_DIM
    nq = seq // tq
    k_off = DIFF_HEADS
    v_off = 2 * DIFF_HEADS
    vec = pl.BlockSpec((1, dh), lambda b, h, q: (0, 0))
    return pl.pallas_call(
        functools.partial(_diff_body, lambda_init=lambda_init), grid=(batch, DIFF_HEADS, nq),
        in_specs=[pl.BlockSpec((tq, 2 * dh), lambda b, h, q: (b * nq + q, h)),
                  pl.BlockSpec((seq, 2 * dh), lambda b, h, q: (b, k_off + h)),
                  pl.BlockSpec((seq, 2 * dh), lambda b, h, q: (b, v_off + h)),
                  vec, vec, vec, vec,
                  pl.BlockSpec((2 * dh, 1), lambda b, h, q: (0, 0))],
        out_specs=pl.BlockSpec((tq, 2 * dh), lambda b, h, q: (b * nq + q, h)),
        out_shape=jax.ShapeDtypeStruct((n, DIFF_HEADS * 2 * dh), BF16),
        scratch_shapes=[pltpu.VMEM((2 * dh, tq), BF16), pltpu.VMEM((nq, 2 * dh, tq), BF16),
                        pltpu.VMEM((2, tq, tq), F32), pltpu.VMEM((2, tq, tq), F32), pltpu.VMEM((2, 1, tq), F32), pltpu.VMEM((2, 1, tq), F32), pltpu.VMEM((2, 2 * dh, tq), F32)],
        compiler_params=_cparams("parallel", "parallel", "arbitrary"), name="diff_attention")(
            qkv, qkv, qkv, lq1, lk1, lq2, lk2, norm_col)


def _ffn_body(h_ref, nw_ref, wg_ref, wu_ref, wd_ref, o_ref, xn_ref):
    @pl.when(pl.program_id(1) == 0)
    def _():
        h = h_ref[...]
        xn_ref[...] = _rms(h, nw_ref[...]).astype(BF16)
        o_ref[...] = h

    xn = xn_ref[...]
    gate = jnp.dot(xn, wg_ref[...], preferred_element_type=F32)
    up = jnp.dot(xn, wu_ref[...], preferred_element_type=F32)
    act = (gate * jax.nn.sigmoid(gate) * up).astype(BF16)
    o_ref[...] += jnp.dot(act, wd_ref[...], preferred_element_type=F32)


def ffn_residual(h, nw, w_gu, w_down, *, tm, tf):
    n, d = h.shape
    f = w_down.shape[0]
    nf = f // tf
    return pl.pallas_call(
        _ffn_body, grid=(n // tm, nf),
        in_specs=[pl.BlockSpec((tm, d), lambda i, j: (i, 0)), pl.BlockSpec((1, d), lambda i, j: (0, 0)),
                  pl.BlockSpec((d, tf), lambda i, j: (0, j)), pl.BlockSpec((d, tf), lambda i, j: (0, nf + j)),
                  pl.BlockSpec((tf, d), lambda i, j: (j, 0))],
        out_specs=pl.BlockSpec((tm, d), lambda i, j: (i, 0)),
        out_shape=jax.ShapeDtypeStruct((n, d), F32),
        scratch_shapes=[pltpu.VMEM((tm, d), BF16)],
        compiler_params=_cparams("parallel", "arbitrary"), name="ffn_residual")(h, nw, w_gu, w_gu, w_down)


def _pack_bf16_pairs(x):
    w = x.shape[1] // 2
    lo = pltpu.bitcast(x[:, :w].astype(BF16).astype(F32), jnp.uint32) >> 16
    hi = pltpu.bitcast(x[:, w:].astype(BF16).astype(F32), jnp.uint32)
    return hi | lo


def _unpack_bf16_pairs(u):
    lo = pltpu.bitcast(u << 16, F32).astype(BF16)
    hi = pltpu.bitcast(u & jnp.uint32(0xFFFF0000), F32).astype(BF16)
    return jnp.concatenate([lo, hi], axis=1)


def _router_body(h_ref, nw_ref, r_ref, xn_ref, info_ref, cnt_ref, carry_ref):
    i = pl.program_id(0)

    @pl.when(i == 0)
    def _():
        carry_ref[...] = jnp.zeros_like(carry_ref)

    xn = _rms(h_ref[...], nw_ref[...])
    xn_ref[...] = _pack_bf16_pairs(xn)
    x_hi, x_mid, x_lo = _split3(xn)
    dot = functools.partial(jnp.dot, preferred_element_type=F32)
    r3 = r_ref[...]
    a, b, c = dot(x_hi, r3), dot(x_mid, r3), dot(x_lo, r3)
    e = N_EXPERTS
    logits = ((c + pltpu.roll(b, LANES - e, 1) + pltpu.roll(a, LANES - 2 * e, 1))
              + (b + pltpu.roll(a, LANES - e, 1))) + a
    tm = logits.shape[0]
    lane = lax.broadcasted_iota(jnp.int32, logits.shape, 1)
    lane_f = lane.astype(F32)
    lg = jnp.where(lane < N_EXPERTS, logits, -jnp.inf)
    m1 = jnp.max(lg, axis=-1, keepdims=True)
    e1 = jnp.min(jnp.where(lg == m1, lane_f, float(LANES)), axis=-1, keepdims=True)
    lg2 = jnp.where(lane_f == e1, -jnp.inf, lg)
    m2 = jnp.max(lg2, axis=-1, keepdims=True)
    e2 = jnp.min(jnp.where(lg2 == m2, lane_f, float(LANES)), axis=-1, keepdims=True)
    ex = jnp.exp(m2 - m1)
    g1 = 1.0 / (1.0 + ex)
    g2 = ex / (1.0 + ex)
    hot1 = lane_f == e1
    hot2 = lane_f == e2
    onehot = jnp.where(hot1 | hot2, 1.0, 0.0)
    row = lax.broadcasted_iota(jnp.int32, (tm, tm), 0)
    col = lax.broadcasted_iota(jnp.int32, (tm, tm), 1)
    before = jnp.where(col < row, 1.0, 0.0).astype(BF16)
    cum = dot(before, onehot.astype(BF16)) + carry_ref[...]
    rank1 = jnp.sum(jnp.where(hot1, cum, 0.0), axis=-1, keepdims=True)
    rank2 = jnp.sum(jnp.where(hot2, cum, 0.0), axis=-1, keepdims=True)
    info = jnp.zeros(logits.shape, F32)
    for idx, val in enumerate((e1, e2, rank1, rank2, g1, g2)):
        info = jnp.where(lane == idx, val, info)
    info_ref[...] = info
    carry_ref[...] += jnp.sum(onehot, axis=0, keepdims=True)
    cnt_ref[...] = jnp.broadcast_to(carry_ref[...], cnt_ref.shape)


def moe_router(h, nw, router_pad, *, tm):
    n, d = h.shape
    return pl.pallas_call(
        _router_body, grid=(n // tm,),
        in_specs=[pl.BlockSpec((tm, d), lambda i: (i, 0)), pl.BlockSpec((1, d), lambda i: (0, 0)),
                  pl.BlockSpec((d, LANES), lambda i: (0, 0))],
        out_specs=[pl.BlockSpec((tm, d // 2), lambda i: (i, 0)), pl.BlockSpec((tm, LANES), lambda i: (i, 0)),
                   pl.BlockSpec((8, LANES), lambda i: (0, 0))],
        out_shape=[jax.ShapeDtypeStruct((n, d // 2), jnp.uint32), jax.ShapeDtypeStruct((n, LANES), F32),
                   jax.ShapeDtypeStruct((8, LANES), F32)],
        scratch_shapes=[pltpu.VMEM((1, LANES), F32)],
        compiler_params=_cparams("arbitrary"), name="moe_router")(h, nw, router_pad)


def _dispatch_body(p1_ref, p2_ref, ps_ref, pn_ref, ts_ref, tn_ref, x_ref, xs_ref, zero_ref, sem, zsem):
    tr = x_ref.shape[0]
    zr = zero_ref.shape[0]
    base = pl.program_id(0) * tr

    @pl.when(pl.program_id(0) == 0)
    def _():
        zero_ref[...] = jnp.zeros_like(zero_ref)

        def pad_copy(dst):
            return pltpu.make_async_copy(zero_ref.at[pl.ds(0, 1)], xs_ref.at[pl.ds(dst, 1)], zsem)

        def tail_copy(c):
            return pltpu.make_async_copy(zero_ref, xs_ref.at[pl.ds(pl.multiple_of(ts_ref[0] + c * zr, zr), zr)], zsem)

        for e in range(N_EXPERTS):
            lax.fori_loop(0, pn_ref[e], lambda r, c, e=e: (pad_copy(ps_ref[e] + r).start(), c)[1], 0)
        lax.fori_loop(0, tn_ref[0], lambda c, carry: (tail_copy(c).start(), carry)[1], 0)
        for e in range(N_EXPERTS):
            lax.fori_loop(0, pn_ref[e], lambda r, c, e=e: (pad_copy(ps_ref[e] + r).wait(), c)[1], 0)
        lax.fori_loop(0, tn_ref[0], lambda c, carry: (tail_copy(c).wait(), carry)[1], 0)

    def row_copy(r, dst):
        return pltpu.make_async_copy(x_ref.at[pl.ds(r, 1)], xs_ref.at[pl.ds(dst, 1)], sem)

    def issue(r, carry):
        row_copy(r, p1_ref[base + r]).start()
        row_copy(r, p2_ref[base + r]).start()
        return carry

    def drain(r, carry):
        row_copy(r, p1_ref[base + r]).wait()
        row_copy(r, p2_ref[base + r]).wait()
        return carry

    lax.fori_loop(0, tr, issue, 0, unroll=DMA_LOOP_UNROLL)
    lax.fori_loop(0, tr, drain, 0, unroll=DMA_LOOP_UNROLL)


def moe_dispatch(x32, pos1, pos2, pad_start, pad_len, tail_start, tail_chunks, *, p_rows, tr, zr):
    n, w = x32.shape
    grid_spec = pltpu.PrefetchScalarGridSpec(
        num_scalar_prefetch=6, grid=(n // tr,),
        in_specs=[pl.BlockSpec((tr, w), lambda i, *_: (i, 0))],
        out_specs=pl.BlockSpec(memory_space=pl.ANY),
        scratch_shapes=[pltpu.VMEM((zr, w), x32.dtype), pltpu.SemaphoreType.DMA, pltpu.SemaphoreType.DMA])
    return pl.pallas_call(
        _dispatch_body, grid_spec=grid_spec, out_shape=jax.ShapeDtypeStruct((p_rows, w), x32.dtype),
        compiler_params=_cparams("arbitrary"), name="moe_dispatch")(
            pos1, pos2, pad_start, pad_len, tail_start, tail_chunks, x32)


def _moe_ffn_body(te_ref, nv_ref, na_ref, x_ref, wg_ref, wu_ref, wd_ref, o_ref, xb_ref, acc_ref, *, parts):
    i = pl.program_id(0)
    j = pl.program_id(1)
    part_rows = x_ref.shape[0] // parts

    @pl.when(j == 0)
    def _():
        acc_ref[...] = jnp.zeros_like(acc_ref)
        xb_ref[...] = _unpack_bf16_pairs(x_ref[...])

    def leading_rows(rows):
        x = xb_ref[:rows, :]
        gate = jnp.dot(x, wg_ref[0].astype(BF16), preferred_element_type=F32)
        up = jnp.dot(x, wu_ref[0].astype(BF16), preferred_element_type=F32)
        act = (gate * jax.nn.sigmoid(gate) * up).astype(BF16)
        acc_ref[:rows, :] += jnp.dot(act, wd_ref[0].astype(BF16), preferred_element_type=F32)

    for k in range(1, parts + 1):
        lo, hi = (k - 1) * part_rows, k * part_rows
        pl.when((nv_ref[i] > lo) & (nv_ref[i] <= hi))(functools.partial(leading_rows, hi))

    @pl.when(j == pl.num_programs(1) - 1)
    def _():
        o_ref[...] = _pack_bf16_pairs(acc_ref[...])


def moe_ffn(xs32, tile_expert, tile_valid, n_active, w_gu, w_down, *, tm, tf, parts):
    p, w = xs32.shape
    d = 2 * w
    f = w_down.shape[1]
    nf = f // tf

    def frozen(i, j, na):
        return jnp.where(i < na[0], j, nf - 1)

    grid_spec = pltpu.PrefetchScalarGridSpec(
        num_scalar_prefetch=3, grid=(p // tm, nf),
        in_specs=[pl.BlockSpec((tm, w), lambda i, j, te, nv, na: (jnp.minimum(i, na[0] - 1), 0)),
                  pl.BlockSpec((1, d, tf), lambda i, j, te, nv, na: (te[i], 0, frozen(i, j, na))),
                  pl.BlockSpec((1, d, tf), lambda i, j, te, nv, na: (te[i], 0, nf + frozen(i, j, na))),
                  pl.BlockSpec((1, tf, d), lambda i, j, te, nv, na: (te[i], frozen(i, j, na), 0))],
        out_specs=pl.BlockSpec((tm, w), lambda i, j, te, nv, na: (i, 0)),
        scratch_shapes=[pltpu.VMEM((tm, d), BF16), pltpu.VMEM((tm, d), F32)])
    return pl.pallas_call(
        functools.partial(_moe_ffn_body, parts=parts), grid_spec=grid_spec,
        out_shape=jax.ShapeDtypeStruct((p, w), jnp.uint32),
        compiler_params=_cparams("parallel", "arbitrary"), name="moe_ffn")(
            tile_expert, tile_valid, n_active, xs32, w_gu, w_gu, w_down)


def _combine_body(p1_ref, p2_ref, h_ref, info_ref, y_ref, o_ref, buf1, buf2, sem):
    tr = h_ref.shape[0]
    base = pl.program_id(0) * tr

    def row_copy(src, buf, r):
        return pltpu.make_async_copy(y_ref.at[pl.ds(src, 1)], buf.at[pl.ds(r, 1)], sem)

    def issue(r, carry):
        row_copy(p1_ref[base + r], buf1, r).start()
        row_copy(p2_ref[base + r], buf2, r).start()
        return carry

    def drain(r, carry):
        row_copy(p1_ref[base + r], buf1, r).wait()
        row_copy(p2_ref[base + r], buf2, r).wait()
        return carry

    lax.fori_loop(0, tr, issue, 0, unroll=DMA_LOOP_UNROLL)
    lax.fori_loop(0, tr, drain, 0, unroll=DMA_LOOP_UNROLL)
    info = info_ref[...]
    g1 = info[:, 4:5]
    g2 = info[:, 5:6]
    y1 = _unpack_bf16_pairs(buf1[...]).astype(F32)
    y2 = _unpack_bf16_pairs(buf2[...]).astype(F32)
    o_ref[...] = h_ref[...] + (g1 * y1 + g2 * y2)


def moe_combine(h, info, y32, pos1, pos2, *, tr):
    n, d = h.shape
    w = y32.shape[1]
    grid_spec = pltpu.PrefetchScalarGridSpec(
        num_scalar_prefetch=2, grid=(n // tr,),
        in_specs=[pl.BlockSpec((tr, d), lambda i, p1, p2: (i, 0)), pl.BlockSpec((tr, LANES), lambda i, p1, p2: (i, 0)),
                  pl.BlockSpec(memory_space=pl.ANY)],
        out_specs=pl.BlockSpec((tr, d), lambda i, p1, p2: (i, 0)),
        scratch_shapes=[pltpu.VMEM((tr, w), y32.dtype), pltpu.VMEM((tr, w), y32.dtype), pltpu.SemaphoreType.DMA])
    return pl.pallas_call(
        _combine_body, grid_spec=grid_spec, out_shape=jax.ShapeDtypeStruct((n, d), F32),
        compiler_params=_cparams("arbitrary"), name="moe_combine")(pos1, pos2, h, info, y32)


def moe_residual(h, nw, router, w_gu, w_down, *, tm_route, tr, tm, tf):
    n, d = h.shape
    router_terms = jnp.concatenate(_split3(router.astype(F32)), axis=1)
    router_pad = jnp.zeros((d, LANES), BF16).at[:, :3 * N_EXPERTS].set(router_terms)
    x32, info, counts = moe_router(h, nw, router_pad, tm=tm_route)
    cnt = counts[0, :N_EXPERTS].astype(jnp.int32)
    padded = (cnt + tm - 1) // tm * tm
    ends = jnp.cumsum(padded)
    offs = ends - padded
    e1 = info[:, 0].astype(jnp.int32)
    e2 = info[:, 1].astype(jnp.int32)
    pos1 = offs[e1] + info[:, 2].astype(jnp.int32)
    pos2 = offs[e2] + info[:, 3].astype(jnp.int32)
    p_rows = 2 * n + N_EXPERTS * tm
    n_tiles = p_rows // tm
    tile_start = jnp.arange(n_tiles, dtype=jnp.int32) * tm
    tile_expert = jnp.minimum(jnp.sum(tile_start[:, None] >= ends[None, :], axis=1), N_EXPERTS - 1).astype(jnp.int32)
    tile_valid = jnp.clip((offs + cnt)[tile_expert] - tile_start, 0, tm).astype(jnp.int32)
    n_active = (ends[-1:] // tm).astype(jnp.int32)
    zr = min(tm, 256)
    xs32 = moe_dispatch(x32, pos1, pos2, offs + cnt, padded - cnt, ends[-1:], (p_rows - ends[-1:]) // zr,
                        p_rows=p_rows, tr=tr, zr=zr)
    y32 = moe_ffn(xs32, tile_expert, tile_valid, n_active, w_gu, w_down, tm=tm, tf=tf, parts=MOE_TILE_PARTS)
    return moe_combine(h, info, y32, pos1, pos2, tr=tr)


def _ple_body(h_ref, nw_ref, wg_ref, bg_ref, p_ref, wp_ref, *rest):
    o_ref = rest[-1]
    h = h_ref[...]
    hn = _rms(h, nw_ref[...]).astype(BF16)
    z = jnp.dot(hn, wg_ref[...], preferred_element_type=F32) + bg_ref[...]
    proj = jnp.dot(p_ref[...].astype(BF16), wp_ref[...], preferred_element_type=F32)
    out = h + jax.nn.sigmoid(z) * proj
    if len(rest) == 2:
        out = _rms(out, rest[0][...])
    o_ref[...] = out


def ple_residual(h, nw, w_gate, b_gate, p_i, w_proj, *, tm, final_norm_w=None):
    n, d = h.shape
    pd = p_i.shape[1]
    in_specs = [pl.BlockSpec((tm, d), lambda i: (i, 0)), _resident((1, d)), _resident((d, d)), _resident((1, d)),
                pl.BlockSpec((tm, pd), lambda i: (i, 0)), _resident((pd, d))]
    args = [h, nw, w_gate, b_gate, p_i, w_proj]
    if final_norm_w is not None:
        in_specs.append(_resident((1, d)))
        args.append(final_norm_w)
    return pl.pallas_call(
        _ple_body, grid=(n // tm,), in_specs=in_specs,
        out_specs=pl.BlockSpec((tm, d), lambda i: (i, 0)),
        out_shape=jax.ShapeDtypeStruct((n, d), F32),
        compiler_params=_cparams("parallel"), name="ple_residual")(*args)


def _row(v):
    return v.reshape(1, -1).astype(F32)


def _tile(n, want):
    return min(n, want)


def _gla_layer(h, norm_mix, w_in, w_a2, b_a2, gla_norm, w_o, *, batch, seq):
    n = h.shape[0]
    main = w_in.shape[1] - GLA_GATE_RANK
    w_main = w_in[:, :main].astype(BF16)
    w_a = jnp.zeros((w_in.shape[0], LANES), BF16).at[:, :GLA_GATE_RANK].set(w_in[:, main:].astype(BF16))
    w_a2_pad = jnp.zeros((LANES, w_a2.shape[1]), BF16).at[:GLA_GATE_RANK].set(w_a2.astype(BF16))
    qkvg, a = norm_matmul_extra(h, _row(norm_mix), w_main, w_a, tm=_tile(n, 256))
    o = gla_core(qkvg, a, w_a2_pad, _row(b_a2), _row(gla_norm), batch=batch, seq=seq, tb=_tile(seq, 1024))
    return matmul_residual(o, w_o.astype(BF16), h, tm=_tile(n, 512))


def _diff_layer(h, positions, norm_mix, w_in, lq1, lk1, lq2, lk2, diff_norm, w_o, *, batch, seq, lambda_init):
    n = h.shape[0]
    tabs = rope_tables(positions, tm=_tile(n, 1024))
    qk_width = 2 * DIFF_HEADS * DIFF_HEAD_DIM
    qkv = norm_matmul_rope(h, _row(norm_mix), w_in.astype(BF16), tabs, tm=_tile(n, 256),
                           q_width=qk_width, k_width=qk_width, q_scale=DIFF_HEAD_DIM ** -0.5 * math.log2(math.e))
    o = diff_attention_core(qkv, _row(lq1), _row(lk1), _row(lq2), _row(lk2), diff_norm.reshape(-1, 1).astype(F32),
                            batch=batch, seq=seq, tq=_tile(seq, 512), lambda_init=lambda_init)
    return matmul_residual(o, w_o.astype(BF16), h, tm=_tile(n, 512))


def _diff_lambda_init(layer_idx):
    return 0.8 - 0.6 * math.exp(-0.3 * layer_idx)


def kernel(x, p, positions, l0_norm_mix, l0_gla_w_in, l0_gla_w_a2, l0_gla_b_a2, l0_gla_norm, l0_gla_w_o, l0_norm_ffn, l0_ffn_w_gu, l0_ffn_w_down, l0_ple_norm, l0_ple_w_gate, l0_ple_b_gate, l0_ple_w_proj, l1_norm_mix, l1_pool_w, l1_pool_scale, l1_norm_ffn, l1_moe_router, l1_moe_w_gu, l1_moe_w_down, l1_ple_norm, l1_ple_w_gate, l1_ple_b_gate, l1_ple_w_proj, l2_norm_mix, l2_diff_w_in, l2_diff_lq1, l2_diff_lk1, l2_diff_lq2, l2_diff_lk2, l2_diff_norm, l2_diff_w_o, l2_norm_ffn, l2_ffn_w_gu, l2_ffn_w_down, l2_ple_norm, l2_ple_w_gate, l2_ple_b_gate, l2_ple_w_proj, l3_norm_mix, l3_gla_w_in, l3_gla_w_a2, l3_gla_b_a2, l3_gla_norm, l3_gla_w_o, l3_norm_ffn, l3_moe_router, l3_moe_w_gu, l3_moe_w_down, l3_ple_norm, l3_ple_w_gate, l3_ple_b_gate, l3_ple_w_proj, final_norm):
    batch, seq, d = x.shape
    n = batch * seq
    h = x.reshape(n, d)
    pf = p.reshape(p.shape[0], n, p.shape[-1])
    tm = _tile(n, 512)

    def ffn(h, nw, w_gu, w_down):
        return ffn_residual(h, _row(nw), w_gu.astype(BF16), w_down.astype(BF16), tm=tm, tf=512)

    def moe(h, nw, router, w_gu, w_down):
        return moe_residual(h, _row(nw), router, w_gu, w_down, tm_route=tm, tr=_tile(n, 256), tm=_tile(n, 1024), tf=256)

    def ple(h, i, nw, w_gate, b_gate, w_proj, final_norm_w=None):
        return ple_residual(h, _row(nw), w_gate.astype(BF16), _row(b_gate), pf[i], w_proj.astype(BF16), tm=tm,
                            final_norm_w=final_norm_w)

    h = _gla_layer(h, l0_norm_mix, l0_gla_w_in, l0_gla_w_a2, l0_gla_b_a2, l0_gla_norm, l0_gla_w_o, batch=batch, seq=seq)
    h = ffn(h, l0_norm_ffn, l0_ffn_w_gu, l0_ffn_w_down)
    h = ple(h, 0, l0_ple_norm, l0_ple_w_gate, l0_ple_b_gate, l0_ple_w_proj)

    h = pool_mixer(h, _row(l1_norm_mix), l1_pool_w.astype(BF16), _row(l1_pool_scale), seq=seq, tt=_tile(seq, 512))
    h = moe(h, l1_norm_ffn, l1_moe_router, l1_moe_w_gu, l1_moe_w_down)
    h = ple(h, 1, l1_ple_norm, l1_ple_w_gate, l1_ple_b_gate, l1_ple_w_proj)

    h = _diff_layer(h, positions, l2_norm_mix, l2_diff_w_in, l2_diff_lq1, l2_diff_lk1, l2_diff_lq2, l2_diff_lk2,
                    l2_diff_norm, l2_diff_w_o, batch=batch, seq=seq, lambda_init=_diff_lambda_init(2))
    h = ffn(h, l2_norm_ffn, l2_ffn_w_gu, l2_ffn_w_down)
    h = ple(h, 2, l2_ple_norm, l2_ple_w_gate, l2_ple_b_gate, l2_ple_w_proj)

    h = _gla_layer(h, l3_norm_mix, l3_gla_w_in, l3_gla_w_a2, l3_gla_b_a2, l3_gla_norm, l3_gla_w_o, batch=batch, seq=seq)
    h = moe(h, l3_norm_ffn, l3_moe_router, l3_moe_w_gu, l3_moe_w_down)
    h = ple(h, 3, l3_ple_norm, l3_ple_w_gate, l3_ple_b_gate, l3_ple_w_proj, final_norm_w=_row(final_norm))
    return h.reshape(batch, seq, d)
```

```python
import functools
import math

import jax
import jax.numpy as jnp
from jax import lax
from jax.experimental import pallas as pl
from jax.experimental.pallas import tpu as pltpu

F32 = jnp.float32
BF16 = jnp.bfloat16

NORM_EPS = 1e-6
LANES = 128
V7X_VMEM_BYTES = 64 * 1024 * 1024
VMEM_LIMIT = V7X_VMEM_BYTES * 7 // 8

GLA_HEADS = 4
GLA_GATE_RANK = 16
GLA_GATE_NORMALIZER = 16.0
GLA_CHUNK = 64
GLA_BLOCK = 256
POOL_WINDOWS = (2, 4, 8, 16)
POOL_HALO = 16
DIFF_HEADS = 8
DIFF_HEAD_DIM = 128
ROPE_THETA = 500000.0
ROPE_DIMS = 32
N_EXPERTS = 8
MOE_TILE_PARTS = 2
DMA_LOOP_UNROLL = 8

NT_DIMS = (((1,), (1,)), ((), ()))
TN_DIMS = (((0,), (0,)), ((), ()))


def _cparams(*sem):
    return pltpu.CompilerParams(dimension_semantics=sem, vmem_limit_bytes=VMEM_LIMIT)


def _rms(x, w):
    ms = jnp.mean(x * x, axis=-1, keepdims=True)
    return x * lax.rsqrt(ms + NORM_EPS) * w


def _split3(x):
    hi = x.astype(BF16)
    r1 = x - hi.astype(F32)
    mid = r1.astype(BF16)
    lo = (r1 - mid.astype(F32)).astype(BF16)
    return hi, mid, lo


def _resident(shape):
    return pl.BlockSpec(shape, lambda i: (0,) * len(shape), pipeline_mode=pl.Buffered(1))


def _norm_mm_extra_body(h_ref, nw_ref, w_ref, wx_ref, o_ref, ox_ref):
    xn = _rms(h_ref[...], nw_ref[...]).astype(BF16)
    ox_ref[...] = jnp.dot(xn, wx_ref[...], preferred_element_type=F32)
    o_ref[...] = jnp.dot(xn, w_ref[...], preferred_element_type=F32).astype(o_ref.dtype)


def norm_matmul_extra(h, nw, w, w_extra, *, tm):
    n, d = h.shape
    nout = w.shape[1]
    nx = w_extra.shape[1]
    return pl.pallas_call(
        _norm_mm_extra_body, grid=(n // tm,),
        in_specs=[pl.BlockSpec((tm, d), lambda i: (i, 0)), _resident((1, d)), _resident((d, nout)), _resident((d, nx))],
        out_specs=[pl.BlockSpec((tm, nout), lambda i: (i, 0)), pl.BlockSpec((tm, nx), lambda i: (i, 0))],
        out_shape=[jax.ShapeDtypeStruct((n, nout), BF16), jax.ShapeDtypeStruct((n, nx), F32)],
        compiler_params=_cparams("parallel"), name="norm_matmul_extra")(h, nw, w, w_extra)


def _rope_tab_body(pos_ref, invf_ref, cos_ref, sina_ref, sinb_ref):
    ang = pos_ref[...].astype(F32) * invf_ref[...]
    lane = lax.broadcasted_iota(jnp.int32, ang.shape, 1)
    half = ROPE_DIMS // 2
    s = jnp.sin(ang)
    cos_ref[...] = jnp.cos(ang)
    sina_ref[...] = jnp.where((lane >= half) & (lane < ROPE_DIMS), s, 0.0)
    sinb_ref[...] = jnp.where(lane < half, -s, 0.0)


def rope_tables(positions, *, tm):
    n = positions.size
    half = ROPE_DIMS // 2
    inv_freq = ROPE_THETA ** (-jnp.arange(half, dtype=F32) * 2.0 / ROPE_DIMS)
    invf = jnp.zeros((1, LANES), F32).at[0, :half].set(inv_freq).at[0, half:ROPE_DIMS].set(inv_freq)
    pos = positions.reshape(n, 1)
    tab = jax.ShapeDtypeStruct((n, LANES), F32)
    t_spec = pl.BlockSpec((tm, LANES), lambda i: (i, 0))
    return pl.pallas_call(
        _rope_tab_body, grid=(n // tm,),
        in_specs=[pl.BlockSpec((tm, 1), lambda i: (i, 0)), pl.BlockSpec((1, LANES), lambda i: (0, 0))],
        out_specs=[t_spec, t_spec, t_spec], out_shape=[tab, tab, tab],
        compiler_params=_cparams("parallel"), name="rope_tables")(pos, invf)


def _norm_mm_rope_body(h_ref, nw_ref, w_ref, cos_ref, sina_ref, sinb_ref, o_ref, *, q_heads, k_heads, q_scale):
    xn = _rms(h_ref[...], nw_ref[...]).astype(BF16)
    acc = jnp.dot(xn, w_ref[...], preferred_element_type=F32)
    half = ROPE_DIMS // 2
    c, sa, sb = cos_ref[...], sina_ref[...], sinb_ref[...]
    cq, saq, sbq = c * q_scale, sa * q_scale, sb * q_scale
    for t in range(q_heads + k_heads):
        xj = acc[:, t * LANES:(t + 1) * LANES]
        tc, tsa, tsb = (cq, saq, sbq) if t < q_heads else (c, sa, sb)
        r = xj * tc + pltpu.roll(xj, half, 1) * tsa + pltpu.roll(xj, LANES - half, 1) * tsb
        o_ref[:, t * LANES:(t + 1) * LANES] = r.astype(o_ref.dtype)
    rest = (q_heads + k_heads) * LANES
    o_ref[:, rest:] = acc[:, rest:].astype(o_ref.dtype)


def norm_matmul_rope(h, nw, w, tabs, *, tm, q_width, k_width, q_scale):
    n, d = h.shape
    nout = w.shape[1]
    body = functools.partial(_norm_mm_rope_body, q_heads=q_width // LANES, k_heads=k_width // LANES, q_scale=q_scale)
    t_spec = pl.BlockSpec((tm, LANES), lambda i: (i, 0))
    return pl.pallas_call(
        body, grid=(n // tm,),
        in_specs=[pl.BlockSpec((tm, d), lambda i: (i, 0)), _resident((1, d)), _resident((d, nout)),
                  t_spec, t_spec, t_spec],
        out_specs=pl.BlockSpec((tm, nout), lambda i: (i, 0)),
        out_shape=jax.ShapeDtypeStruct((n, nout), BF16),
        compiler_params=_cparams("parallel"), name="norm_matmul_rope")(h, nw, w, *tabs)


def _mm_res_body(a_ref, w_ref, h_ref, o_ref):
    o_ref[...] = h_ref[...] + jnp.dot(a_ref[...], w_ref[...], preferred_element_type=F32)


def matmul_residual(a, w, h, *, tm):
    n, k = a.shape
    d = w.shape[1]
    return pl.pallas_call(
        _mm_res_body, grid=(n // tm,),
        in_specs=[pl.BlockSpec((tm, k), lambda i: (i, 0)), _resident((k, d)), pl.BlockSpec((tm, d), lambda i: (i, 0))],
        out_specs=pl.BlockSpec((tm, d), lambda i: (i, 0)),
        out_shape=jax.ShapeDtypeStruct((n, d), F32),
        compiler_params=_cparams("parallel"), name="matmul_residual")(a, w, h)


def _gla_body(q_ref, k_ref, v_ref, g_ref, a_ref, wa_ref, ba_ref, nw_ref, o_ref, st_ref, gk_ref, *, block, sub, q_scale):
    @pl.when(pl.program_id(2) == 0)
    def _():
        st_ref[...] = jnp.zeros_like(st_ref)

    z = jnp.dot(a_ref[...].astype(BF16), wa_ref[...], preferred_element_type=F32) + ba_ref[...]
    gk_ref[...] = (jnp.minimum(z, 0.0) - jnp.log1p(jnp.exp(-jnp.abs(z)))) * (1.0 / GLA_GATE_NORMALIZER)

    row = lax.broadcasted_iota(jnp.int32, (block, block), 0)
    col = lax.broadcasted_iota(jnp.int32, (block, block), 1)
    tri = (col <= row).astype(BF16)
    nw = nw_ref[...]

    def step(c, carry):
        r = pl.ds(pl.multiple_of(c * block, block), block)
        g_hi, g_mid, g_lo = _split3(gk_ref[r, :])
        b = (jnp.dot(tri, g_hi, preferred_element_type=F32) + jnp.dot(tri, g_mid, preferred_element_type=F32)
             + jnp.dot(tri, g_lo, preferred_element_type=F32))
        b_last = b[block - 1:block, :]
        q = q_ref[r, :].astype(F32) * q_scale
        k = k_ref[r, :].astype(F32)
        v = v_ref[r, :]
        st = st_ref[...]
        q_in = (q * jnp.exp(b)).astype(BF16)
        k_out = (k * jnp.exp(b_last - b)).astype(BF16)
        o_inter = lax.dot_general(q_in, st.astype(BF16), NT_DIMS, preferred_element_type=F32)
        st_ref[...] = st * jnp.exp(b_last) + lax.dot_general(v, k_out, TN_DIMS, preferred_element_type=F32)
        o_intra = []
        for i in range(block // sub):
            lo, hi = i * sub, (i + 1) * sub
            base = b[lo - 1:lo, :] if i else jnp.zeros_like(b_last)
            q_rel = (q[lo:hi] * jnp.exp(b[lo:hi] - base)).astype(BF16)
            k_rel = (k[:hi] * jnp.exp(base - b[:hi])).astype(BF16)
            s = lax.dot_general(q_rel, k_rel, NT_DIMS, preferred_element_type=F32)
            key = lax.broadcasted_iota(jnp.int32, (sub, hi), 1)
            qry = lax.broadcasted_iota(jnp.int32, (sub, hi), 0) + lo
            s = jnp.where(key <= qry, s, 0.0).astype(BF16)
            o_intra.append(jnp.dot(s, v[:hi], preferred_element_type=F32))
        o = o_inter + jnp.concatenate(o_intra, axis=0)
        gate = g_ref[r, :].astype(F32)
        o_ref[r, :] = (_rms(o, nw) * (gate * jax.nn.sigmoid(gate))).astype(o_ref.dtype)
        return carry

    lax.fori_loop(0, q_ref.shape[0] // block, step, 0)


def gla_core(qkvg, a, w_a2, b_a2, norm_w, *, batch, seq, tb):
    n = qkvg.shape[0]
    kd_total = w_a2.shape[1]
    dk = kd_total // GLA_HEADS
    vd_total = (qkvg.shape[1] - 2 * kd_total) // 2
    dv = vd_total // GLA_HEADS
    nt = seq // tb
    kq = kd_total // dk
    kv = 2 * kd_total // dv
    kg = kv + vd_total // dv
    body = functools.partial(_gla_body, block=min(tb, GLA_BLOCK), sub=GLA_CHUNK, q_scale=dk ** -0.5)
    rows = lambda b, h, t: b * nt + t
    return pl.pallas_call(
        body, grid=(batch, GLA_HEADS, nt),
        in_specs=[pl.BlockSpec((tb, dk), lambda b, h, t: (rows(b, h, t), h)),
                  pl.BlockSpec((tb, dk), lambda b, h, t: (rows(b, h, t), kq + h)),
                  pl.BlockSpec((tb, dv), lambda b, h, t: (rows(b, h, t), kv + h)),
                  pl.BlockSpec((tb, dv), lambda b, h, t: (rows(b, h, t), kg + h)),
                  pl.BlockSpec((tb, LANES), lambda b, h, t: (rows(b, h, t), 0)),
                  pl.BlockSpec((LANES, dk), lambda b, h, t: (0, h)),
                  pl.BlockSpec((1, dk), lambda b, h, t: (0, h)),
                  pl.BlockSpec((1, dv), lambda b, h, t: (0, 0))],
        out_specs=pl.BlockSpec((tb, dv), lambda b, h, t: (rows(b, h, t), h)),
        out_shape=jax.ShapeDtypeStruct((n, vd_total), BF16),
        scratch_shapes=[pltpu.VMEM((dv, dk), F32), pltpu.VMEM((tb, dk), F32)],
        compiler_params=_cparams("parallel", "parallel", "arbitrary"), name="gla_core")(
            qkvg, qkvg, qkvg, qkvg, a, w_a2, b_a2, norm_w)


def _pool_body(h_ref, hp_ref, nw_ref, w_ref, sc_ref, o_ref, x_ref, *, tiles_per_seq, group):
    i = pl.program_id(0)
    tt = h_ref.shape[0]
    h = h_ref[...]
    nw = nw_ref[...]
    first = (i % tiles_per_seq) == 0
    x_ref[0:POOL_HALO, :] = jnp.where(first, 0.0, _rms(hp_ref[...], nw))
    x_ref[POOL_HALO:, :] = _rms(h, nw)
    pos = (i % tiles_per_seq) * tt + lax.broadcasted_iota(jnp.int32, (tt, 1), 0)
    for gi, win in enumerate(POOL_WINDOWS):
        cols = slice(gi * group, (gi + 1) * group)
        acc = x_ref[POOL_HALO:POOL_HALO + tt, cols]
        cur = acc
        for back in range(1, win):
            acc = acc + x_ref[POOL_HALO - back:POOL_HALO - back + tt, cols]
        count = jnp.minimum(pos + 1, win).astype(F32)
        pooled = (acc / count - cur).astype(BF16)
        y = jnp.dot(pooled, w_ref[gi], preferred_element_type=F32)
        o_ref[:, cols] = h[:, cols] + y * sc_ref[:, cols]


def pool_mixer(h, nw, w_groups, scale, *, seq, tt):
    n, d = h.shape
    ng, group, _ = w_groups.shape
    tiles_per_seq = seq // tt
    halo_blocks = tt // POOL_HALO
    body = functools.partial(_pool_body, tiles_per_seq=tiles_per_seq, group=group)
    return pl.pallas_call(
        body, grid=(n // tt,),
        in_specs=[pl.BlockSpec((tt, d), lambda i: (i, 0)),
                  pl.BlockSpec((POOL_HALO, d), lambda i: (jnp.maximum(i * halo_blocks - 1, 0), 0)),
                  pl.BlockSpec((1, d), lambda i: (0, 0)),
                  pl.BlockSpec((ng, group, group), lambda i: (0, 0, 0)),
                  pl.BlockSpec((1, d), lambda i: (0, 0))],
        out_specs=pl.BlockSpec((tt, d), lambda i: (i, 0)),
        out_shape=jax.ShapeDtypeStruct((n, d), F32),
        scratch_shapes=[pltpu.VMEM((tt + POOL_HALO, d), F32)],
        compiler_params=_cparams("parallel"), name="pool_mixer")(h, h, nw, w_groups, scale)


def _diff_body(q_ref, k_ref, v_ref, lq1_ref, lk1_ref, lq2_ref, lk2_ref, nw_ref, o_ref,
               qt_ref, vt_ref, sa_ref, sb_ref, m_ref, l_ref, acc_ref, *, lambda_init):
    qi = pl.program_id(2)
    tq = q_ref.shape[0]
    dh = DIFF_HEAD_DIM
    nk = vt_ref.shape[0]

    @pl.when(qi == 0)
    def _():
        def transpose_v(j, carry):
            r = pl.ds(pl.multiple_of(j * tq, tq), tq)
            vt_ref[j] = v_ref[r, :].astype(F32).T.astype(BF16)
            return carry

        lax.fori_loop(0, nk, transpose_v, 0)

    qt_ref[...] = q_ref[...].astype(F32).T.astype(BF16)
    m_ref[...] = jnp.full_like(m_ref, -jnp.inf)
    l_ref[...] = jnp.zeros_like(l_ref)
    acc_ref[...] = jnp.zeros_like(acc_ref)

    def scores(j, s_ref):
        k = k_ref[pl.ds(pl.multiple_of(j * tq, tq), tq), :]
        for i in range(2):
            s_ref[i] = jnp.dot(k[:, i * dh:(i + 1) * dh], qt_ref[i * dh:(i + 1) * dh, :],
                               preferred_element_type=F32)

    def consume(j, s_ref, masked):
        vt = vt_ref[j]
        if masked:
            key = lax.broadcasted_iota(jnp.int32, (tq, tq), 0)
            qry = lax.broadcasted_iota(jnp.int32, (tq, tq), 1)
            keep = key <= qry
        for i in range(2):
            s = s_ref[i]
            if masked:
                s = jnp.where(keep, s, -jnp.inf)
            m_old = m_ref[i]
            m_new = jnp.maximum(m_old, jnp.max(s, axis=0, keepdims=True))
            alpha = jnp.exp2(m_old - m_new)
            p = jnp.exp2(s - m_new)
            l_ref[i] = alpha * l_ref[i] + jnp.sum(p, axis=0, keepdims=True)
            acc_ref[i] = alpha * acc_ref[i] + jnp.dot(vt, p.astype(BF16), preferred_element_type=F32)
            m_ref[i] = m_new

    def pipelined_pair(t, carry):
        j = 2 * t
        scores(j + 1, sb_ref)
        consume(j, sa_ref, False)
        scores(j + 2, sa_ref)
        consume(j + 1, sb_ref, False)
        return carry

    scores(0, sa_ref)
    lax.fori_loop(0, qi // 2, pipelined_pair, 0)

    @pl.when(qi % 2 == 0)
    def _():
        consume(qi, sa_ref, True)

    @pl.when(qi % 2 == 1)
    def _():
        scores(qi, sb_ref)
        consume(qi - 1, sa_ref, False)
        consume(qi, sb_ref, True)

    lam = (jnp.exp(jnp.sum(lq1_ref[...] * lk1_ref[...], axis=-1, keepdims=True))
           - jnp.exp(jnp.sum(lq2_ref[...] * lk2_ref[...], axis=-1, keepdims=True)) + lambda_init)
    ot = acc_ref[0] * (1.0 / l_ref[0]) - lam * (acc_ref[1] * (1.0 / l_ref[1]))
    ms = jnp.mean(ot * ot, axis=0, keepdims=True)
    ot = ot * lax.rsqrt(ms + NORM_EPS) * (nw_ref[...] * (1.0 - lambda_init))
    o_ref[...] = ot.T.astype(o_ref.dtype)


def diff_attention_core(qkv, lq1, lk1, lq2, lk2, norm_col, *, batch, seq, tq, lambda_init):
    n = qkv.shape[0]
    dh = DIFF_HEAD_DIM
    nq = seq // tq
    k_off = DIFF_HEADS
    v_off = 2 * DIFF_HEADS
    vec = pl.BlockSpec((1, dh), lambda b, h, q: (0, 0))
    return pl.pallas_call(
        functools.partial(_diff_body, lambda_init=lambda_init), grid=(batch, DIFF_HEADS, nq),
        in_specs=[pl.BlockSpec((tq, 2 * dh), lambda b, h, q: (b * nq + q, h)),
                  pl.BlockSpec((seq, 2 * dh), lambda b, h, q: (b, k_off + h)),
                  pl.BlockSpec((seq, 2 * dh), lambda b, h, q: (b, v_off + h)),
                  vec, vec, vec, vec,
                  pl.BlockSpec((2 * dh, 1), lambda b, h, q: (0, 0))],
        out_specs=pl.BlockSpec((tq, 2 * dh), lambda b, h, q: (b * nq + q, h)),
        out_shape=jax.ShapeDtypeStruct((n, DIFF_HEADS * 2 * dh), BF16),
        scratch_shapes=[pltpu.VMEM((2 * dh, tq), BF16), pltpu.VMEM((nq, 2 * dh, tq), BF16),
                        pltpu.VMEM((2, tq, tq), F32), pltpu.VMEM((2, tq, tq), F32), pltpu.VMEM((2, 1, tq), F32), pltpu.VMEM((2, 1, tq), F32), pltpu.VMEM((2, 2 * dh, tq), F32)],
        compiler_params=_cparams("parallel", "parallel", "arbitrary"), name="diff_attention")(
            qkv, qkv, qkv, lq1, lk1, lq2, lk2, norm_col)


def _ffn_body(h_ref, nw_ref, wg_ref, wu_ref, wd_ref, o_ref, xn_ref):
    @pl.when(pl.program_id(1) == 0)
    def _():
        h = h_ref[...]
        xn_ref[...] = _rms(h, nw_ref[...]).astype(BF16)
        o_ref[...] = h

    xn = xn_ref[...]
    gate = jnp.dot(xn, wg_ref[...], preferred_element_type=F32)
    up = jnp.dot(xn, wu_ref[...], preferred_element_type=F32)
    act = (gate * jax.nn.sigmoid(gate) * up).astype(BF16)
    o_ref[...] += jnp.dot(act, wd_ref[...], preferred_element_type=F32)


def ffn_residual(h, nw, w_gu, w_down, *, tm, tf):
    n, d = h.shape
    f = w_down.shape[0]
    nf = f // tf
    return pl.pallas_call(
        _ffn_body, grid=(n // tm, nf),
        in_specs=[pl.BlockSpec((tm, d), lambda i, j: (i, 0)), pl.BlockSpec((1, d), lambda i, j: (0, 0)),
                  pl.BlockSpec((d, tf), lambda i, j: (0, j)), pl.BlockSpec((d, tf), lambda i, j: (0, nf + j)),
                  pl.BlockSpec((tf, d), lambda i, j: (j, 0))],
        out_specs=pl.BlockSpec((tm, d), lambda i, j: (i, 0)),
        out_shape=jax.ShapeDtypeStruct((n, d), F32),
        scratch_shapes=[pltpu.VMEM((tm, d), BF16)],
        compiler_params=_cparams("parallel", "arbitrary"), name="ffn_residual")(h, nw, w_gu, w_gu, w_down)


def _pack_bf16_pairs(x):
    w = x.shape[1] // 2
    lo = pltpu.bitcast(x[:, :w].astype(BF16).astype(F32), jnp.uint32) >> 16
    hi = pltpu.bitcast(x[:, w:].astype(BF16).astype(F32), jnp.uint32)
    return hi | lo


def _unpack_bf16_pairs(u):
    lo = pltpu.bitcast(u << 16, F32).astype(BF16)
    hi = pltpu.bitcast(u & jnp.uint32(0xFFFF0000), F32).astype(BF16)
    return jnp.concatenate([lo, hi], axis=1)


def _router_body(h_ref, nw_ref, r_ref, xn_ref, info_ref, cnt_ref, carry_ref):
    i = pl.program_id(0)

    @pl.when(i == 0)
    def _():
        carry_ref[...] = jnp.zeros_like(carry_ref)

    xn = _rms(h_ref[...], nw_ref[...])
    xn_ref[...] = _pack_bf16_pairs(xn)
    x_hi, x_mid, x_lo = _split3(xn)
    dot = functools.partial(jnp.dot, preferred_element_type=F32)
    r3 = r_ref[...]
    a, b, c = dot(x_hi, r3), dot(x_mid, r3), dot(x_lo, r3)
    e = N_EXPERTS
    logits = ((c + pltpu.roll(b, LANES - e, 1) + pltpu.roll(a, LANES - 2 * e, 1))
              + (b + pltpu.roll(a, LANES - e, 1))) + a
    tm = logits.shape[0]
    lane = lax.broadcasted_iota(jnp.int32, logits.shape, 1)
    lane_f = lane.astype(F32)
    lg = jnp.where(lane < N_EXPERTS, logits, -jnp.inf)
    m1 = jnp.max(lg, axis=-1, keepdims=True)
    e1 = jnp.min(jnp.where(lg == m1, lane_f, float(LANES)), axis=-1, keepdims=True)
    lg2 = jnp.where(lane_f == e1, -jnp.inf, lg)
    m2 = jnp.max(lg2, axis=-1, keepdims=True)
    e2 = jnp.min(jnp.where(lg2 == m2, lane_f, float(LANES)), axis=-1, keepdims=True)
    ex = jnp.exp(m2 - m1)
    g1 = 1.0 / (1.0 + ex)
    g2 = ex / (1.0 + ex)
    hot1 = lane_f == e1
    hot2 = lane_f == e2
    onehot = jnp.where(hot1 | hot2, 1.0, 0.0)
    row = lax.broadcasted_iota(jnp.int32, (tm, tm), 0)
    col = lax.broadcasted_iota(jnp.int32, (tm, tm), 1)
    before = jnp.where(col < row, 1.0, 0.0).astype(BF16)
    cum = dot(before, onehot.astype(BF16)) + carry_ref[...]
    rank1 = jnp.sum(jnp.where(hot1, cum, 0.0), axis=-1, keepdims=True)
    rank2 = jnp.sum(jnp.where(hot2, cum, 0.0), axis=-1, keepdims=True)
    info = jnp.zeros(logits.shape, F32)
    for idx, val in enumerate((e1, e2, rank1, rank2, g1, g2)):
        info = jnp.where(lane == idx, val, info)
    info_ref[...] = info
    carry_ref[...] += jnp.sum(onehot, axis=0, keepdims=True)
    cnt_ref[...] = jnp.broadcast_to(carry_ref[...], cnt_ref.shape)


def moe_router(h, nw, router_pad, *, tm):
    n, d = h.shape
    return pl.pallas_call(
        _router_body, grid=(n // tm,),
        in_specs=[pl.BlockSpec((tm, d), lambda i: (i, 0)), pl.BlockSpec((1, d), lambda i: (0, 0)),
                  pl.BlockSpec((d, LANES), lambda i: (0, 0))],
        out_specs=[pl.BlockSpec((tm, d // 2), lambda i: (i, 0)), pl.BlockSpec((tm, LANES), lambda i: (i, 0)),
                   pl.BlockSpec((8, LANES), lambda i: (0, 0))],
        out_shape=[jax.ShapeDtypeStruct((n, d // 2), jnp.uint32), jax.ShapeDtypeStruct((n, LANES), F32),
                   jax.ShapeDtypeStruct((8, LANES), F32)],
        scratch_shapes=[pltpu.VMEM((1, LANES), F32)],
        compiler_params=_cparams("arbitrary"), name="moe_router")(h, nw, router_pad)


def _moe_ffn_body(te_ref, nv_ref, na_ref, p1_ref, p2_ref, ps_ref, pn_ref, ts_ref, x_hbm, wg_ref, wu_ref, wd_ref, o_ref,
                  src_ref, xg_ref, xb_ref, acc_ref, sem, *, parts, rows_per_step):
    i = pl.program_id(0)
    j = pl.program_id(1)
    n_tiles = pl.num_programs(0)
    n_steps = pl.num_programs(1)
    tm = acc_ref.shape[0]
    part_rows = tm // parts
    rows_staged = xg_ref.shape[1]

    def row_copy(tile, slot, r):
        return pltpu.make_async_copy(x_hbm.at[pl.ds(src_ref[tile * tm + r], 1)], xg_ref.at[slot, pl.ds(r, 1)],
                                     sem.at[slot])

    def row_wait(slot, r):
        pltpu.make_async_copy(x_hbm.at[pl.ds(0, 1)], xg_ref.at[slot, pl.ds(r, 1)], sem.at[slot]).wait()

    def wait_tile(slot):
        lax.fori_loop(0, rows_staged, lambda r, c: (row_wait(slot, r), c)[1], 0, unroll=DMA_LOOP_UNROLL)

    @pl.when((i == 0) & (j == 0))
    def _():
        def clear(r, c):
            src_ref[r] = 0
            return c

        def fill(t, c):
            src_ref[p1_ref[t]] = t
            src_ref[p2_ref[t]] = t
            return c

        for e in range(N_EXPERTS):
            lax.fori_loop(ps_ref[e], ps_ref[e] + pn_ref[e], clear, 0)
        lax.fori_loop(ts_ref[0], src_ref.shape[0], clear, 0)
        lax.fori_loop(0, p1_ref.shape[0], fill, 0, unroll=DMA_LOOP_UNROLL)
        lax.fori_loop(0, rows_staged, lambda r, c: (row_copy(0, 0, r).start(), c)[1], 0, unroll=DMA_LOOP_UNROLL)

    @pl.when(j == 0)
    def _():
        wait_tile(i % 2)
        acc_ref[...] = jnp.zeros_like(acc_ref)
        xb_ref[...] = _unpack_bf16_pairs(xg_ref[i % 2, :tm, :])

    def issue_next_rows():
        for k in range(rows_per_step):
            row_copy(i + 1, (i + 1) % 2, j * rows_per_step + k).start(priority=k % 2)

    def leading_rows(rows):
        issue_next_rows()
        x = xb_ref[:rows, :]
        gate = jnp.dot(x, wg_ref[0].astype(BF16), preferred_element_type=F32)
        up = jnp.dot(x, wu_ref[0].astype(BF16), preferred_element_type=F32)
        act = (gate * jax.nn.sigmoid(gate) * up).astype(BF16)
        acc_ref[:rows, :] += jnp.dot(act, wd_ref[0].astype(BF16), preferred_element_type=F32)

    pl.when(nv_ref[i] == 0)(issue_next_rows)
    for k in range(1, parts + 1):
        lo, hi = (k - 1) * part_rows, k * part_rows
        pl.when((nv_ref[i] > lo) & (nv_ref[i] <= hi))(functools.partial(leading_rows, hi))

    @pl.when(j == n_steps - 1)
    def _():
        o_ref[...] = _pack_bf16_pairs(acc_ref[...])

    @pl.when((i == n_tiles - 1) & (j == n_steps - 1))
    def _():
        wait_tile((i + 1) % 2)


def moe_ffn(x32, pos1, pos2, pad_start, pad_len, tail_start, tile_expert, tile_valid, n_active, w_gu, w_down, *,
            p_rows, tm, tf, parts):
    n, w = x32.shape
    d = 2 * w
    f = w_down.shape[1]
    nf = f // tf
    n_tiles = p_rows // tm
    rows_per_step = -(-tm // nf)

    def frozen(i, j, na):
        return jnp.where(i < na[0], j, nf - 1)

    grid_spec = pltpu.PrefetchScalarGridSpec(
        num_scalar_prefetch=8, grid=(n_tiles, nf),
        in_specs=[pl.BlockSpec(memory_space=pl.ANY),
                  pl.BlockSpec((1, d, tf), lambda i, j, te, nv, na, *_: (te[i], 0, frozen(i, j, na))),
                  pl.BlockSpec((1, d, tf), lambda i, j, te, nv, na, *_: (te[i], 0, nf + frozen(i, j, na))),
                  pl.BlockSpec((1, tf, d), lambda i, j, te, nv, na, *_: (te[i], frozen(i, j, na), 0))],
        out_specs=pl.BlockSpec((tm, w), lambda i, j, te, nv, na, *_: (i, 0)),
        scratch_shapes=[pltpu.SMEM(((n_tiles + 2) * tm,), jnp.int32), pltpu.VMEM((2, rows_per_step * nf, w), x32.dtype),
                        pltpu.VMEM((tm, d), BF16), pltpu.VMEM((tm, d), F32), pltpu.SemaphoreType.DMA((2,))])
    return pl.pallas_call(
        functools.partial(_moe_ffn_body, parts=parts, rows_per_step=rows_per_step), grid_spec=grid_spec,
        out_shape=jax.ShapeDtypeStruct((p_rows, w), jnp.uint32),
        compiler_params=_cparams("arbitrary", "arbitrary"), name="moe_ffn")(
            tile_expert, tile_valid, n_active, pos1, pos2, pad_start, pad_len, tail_start, x32, w_gu, w_gu, w_down)


def _combine_body(p1_ref, p2_ref, h_ref, info_ref, y_ref, o_ref, buf1, buf2, sem):
    tr = h_ref.shape[0]
    base = pl.program_id(0) * tr

    def row_copy(src, buf, r):
        return pltpu.make_async_copy(y_ref.at[pl.ds(src, 1)], buf.at[pl.ds(r, 1)], sem)

    def issue(r, carry):
        row_copy(p1_ref[base + r], buf1, r).start(priority=0)
        row_copy(p2_ref[base + r], buf2, r).start(priority=1)
        return carry

    def drain(r, carry):
        row_copy(p1_ref[base + r], buf1, r).wait()
        row_copy(p2_ref[base + r], buf2, r).wait()
        return carry

    lax.fori_loop(0, tr, issue, 0, unroll=DMA_LOOP_UNROLL)
    lax.fori_loop(0, tr, drain, 0, unroll=DMA_LOOP_UNROLL)
    info = info_ref[...]
    g1 = info[:, 4:5]
    g2 = info[:, 5:6]
    y1 = _unpack_bf16_pairs(buf1[...]).astype(F32)
    y2 = _unpack_bf16_pairs(buf2[...]).astype(F32)
    o_ref[...] = h_ref[...] + (g1 * y1 + g2 * y2)


def moe_combine(h, info, y32, pos1, pos2, *, tr):
    n, d = h.shape
    w = y32.shape[1]
    grid_spec = pltpu.PrefetchScalarGridSpec(
        num_scalar_prefetch=2, grid=(n // tr,),
        in_specs=[pl.BlockSpec((tr, d), lambda i, p1, p2: (i, 0)), pl.BlockSpec((tr, LANES), lambda i, p1, p2: (i, 0)),
                  pl.BlockSpec(memory_space=pl.ANY)],
        out_specs=pl.BlockSpec((tr, d), lambda i, p1, p2: (i, 0)),
        scratch_shapes=[pltpu.VMEM((tr, w), y32.dtype), pltpu.VMEM((tr, w), y32.dtype), pltpu.SemaphoreType.DMA])
    return pl.pallas_call(
        _combine_body, grid_spec=grid_spec, out_shape=jax.ShapeDtypeStruct((n, d), F32),
        compiler_params=_cparams("arbitrary"), name="moe_combine")(pos1, pos2, h, info, y32)


def moe_residual(h, nw, router, w_gu, w_down, *, tm_route, tr, tm, tf):
    n, d = h.shape
    router_terms = jnp.concatenate(_split3(router.astype(F32)), axis=1)
    router_pad = jnp.zeros((d, LANES), BF16).at[:, :3 * N_EXPERTS].set(router_terms)
    x32, info, counts = moe_router(h, nw, router_pad, tm=tm_route)
    cnt = counts[0, :N_EXPERTS].astype(jnp.int32)
    padded = (cnt + tm - 1) // tm * tm
    ends = jnp.cumsum(padded)
    offs = ends - padded
    e1 = info[:, 0].astype(jnp.int32)
    e2 = info[:, 1].astype(jnp.int32)
    pos1 = offs[e1] + info[:, 2].astype(jnp.int32)
    pos2 = offs[e2] + info[:, 3].astype(jnp.int32)
    p_rows = 2 * n + N_EXPERTS * tm
    n_tiles = p_rows // tm
    tile_start = jnp.arange(n_tiles, dtype=jnp.int32) * tm
    tile_expert = jnp.minimum(jnp.sum(tile_start[:, None] >= ends[None, :], axis=1), N_EXPERTS - 1).astype(jnp.int32)
    tile_valid = jnp.clip((offs + cnt)[tile_expert] - tile_start, 0, tm).astype(jnp.int32)
    n_active = (ends[-1:] // tm).astype(jnp.int32)
    y32 = moe_ffn(x32, pos1, pos2, offs + cnt, padded - cnt, ends[-1:], tile_expert, tile_valid, n_active, w_gu, w_down,
                  p_rows=p_rows, tm=tm, tf=tf, parts=MOE_TILE_PARTS)
    return moe_combine(h, info, y32, pos1, pos2, tr=tr)


def _ple_body(h_ref, nw_ref, wg_ref, bg_ref, p_ref, wp_ref, *rest):
    o_ref = rest[-1]
    h = h_ref[...]
    hn = _rms(h, nw_ref[...]).astype(BF16)
    z = jnp.dot(hn, wg_ref[...], preferred_element_type=F32) + bg_ref[...]
    proj = jnp.dot(p_ref[...].astype(BF16), wp_ref[...], preferred_element_type=F32)
    out = h + jax.nn.sigmoid(z) * proj
    if len(rest) == 2:
        out = _rms(out, rest[0][...])
    o_ref[...] = out


def ple_residual(h, nw, w_gate, b_gate, p_i, w_proj, *, tm, final_norm_w=None):
    n, d = h.shape
    pd = p_i.shape[1]
    in_specs = [pl.BlockSpec((tm, d), lambda i: (i, 0)), _resident((1, d)), _resident((d, d)), _resident((1, d)),
                pl.BlockSpec((tm, pd), lambda i: (i, 0)), _resident((pd, d))]
    args = [h, nw, w_gate, b_gate, p_i, w_proj]
    if final_norm_w is not None:
        in_specs.append(_resident((1, d)))
        args.append(final_norm_w)
    return pl.pallas_call(
        _ple_body, grid=(n // tm,), in_specs=in_specs,
        out_specs=pl.BlockSpec((tm, d), lambda i: (i, 0)),
        out_shape=jax.ShapeDtypeStruct((n, d), F32),
        compiler_params=_cparams("parallel"), name="ple_residual")(*args)


def _row(v):
    return v.reshape(1, -1).astype(F32)


def _tile(n, want):
    return min(n, want)


def _gla_layer(h, norm_mix, w_in, w_a2, b_a2, gla_norm, w_o, *, batch, seq):
    n = h.shape[0]
    main = w_in.shape[1] - GLA_GATE_RANK
    w_main = w_in[:, :main].astype(BF16)
    w_a = jnp.zeros((w_in.shape[0], LANES), BF16).at[:, :GLA_GATE_RANK].set(w_in[:, main:].astype(BF16))
    w_a2_pad = jnp.zeros((LANES, w_a2.shape[1]), BF16).at[:GLA_GATE_RANK].set(w_a2.astype(BF16))
    qkvg, a = norm_matmul_extra(h, _row(norm_mix), w_main, w_a, tm=_tile(n, 256))
    o = gla_core(qkvg, a, w_a2_pad, _row(b_a2), _row(gla_norm), batch=batch, seq=seq, tb=_tile(seq, 1024))
    return matmul_residual(o, w_o.astype(BF16), h, tm=_tile(n, 512))


def _diff_layer(h, positions, norm_mix, w_in, lq1, lk1, lq2, lk2, diff_norm, w_o, *, batch, seq, lambda_init):
    n = h.shape[0]
    tabs = rope_tables(positions, tm=_tile(n, 1024))
    qk_width = 2 * DIFF_HEADS * DIFF_HEAD_DIM
    qkv = norm_matmul_rope(h, _row(norm_mix), w_in.astype(BF16), tabs, tm=_tile(n, 256),
                           q_width=qk_width, k_width=qk_width, q_scale=DIFF_HEAD_DIM ** -0.5 * math.log2(math.e))
    o = diff_attention_core(qkv, _row(lq1), _row(lk1), _row(lq2), _row(lk2), diff_norm.reshape(-1, 1).astype(F32),
                            batch=batch, seq=seq, tq=_tile(seq, 512), lambda_init=lambda_init)
    return matmul_residual(o, w_o.astype(BF16), h, tm=_tile(n, 512))


def _diff_lambda_init(layer_idx):
    return 0.8 - 0.6 * math.exp(-0.3 * layer_idx)


def kernel(x, p, positions, l0_norm_mix, l0_gla_w_in, l0_gla_w_a2, l0_gla_b_a2, l0_gla_norm, l0_gla_w_o, l0_norm_ffn, l0_ffn_w_gu, l0_ffn_w_down, l0_ple_norm, l0_ple_w_gate, l0_ple_b_gate, l0_ple_w_proj, l1_norm_mix, l1_pool_w, l1_pool_scale, l1_norm_ffn, l1_moe_router, l1_moe_w_gu, l1_moe_w_down, l1_ple_norm, l1_ple_w_gate, l1_ple_b_gate, l1_ple_w_proj, l2_norm_mix, l2_diff_w_in, l2_diff_lq1, l2_diff_lk1, l2_diff_lq2, l2_diff_lk2, l2_diff_norm, l2_diff_w_o, l2_norm_ffn, l2_ffn_w_gu, l2_ffn_w_down, l2_ple_norm, l2_ple_w_gate, l2_ple_b_gate, l2_ple_w_proj, l3_norm_mix, l3_gla_w_in, l3_gla_w_a2, l3_gla_b_a2, l3_gla_norm, l3_gla_w_o, l3_norm_ffn, l3_moe_router, l3_moe_w_gu, l3_moe_w_down, l3_ple_norm, l3_ple_w_gate, l3_ple_b_gate, l3_ple_w_proj, final_norm):
    batch, seq, d = x.shape
    n = batch * seq
    h = x.reshape(n, d)
    pf = p.reshape(p.shape[0], n, p.shape[-1])
    tm = _tile(n, 512)

    def ffn(h, nw, w_gu, w_down):
        return ffn_residual(h, _row(nw), w_gu.astype(BF16), w_down.astype(BF16), tm=tm, tf=512)

    def moe(h, nw, router, w_gu, w_down):
        return moe_residual(h, _row(nw), router, w_gu, w_down, tm_route=tm, tr=_tile(n, 256), tm=_tile(n, 1024), tf=256)

    def ple(h, i, nw, w_gate, b_gate, w_proj, final_norm_w=None):
        return ple_residual(h, _row(nw), w_gate.astype(BF16), _row(b_gate), pf[i], w_proj.astype(BF16), tm=tm,
                            final_norm_w=final_norm_w)

    h = _gla_layer(h, l0_norm_mix, l0_gla_w_in, l0_gla_w_a2, l0_gla_b_a2, l0_gla_norm, l0_gla_w_o, batch=batch, seq=seq)
    h = ffn(h, l0_norm_ffn, l0_ffn_w_gu, l0_ffn_w_down)
    h = ple(h, 0, l0_ple_norm, l0_ple_w_gate, l0_ple_b_gate, l0_ple_w_proj)

    h = pool_mixer(h, _row(l1_norm_mix), l1_pool_w.astype(BF16), _row(l1_pool_scale), seq=seq, tt=_tile(seq, 512))
    h = moe(h, l1_norm_ffn, l1_moe_router, l1_moe_w_gu, l1_moe_w_down)
    h = ple(h, 1, l1_ple_norm, l1_ple_w_gate, l1_ple_b_gate, l1_ple_w_proj)

    h = _diff_layer(h, positions, l2_norm_mix, l2_diff_w_in, l2_diff_lq1, l2_diff_lk1, l2_diff_lq2, l2_diff_lk2,
                    l2_diff_norm, l2_diff_w_o, batch=batch, seq=seq, lambda_init=_diff_lambda_init(2))
    h = ffn(h, l2_norm_ffn, l2_ffn_w_gu, l2_ffn_w_down)
    h = ple(h, 2, l2_ple_norm, l2_ple_w_gate, l2_ple_b_gate, l2_ple_w_proj)

    h = _gla_layer(h, l3_norm_mix, l3_gla_w_in, l3_gla_w_a2, l3_gla_b_a2, l3_gla_norm, l3_gla_w_o, batch=batch, seq=seq)
    h = moe(h, l3_norm_ffn, l3_moe_router, l3_moe_w_gu, l3_moe_w_down)
    h = ple(h, 3, l3_ple_norm, l3_ple_w_gate, l3_ple_b_gate, l3_ple_w_proj, final_norm_w=_row(final_norm))
    return h.reshape(batch, seq, d)
```

```python
import functools
import math

import jax
import jax.numpy as jnp
from jax import lax
from jax.experimental import pallas as pl
from jax.experimental.pallas import tpu as pltpu

F32 = jnp.float32
BF16 = jnp.bfloat16

NORM_EPS = 1e-6
LANES = 128
V7X_VMEM_BYTES = 64 * 1024 * 1024
VMEM_LIMIT = V7X_VMEM_BYTES * 7 // 8

GLA_HEADS = 4
GLA_GATE_RANK = 16
GLA_GATE_NORMALIZER = 16.0
GLA_CHUNK = 64
GLA_BLOCK = 256
GLA_HEADS_PER_STEP = 4
POOL_WINDOWS = (2, 4, 8, 16)
POOL_HALO = 16
DIFF_HEADS = 8
DIFF_HEAD_DIM = 128
ROPE_THETA = 500000.0
ROPE_DIMS = 32
N_EXPERTS = 8
MOE_TILE_PARTS = 2
DMA_LOOP_UNROLL = 8

NT_DIMS = (((1,), (1,)), ((), ()))
TN_DIMS = (((0,), (0,)), ((), ()))


def _cparams(*sem):
    return pltpu.CompilerParams(dimension_semantics=sem, vmem_limit_bytes=VMEM_LIMIT)


def _rms(x, w):
    ms = jnp.mean(x * x, axis=-1, keepdims=True)
    return x * lax.rsqrt(ms + NORM_EPS) * w


def _split3(x):
    hi = x.astype(BF16)
    r1 = x - hi.astype(F32)
    mid = r1.astype(BF16)
    lo = (r1 - mid.astype(F32)).astype(BF16)
    return hi, mid, lo


def _resident(shape):
    return pl.BlockSpec(shape, lambda i: (0,) * len(shape), pipeline_mode=pl.Buffered(1))


def _norm_mm_extra_body(h_ref, nw_ref, w_ref, wx_ref, o_ref, ox_ref):
    xn = _rms(h_ref[...], nw_ref[...]).astype(BF16)
    ox_ref[...] = jnp.dot(xn, wx_ref[...], preferred_element_type=F32)
    o_ref[...] = jnp.dot(xn, w_ref[...], preferred_element_type=F32).astype(o_ref.dtype)


def norm_matmul_extra(h, nw, w, w_extra, *, tm):
    n, d = h.shape
    nout = w.shape[1]
    nx = w_extra.shape[1]
    return pl.pallas_call(
        _norm_mm_extra_body, grid=(n // tm,),
        in_specs=[pl.BlockSpec((tm, d), lambda i: (i, 0)), _resident((1, d)), _resident((d, nout)), _resident((d, nx))],
        out_specs=[pl.BlockSpec((tm, nout), lambda i: (i, 0)), pl.BlockSpec((tm, nx), lambda i: (i, 0))],
        out_shape=[jax.ShapeDtypeStruct((n, nout), BF16), jax.ShapeDtypeStruct((n, nx), F32)],
        compiler_params=_cparams("parallel"), name="norm_matmul_extra")(h, nw, w, w_extra)


def _rope_tab_body(pos_ref, invf_ref, cos_ref, sina_ref, sinb_ref):
    ang = pos_ref[...].astype(F32) * invf_ref[...]
    lane = lax.broadcasted_iota(jnp.int32, ang.shape, 1)
    half = ROPE_DIMS // 2
    s = jnp.sin(ang)
    cos_ref[...] = jnp.cos(ang)
    sina_ref[...] = jnp.where((lane >= half) & (lane < ROPE_DIMS), s, 0.0)
    sinb_ref[...] = jnp.where(lane < half, -s, 0.0)


def rope_tables(positions, *, tm):
    n = positions.size
    half = ROPE_DIMS // 2
    inv_freq = ROPE_THETA ** (-jnp.arange(half, dtype=F32) * 2.0 / ROPE_DIMS)
    invf = jnp.zeros((1, LANES), F32).at[0, :half].set(inv_freq).at[0, half:ROPE_DIMS].set(inv_freq)
    pos = positions.reshape(n, 1)
    tab = jax.ShapeDtypeStruct((n, LANES), F32)
    t_spec = pl.BlockSpec((tm, LANES), lambda i: (i, 0))
    return pl.pallas_call(
        _rope_tab_body, grid=(n // tm,),
        in_specs=[pl.BlockSpec((tm, 1), lambda i: (i, 0)), pl.BlockSpec((1, LANES), lambda i: (0, 0))],
        out_specs=[t_spec, t_spec, t_spec], out_shape=[tab, tab, tab],
        compiler_params=_cparams("parallel"), name="rope_tables")(pos, invf)


def _norm_mm_rope_body(h_ref, nw_ref, w_ref, cos_ref, sina_ref, sinb_ref, o_ref, *, q_heads, k_heads, q_scale):
    xn = _rms(h_ref[...], nw_ref[...]).astype(BF16)
    acc = jnp.dot(xn, w_ref[...], preferred_element_type=F32)
    half = ROPE_DIMS // 2
    c, sa, sb = cos_ref[...], sina_ref[...], sinb_ref[...]
    cq, saq, sbq = c * q_scale, sa * q_scale, sb * q_scale
    for t in range(q_heads + k_heads):
        xj = acc[:, t * LANES:(t + 1) * LANES]
        tc, tsa, tsb = (cq, saq, sbq) if t < q_heads else (c, sa, sb)
        r = xj * tc + pltpu.roll(xj, half, 1) * tsa + pltpu.roll(xj, LANES - half, 1) * tsb
        o_ref[:, t * LANES:(t + 1) * LANES] = r.astype(o_ref.dtype)
    rest = (q_heads + k_heads) * LANES
    o_ref[:, rest:] = acc[:, rest:].astype(o_ref.dtype)


def norm_matmul_rope(h, nw, w, tabs, *, tm, q_width, k_width, q_scale):
    n, d = h.shape
    nout = w.shape[1]
    body = functools.partial(_norm_mm_rope_body, q_heads=q_width // LANES, k_heads=k_width // LANES, q_scale=q_scale)
    t_spec = pl.BlockSpec((tm, LANES), lambda i: (i, 0))
    return pl.pallas_call(
        body, grid=(n // tm,),
        in_specs=[pl.BlockSpec((tm, d), lambda i: (i, 0)), _resident((1, d)), _resident((d, nout)),
                  t_spec, t_spec, t_spec],
        out_specs=pl.BlockSpec((tm, nout), lambda i: (i, 0)),
        out_shape=jax.ShapeDtypeStruct((n, nout), BF16),
        compiler_params=_cparams("parallel"), name="norm_matmul_rope")(h, nw, w, *tabs)


def _mm_res_body(a_ref, w_ref, h_ref, o_ref):
    o_ref[...] = h_ref[...] + jnp.dot(a_ref[...], w_ref[...], preferred_element_type=F32)


def matmul_residual(a, w, h, *, tm):
    n, k = a.shape
    d = w.shape[1]
    return pl.pallas_call(
        _mm_res_body, grid=(n // tm,),
        in_specs=[pl.BlockSpec((tm, k), lambda i: (i, 0)), _resident((k, d)), pl.BlockSpec((tm, d), lambda i: (i, 0))],
        out_specs=pl.BlockSpec((tm, d), lambda i: (i, 0)),
        out_shape=jax.ShapeDtypeStruct((n, d), F32),
        compiler_params=_cparams("parallel"), name="matmul_residual")(a, w, h)


def _gla_body(q_ref, k_ref, v_ref, g_ref, a_ref, wa_ref, ba_ref, nw_ref, o_ref, st_ref, gk_ref, *, heads, block, sub,
              q_scale):
    dk = q_ref.shape[1] // heads
    dv = v_ref.shape[1] // heads

    @pl.when(pl.program_id(2) == 0)
    def _():
        st_ref[...] = jnp.zeros_like(st_ref)

    z = jnp.dot(a_ref[...].astype(BF16), wa_ref[...], preferred_element_type=F32) + ba_ref[...]
    gk_ref[...] = (jnp.minimum(z, 0.0) - jnp.log1p(jnp.exp(-jnp.abs(z)))) * (1.0 / GLA_GATE_NORMALIZER)

    row = lax.broadcasted_iota(jnp.int32, (block, block), 0)
    col = lax.broadcasted_iota(jnp.int32, (block, block), 1)
    tri = (col <= row).astype(BF16)
    nw = nw_ref[...]

    def head_block(r, hh):
        kc = slice(hh * dk, (hh + 1) * dk)
        vc = slice(hh * dv, (hh + 1) * dv)
        g_hi, g_mid, g_lo = _split3(gk_ref[r, kc])
        b = (jnp.dot(tri, g_hi, preferred_element_type=F32) + jnp.dot(tri, g_mid, preferred_element_type=F32)
             + jnp.dot(tri, g_lo, preferred_element_type=F32))
        b_last = b[block - 1:block, :]
        q = q_ref[r, kc].astype(F32) * q_scale
        k = k_ref[r, kc].astype(F32)
        v = v_ref[r, vc]
        st = st_ref[hh]
        q_in = (q * jnp.exp(b)).astype(BF16)
        k_out = (k * jnp.exp(b_last - b)).astype(BF16)
        o_inter = lax.dot_general(q_in, st.astype(BF16), NT_DIMS, preferred_element_type=F32)
        st_ref[hh] = st * jnp.exp(b_last) + lax.dot_general(v, k_out, TN_DIMS, preferred_element_type=F32)
        o_intra = []
        for i in range(block // sub):
            lo, hi = i * sub, (i + 1) * sub
            base = b[lo - 1:lo, :] if i else jnp.zeros_like(b_last)
            q_rel = (q[lo:hi] * jnp.exp(b[lo:hi] - base)).astype(BF16)
            k_rel = (k[:hi] * jnp.exp(base - b[:hi])).astype(BF16)
            s = lax.dot_general(q_rel, k_rel, NT_DIMS, preferred_element_type=F32)
            key = lax.broadcasted_iota(jnp.int32, (sub, hi), 1)
            qry = lax.broadcasted_iota(jnp.int32, (sub, hi), 0) + lo
            s = jnp.where(key <= qry, s, 0.0).astype(BF16)
            o_intra.append(jnp.dot(s, v[:hi], preferred_element_type=F32))
        o = o_inter + jnp.concatenate(o_intra, axis=0)
        gate = g_ref[r, vc].astype(F32)
        o_ref[r, vc] = (_rms(o, nw) * (gate * jax.nn.sigmoid(gate))).astype(o_ref.dtype)

    def step(c, carry):
        r = pl.ds(pl.multiple_of(c * block, block), block)
        for hh in range(heads):
            head_block(r, hh)
        return carry

    lax.fori_loop(0, q_ref.shape[0] // block, step, 0)


def gla_core(qkvg, a, w_a2, b_a2, norm_w, *, batch, seq, tb, heads_per_step):
    n = qkvg.shape[0]
    kd_total = w_a2.shape[1]
    dk = kd_total // GLA_HEADS
    vd_total = (qkvg.shape[1] - 2 * kd_total) // 2
    dv = vd_total // GLA_HEADS
    nt = seq // tb
    wk, wv = heads_per_step * dk, heads_per_step * dv
    kq = kd_total // wk
    kv = 2 * kd_total // wv
    kg = kv + vd_total // wv
    body = functools.partial(_gla_body, heads=heads_per_step, block=min(tb, GLA_BLOCK), sub=GLA_CHUNK, q_scale=dk ** -0.5)
    rows = lambda b, h, t: b * nt + t
    return pl.pallas_call(
        body, grid=(batch, GLA_HEADS // heads_per_step, nt),
        in_specs=[pl.BlockSpec((tb, wk), lambda b, h, t: (rows(b, h, t), h)),
                  pl.BlockSpec((tb, wk), lambda b, h, t: (rows(b, h, t), kq + h)),
                  pl.BlockSpec((tb, wv), lambda b, h, t: (rows(b, h, t), kv + h)),
                  pl.BlockSpec((tb, wv), lambda b, h, t: (rows(b, h, t), kg + h)),
                  pl.BlockSpec((tb, LANES), lambda b, h, t: (rows(b, h, t), 0)),
                  pl.BlockSpec((LANES, wk), lambda b, h, t: (0, h)),
                  pl.BlockSpec((1, wk), lambda b, h, t: (0, h)),
                  pl.BlockSpec((1, dv), lambda b, h, t: (0, 0))],
        out_specs=pl.BlockSpec((tb, wv), lambda b, h, t: (rows(b, h, t), h)),
        out_shape=jax.ShapeDtypeStruct((n, vd_total), BF16),
        scratch_shapes=[pltpu.VMEM((heads_per_step, dv, dk), F32), pltpu.VMEM((tb, wk), F32)],
        compiler_params=_cparams("parallel", "parallel", "arbitrary"), name="gla_core")(
            qkvg, qkvg, qkvg, qkvg, a, w_a2, b_a2, norm_w)


def _pool_body(h_ref, hp_ref, nw_ref, w_ref, sc_ref, o_ref, x_ref, *, tiles_per_seq, group):
    i = pl.program_id(0)
    tt = h_ref.shape[0]
    h = h_ref[...]
    nw = nw_ref[...]
    first = (i % tiles_per_seq) == 0
    x_ref[0:POOL_HALO, :] = jnp.where(first, 0.0, _rms(hp_ref[...], nw))
    x_ref[POOL_HALO:, :] = _rms(h, nw)
    pos = (i % tiles_per_seq) * tt + lax.broadcasted_iota(jnp.int32, (tt, 1), 0)
    for gi, win in enumerate(POOL_WINDOWS):
        cols = slice(gi * group, (gi + 1) * group)
        acc = x_ref[POOL_HALO:POOL_HALO + tt, cols]
        cur = acc
        for back in range(1, win):
            acc = acc + x_ref[POOL_HALO - back:POOL_HALO - back + tt, cols]
        count = jnp.minimum(pos + 1, win).astype(F32)
        pooled = (acc / count - cur).astype(BF16)
        y = jnp.dot(pooled, w_ref[gi], preferred_element_type=F32)
        o_ref[:, cols] = h[:, cols] + y * sc_ref[:, cols]


def pool_mixer(h, nw, w_groups, scale, *, seq, tt):
    n, d = h.shape
    ng, group, _ = w_groups.shape
    tiles_per_seq = seq // tt
    halo_blocks = tt // POOL_HALO
    body = functools.partial(_pool_body, tiles_per_seq=tiles_per_seq, group=group)
    return pl.pallas_call(
        body, grid=(n // tt,),
        in_specs=[pl.BlockSpec((tt, d), lambda i: (i, 0)),
                  pl.BlockSpec((POOL_HALO, d), lambda i: (jnp.maximum(i * halo_blocks - 1, 0), 0)),
                  pl.BlockSpec((1, d), lambda i: (0, 0)),
                  pl.BlockSpec((ng, group, group), lambda i: (0, 0, 0)),
                  pl.BlockSpec((1, d), lambda i: (0, 0))],
        out_specs=pl.BlockSpec((tt, d), lambda i: (i, 0)),
        out_shape=jax.ShapeDtypeStruct((n, d), F32),
        scratch_shapes=[pltpu.VMEM((tt + POOL_HALO, d), F32)],
        compiler_params=_cparams("parallel"), name="pool_mixer")(h, h, nw, w_groups, scale)


def _diff_body(q_ref, k_ref, v_ref, lq1_ref, lk1_ref, lq2_ref, lk2_ref, nw_ref, o_ref,
               qt_ref, vt_ref, sa_ref, sb_ref, m_ref, l_ref, acc_ref, *, lambda_init):
    qi = pl.program_id(2)
    tq = q_ref.shape[0]
    dh = DIFF_HEAD_DIM
    nk = vt_ref.shape[0]

    @pl.when(qi == 0)
    def _():
        def transpose_v(j, carry):
            r = pl.ds(pl.multiple_of(j * tq, tq), tq)
            vt_ref[j] = v_ref[r, :].astype(F32).T.astype(BF16)
            return carry

        lax.fori_loop(0, nk, transpose_v, 0)

    qt_ref[...] = q_ref[...].astype(F32).T.astype(BF16)
    m_ref[...] = jnp.full_like(m_ref, -jnp.inf)
    l_ref[...] = jnp.zeros_like(l_ref)
    acc_ref[...] = jnp.zeros_like(acc_ref)

    def scores(j, s_ref):
        k = k_ref[pl.ds(pl.multiple_of(j * tq, tq), tq), :]
        for i in range(2):
            s_ref[i] = jnp.dot(k[:, i * dh:(i + 1) * dh], qt_ref[i * dh:(i + 1) * dh, :],
                               preferred_element_type=F32)

    def consume(j, s_ref, masked):
        vt = vt_ref[j]
        if masked:
            key = lax.broadcasted_iota(jnp.int32, (tq, tq), 0)
            qry = lax.broadcasted_iota(jnp.int32, (tq, tq), 1)
            keep = key <= qry
        for i in range(2):
            s = s_ref[i]
            if masked:
                s = jnp.where(keep, s, -jnp.inf)
            m_old = m_ref[i]
            m_new = jnp.maximum(m_old, jnp.max(s, axis=0, keepdims=True))
            alpha = jnp.exp2(m_old - m_new)
            p = jnp.exp2(s - m_new)
            l_ref[i] = alpha * l_ref[i] + jnp.sum(p, axis=0, keepdims=True)
            acc_ref[i] = alpha * acc_ref[i] + jnp.dot(vt, p.astype(BF16), preferred_element_type=F32)
            m_ref[i] = m_new

    def pipelined_pair(t, carry):
        j = 2 * t
        scores(j + 1, sb_ref)
        consume(j, sa_ref, False)
        scores(j + 2, sa_ref)
        consume(j + 1, sb_ref, False)
        return carry

    scores(0, sa_ref)
    lax.fori_loop(0, qi // 2, pipelined_pair, 0)

    @pl.when(qi % 2 == 0)
    def _():
        consume(qi, sa_ref, True)

    @pl.when(qi % 2 == 1)
    def _():
        scores(qi, sb_ref)
        consume(qi - 1, sa_ref, False)
        consume(qi, sb_ref, True)

    lam = (jnp.exp(jnp.sum(lq1_ref[...] * lk1_ref[...], axis=-1, keepdims=True))
           - jnp.exp(jnp.sum(lq2_ref[...] * lk2_ref[...], axis=-1, keepdims=True)) + lambda_init)
    ot = acc_ref[0] * (1.0 / l_ref[0]) - lam * (acc_ref[1] * (1.0 / l_ref[1]))
    ms = jnp.mean(ot * ot, axis=0, keepdims=True)
    ot = ot * lax.rsqrt(ms + NORM_EPS) * (nw_ref[...] * (1.0 - lambda_init))
    o_ref[...] = ot.T.astype(o_ref.dtype)


def diff_attention_core(qkv, lq1, lk1, lq2, lk2, norm_col, *, batch, seq, tq, lambda_init):
    n = qkv.shape[0]
    dh = DIFF_HEAD_DIM
    nq = seq // tq
    k_off = DIFF_HEADS
    v_off = 2 * DIFF_HEADS
    vec = pl.BlockSpec((1, dh), lambda b, h, q: (0, 0))
    return pl.pallas_call(
        functools.partial(_diff_body, lambda_init=lambda_init), grid=(batch, DIFF_HEADS, nq),
        in_specs=[pl.BlockSpec((tq, 2 * dh), lambda b, h, q: (b * nq + q, h)),
                  pl.BlockSpec((seq, 2 * dh), lambda b, h, q: (b, k_off + h)),
                  pl.BlockSpec((seq, 2 * dh), lambda b, h, q: (b, v_off + h)),
                  vec, vec, vec, vec,
                  pl.BlockSpec((2 * dh, 1), lambda b, h, q: (0, 0))],
        out_specs=pl.BlockSpec((tq, 2 * dh), lambda b, h, q: (b * nq + q, h)),
        out_shape=jax.ShapeDtypeStruct((n, DIFF_HEADS * 2 * dh), BF16),
        scratch_shapes=[pltpu.VMEM((2 * dh, tq), BF16), pltpu.VMEM((nq, 2 * dh, tq), BF16),
                        pltpu.VMEM((2, tq, tq), F32), pltpu.VMEM((2, tq, tq), F32), pltpu.VMEM((2, 1, tq), F32),
                        pltpu.VMEM((2, 1, tq), F32), pltpu.VMEM((2, 2 * dh, tq), F32)],
        compiler_params=_cparams("parallel", "parallel", "arbitrary"), name="diff_attention")(
            qkv, qkv, qkv, lq1, lk1, lq2, lk2, norm_col)


def _ffn_body(h_ref, nw_ref, wg_ref, wu_ref, wd_ref, o_ref, xn_ref):
    @pl.when(pl.program_id(1) == 0)
    def _():
        h = h_ref[...]
        xn_ref[...] = _rms(h, nw_ref[...]).astype(BF16)
        o_ref[...] = h

    xn = xn_ref[...]
    gate = jnp.dot(xn, wg_ref[...], preferred_element_type=F32)
    up = jnp.dot(xn, wu_ref[...], preferred_element_type=F32)
    act = (gate * jax.nn.sigmoid(gate) * up).astype(BF16)
    o_ref[...] += jnp.dot(act, wd_ref[...], preferred_element_type=F32)


def ffn_residual(h, nw, w_gu, w_down, *, tm, tf):
    n, d = h.shape
    f = w_down.shape[0]
    nf = f // tf
    return pl.pallas_call(
        _ffn_body, grid=(n // tm, nf),
        in_specs=[pl.BlockSpec((tm, d), lambda i, j: (i, 0)), pl.BlockSpec((1, d), lambda i, j: (0, 0)),
                  pl.BlockSpec((d, tf), lambda i, j: (0, j)), pl.BlockSpec((d, tf), lambda i, j: (0, nf + j)),
                  pl.BlockSpec((tf, d), lambda i, j: (j, 0))],
        out_specs=pl.BlockSpec((tm, d), lambda i, j: (i, 0)),
        out_shape=jax.ShapeDtypeStruct((n, d), F32),
        scratch_shapes=[pltpu.VMEM((tm, d), BF16)],
        compiler_params=_cparams("parallel", "arbitrary"), name="ffn_residual")(h, nw, w_gu, w_gu, w_down)


def _pack_bf16_pairs(x):
    w = x.shape[1] // 2
    lo = pltpu.bitcast(x[:, :w].astype(BF16).astype(F32), jnp.uint32) >> 16
    hi = pltpu.bitcast(x[:, w:].astype(BF16).astype(F32), jnp.uint32)
    return hi | lo


def _unpack_bf16_pairs(u):
    lo = pltpu.bitcast(u << 16, F32).astype(BF16)
    hi = pltpu.bitcast(u & jnp.uint32(0xFFFF0000), F32).astype(BF16)
    return jnp.concatenate([lo, hi], axis=1)


def _router_body(h_ref, nw_ref, r_ref, xn_ref, info_ref, cnt_ref, carry_ref):
    i = pl.program_id(0)

    @pl.when(i == 0)
    def _():
        carry_ref[...] = jnp.zeros_like(carry_ref)

    xn = _rms(h_ref[...], nw_ref[...])
    xn_ref[...] = _pack_bf16_pairs(xn)
    x_hi, x_mid, x_lo = _split3(xn)
    dot = functools.partial(jnp.dot, preferred_element_type=F32)
    r3 = r_ref[...]
    a, b, c = dot(x_hi, r3), dot(x_mid, r3), dot(x_lo, r3)
    e = N_EXPERTS
    logits = ((c + pltpu.roll(b, LANES - e, 1) + pltpu.roll(a, LANES - 2 * e, 1))
              + (b + pltpu.roll(a, LANES - e, 1))) + a
    tm = logits.shape[0]
    lane = lax.broadcasted_iota(jnp.int32, logits.shape, 1)
    lane_f = lane.astype(F32)
    lg = jnp.where(lane < N_EXPERTS, logits, -jnp.inf)
    m1 = jnp.max(lg, axis=-1, keepdims=True)
    e1 = jnp.min(jnp.where(lg == m1, lane_f, float(LANES)), axis=-1, keepdims=True)
    lg2 = jnp.where(lane_f == e1, -jnp.inf, lg)
    m2 = jnp.max(lg2, axis=-1, keepdims=True)
    e2 = jnp.min(jnp.where(lg2 == m2, lane_f, float(LANES)), axis=-1, keepdims=True)
    ex = jnp.exp(m2 - m1)
    g1 = 1.0 / (1.0 + ex)
    g2 = ex / (1.0 + ex)
    hot1 = lane_f == e1
    hot2 = lane_f == e2
    onehot = jnp.where(hot1 | hot2, 1.0, 0.0)
    row = lax.broadcasted_iota(jnp.int32, (tm, tm), 0)
    col = lax.broadcasted_iota(jnp.int32, (tm, tm), 1)
    before = jnp.where(col < row, 1.0, 0.0).astype(BF16)
    cum = dot(before, onehot.astype(BF16)) + carry_ref[...]
    rank1 = jnp.sum(jnp.where(hot1, cum, 0.0), axis=-1, keepdims=True)
    rank2 = jnp.sum(jnp.where(hot2, cum, 0.0), axis=-1, keepdims=True)
    info = jnp.zeros(logits.shape, F32)
    for idx, val in enumerate((e1, e2, rank1, rank2, g1, g2)):
        info = jnp.where(lane == idx, val, info)
    info_ref[...] = info
    carry_ref[...] += jnp.sum(onehot, axis=0, keepdims=True)
    cnt_ref[...] = jnp.broadcast_to(carry_ref[...], cnt_ref.shape)


def moe_router(h, nw, router_pad, *, tm):
    n, d = h.shape
    return pl.pallas_call(
        _router_body, grid=(n // tm,),
        in_specs=[pl.BlockSpec((tm, d), lambda i: (i, 0)), pl.BlockSpec((1, d), lambda i: (0, 0)),
                  pl.BlockSpec((d, LANES), lambda i: (0, 0))],
        out_specs=[pl.BlockSpec((tm, d // 2), lambda i: (i, 0)), pl.BlockSpec((tm, LANES), lambda i: (i, 0)),
                   pl.BlockSpec((8, LANES), lambda i: (0, 0))],
        out_shape=[jax.ShapeDtypeStruct((n, d // 2), jnp.uint32), jax.ShapeDtypeStruct((n, LANES), F32),
                   jax.ShapeDtypeStruct((8, LANES), F32)],
        scratch_shapes=[pltpu.VMEM((1, LANES), F32)],
        compiler_params=_cparams("arbitrary"), name="moe_router")(h, nw, router_pad)


def _dispatch_body(p1_ref, p2_ref, ps_ref, pn_ref, ts_ref, tn_ref, x_ref, xs_ref, zero_ref, sem, zsem):
    tr = x_ref.shape[0]
    zr = zero_ref.shape[0]
    base = pl.program_id(0) * tr

    @pl.when(pl.program_id(0) == 0)
    def _():
        zero_ref[...] = jnp.zeros_like(zero_ref)

        def pad_copy(dst):
            return pltpu.make_async_copy(zero_ref.at[pl.ds(0, 1)], xs_ref.at[pl.ds(dst, 1)], zsem)

        def tail_copy(c):
            return pltpu.make_async_copy(zero_ref, xs_ref.at[pl.ds(pl.multiple_of(ts_ref[0] + c * zr, zr), zr)], zsem)

        for e in range(N_EXPERTS):
            lax.fori_loop(0, pn_ref[e], lambda r, c, e=e: (pad_copy(ps_ref[e] + r).start(), c)[1], 0)
        lax.fori_loop(0, tn_ref[0], lambda c, carry: (tail_copy(c).start(), carry)[1], 0)
        for e in range(N_EXPERTS):
            lax.fori_loop(0, pn_ref[e], lambda r, c, e=e: (pad_copy(ps_ref[e] + r).wait(), c)[1], 0)
        lax.fori_loop(0, tn_ref[0], lambda c, carry: (tail_copy(c).wait(), carry)[1], 0)

    def row_copy(r, dst):
        return pltpu.make_async_copy(x_ref.at[pl.ds(r, 1)], xs_ref.at[pl.ds(dst, 1)], sem)

    def issue(r, carry):
        row_copy(r, p1_ref[base + r]).start(priority=0)
        row_copy(r, p2_ref[base + r]).start(priority=1)
        return carry

    def drain(r, carry):
        row_copy(r, p1_ref[base + r]).wait()
        row_copy(r, p2_ref[base + r]).wait()
        return carry

    lax.fori_loop(0, tr, issue, 0, unroll=DMA_LOOP_UNROLL)
    lax.fori_loop(0, tr, drain, 0, unroll=DMA_LOOP_UNROLL)


def moe_dispatch(x32, pos1, pos2, pad_start, pad_len, tail_start, tail_chunks, *, p_rows, tr, zr):
    n, w = x32.shape
    grid_spec = pltpu.PrefetchScalarGridSpec(
        num_scalar_prefetch=6, grid=(n // tr,),
        in_specs=[pl.BlockSpec((tr, w), lambda i, *_: (i, 0))],
        out_specs=pl.BlockSpec(memory_space=pl.ANY),
        scratch_shapes=[pltpu.VMEM((zr, w), x32.dtype), pltpu.SemaphoreType.DMA, pltpu.SemaphoreType.DMA])
    return pl.pallas_call(
        _dispatch_body, grid_spec=grid_spec, out_shape=jax.ShapeDtypeStruct((p_rows, w), x32.dtype),
        compiler_params=_cparams("arbitrary"), name="moe_dispatch")(
            pos1, pos2, pad_start, pad_len, tail_start, tail_chunks, x32)


def _moe_ffn_body(te_ref, nv_ref, na_ref, x_ref, wg_ref, wu_ref, wd_ref, o_ref, xb_ref, acc_ref, *, parts):
    i = pl.program_id(0)
    j = pl.program_id(1)
    part_rows = x_ref.shape[0] // parts

    @pl.when(j == 0)
    def _():
        acc_ref[...] = jnp.zeros_like(acc_ref)
        xb_ref[...] = _unpack_bf16_pairs(x_ref[...])

    def leading_rows(rows):
        x = xb_ref[:rows, :]
        gate = jnp.dot(x, wg_ref[0].astype(BF16), preferred_element_type=F32)
        up = jnp.dot(x, wu_ref[0].astype(BF16), preferred_element_type=F32)
        act = (gate * jax.nn.sigmoid(gate) * up).astype(BF16)
        acc_ref[:rows, :] += jnp.dot(act, wd_ref[0].astype(BF16), preferred_element_type=F32)

    for k in range(1, parts + 1):
        lo, hi = (k - 1) * part_rows, k * part_rows
        pl.when((nv_ref[i] > lo) & (nv_ref[i] <= hi))(functools.partial(leading_rows, hi))

    @pl.when(j == pl.num_programs(1) - 1)
    def _():
        o_ref[...] = _pack_bf16_pairs(acc_ref[...])


def moe_ffn(xs32, tile_expert, tile_valid, n_active, w_gu, w_down, *, tm, tf, parts):
    p, w = xs32.shape
    d = 2 * w
    f = w_down.shape[1]
    nf = f // tf

    def frozen(i, j, na):
        return jnp.where(i < na[0], j, nf - 1)

    grid_spec = pltpu.PrefetchScalarGridSpec(
        num_scalar_prefetch=3, grid=(p // tm, nf),
        in_specs=[pl.BlockSpec((tm, w), lambda i, j, te, nv, na: (jnp.minimum(i, na[0] - 1), 0)),
                  pl.BlockSpec((1, d, tf), lambda i, j, te, nv, na: (te[i], 0, frozen(i, j, na))),
                  pl.BlockSpec((1, d, tf), lambda i, j, te, nv, na: (te[i], 0, nf + frozen(i, j, na))),
                  pl.BlockSpec((1, tf, d), lambda i, j, te, nv, na: (te[i], frozen(i, j, na), 0))],
        out_specs=pl.BlockSpec((tm, w), lambda i, j, te, nv, na: (i, 0)),
        scratch_shapes=[pltpu.VMEM((tm, d), BF16), pltpu.VMEM((tm, d), F32)])
    return pl.pallas_call(
        functools.partial(_moe_ffn_body, parts=parts), grid_spec=grid_spec,
        out_shape=jax.ShapeDtypeStruct((p, w), jnp.uint32),
        compiler_params=_cparams("parallel", "arbitrary"), name="moe_ffn")(
            tile_expert, tile_valid, n_active, xs32, w_gu, w_gu, w_down)


def _combine_body(p1_ref, p2_ref, h_ref, info_ref, y_ref, o_ref, buf1, buf2, sem):
    tr = h_ref.shape[0]
    base = pl.program_id(0) * tr

    def row_copy(src, buf, r):
        return pltpu.make_async_copy(y_ref.at[pl.ds(src, 1)], buf.at[pl.ds(r, 1)], sem)

    def issue(r, carry):
        row_copy(p1_ref[base + r], buf1, r).start(priority=0)
        row_copy(p2_ref[base + r], buf2, r).start(priority=1)
        return carry

    def drain(r, carry):
        row_copy(p1_ref[base + r], buf1, r).wait()
        row_copy(p2_ref[base + r], buf2, r).wait()
        return carry

    lax.fori_loop(0, tr, issue, 0, unroll=DMA_LOOP_UNROLL)
    lax.fori_loop(0, tr, drain, 0, unroll=DMA_LOOP_UNROLL)
    info = info_ref[...]
    g1 = info[:, 4:5]
    g2 = info[:, 5:6]
    y1 = _unpack_bf16_pairs(buf1[...]).astype(F32)
    y2 = _unpack_bf16_pairs(buf2[...]).astype(F32)
    o_ref[...] = h_ref[...] + (g1 * y1 + g2 * y2)


def moe_combine(h, info, y32, pos1, pos2, *, tr):
    n, d = h.shape
    w = y32.shape[1]
    grid_spec = pltpu.PrefetchScalarGridSpec(
        num_scalar_prefetch=2, grid=(n // tr,),
        in_specs=[pl.BlockSpec((tr, d), lambda i, p1, p2: (i, 0)), pl.BlockSpec((tr, LANES), lambda i, p1, p2: (i, 0)),
                  pl.BlockSpec(memory_space=pl.ANY)],
        out_specs=pl.BlockSpec((tr, d), lambda i, p1, p2: (i, 0)),
        scratch_shapes=[pltpu.VMEM((tr, w), y32.dtype), pltpu.VMEM((tr, w), y32.dtype), pltpu.SemaphoreType.DMA])
    return pl.pallas_call(
        _combine_body, grid_spec=grid_spec, out_shape=jax.ShapeDtypeStruct((n, d), F32),
        compiler_params=_cparams("arbitrary"), name="moe_combine")(pos1, pos2, h, info, y32)


def moe_residual(h, nw, router, w_gu, w_down, *, tm_route, tr, tm, tf):
    n, d = h.shape
    router_terms = jnp.concatenate(_split3(router.astype(F32)), axis=1)
    router_pad = jnp.zeros((d, LANES), BF16).at[:, :3 * N_EXPERTS].set(router_terms)
    x32, info, counts = moe_router(h, nw, router_pad, tm=tm_route)
    cnt = counts[0, :N_EXPERTS].astype(jnp.int32)
    padded = (cnt + tm - 1) // tm * tm
    ends = jnp.cumsum(padded)
    offs = ends - padded
    e1 = info[:, 0].astype(jnp.int32)
    e2 = info[:, 1].astype(jnp.int32)
    pos1 = offs[e1] + info[:, 2].astype(jnp.int32)
    pos2 = offs[e2] + info[:, 3].astype(jnp.int32)
    p_rows = 2 * n + N_EXPERTS * tm
    n_tiles = p_rows // tm
    tile_start = jnp.arange(n_tiles, dtype=jnp.int32) * tm
    tile_expert = jnp.minimum(jnp.sum(tile_start[:, None] >= ends[None, :], axis=1), N_EXPERTS - 1).astype(jnp.int32)
    tile_valid = jnp.clip((offs + cnt)[tile_expert] - tile_start, 0, tm).astype(jnp.int32)
    n_active = (ends[-1:] // tm).astype(jnp.int32)
    zr = min(tm, 256)
    xs32 = moe_dispatch(x32, pos1, pos2, offs + cnt, padded - cnt, ends[-1:], (p_rows - ends[-1:]) // zr,
                        p_rows=p_rows, tr=tr, zr=zr)
    y32 = moe_ffn(xs32, tile_expert, tile_valid, n_active, w_gu, w_down, tm=tm, tf=tf, parts=MOE_TILE_PARTS)
    return moe_combine(h, info, y32, pos1, pos2, tr=tr)


def _ple_body(h_ref, nw_ref, wg_ref, bg_ref, p_ref, wp_ref, *rest):
    o_ref = rest[-1]
    h = h_ref[...]
    hn = _rms(h, nw_ref[...]).astype(BF16)
    z = jnp.dot(hn, wg_ref[...], preferred_element_type=F32) + bg_ref[...]
    proj = jnp.dot(p_ref[...].astype(BF16), wp_ref[...], preferred_element_type=F32)
    out = h + jax.nn.sigmoid(z) * proj
    if len(rest) == 2:
        out = _rms(out, rest[0][...])
    o_ref[...] = out


def ple_residual(h, nw, w_gate, b_gate, p_all, layer, w_proj, *, tm, final_norm_w=None):
    n, d = h.shape
    pd = p_all.shape[-1]
    in_specs = [pl.BlockSpec((tm, d), lambda i: (i, 0)), _resident((1, d)), _resident((d, d)), _resident((1, d)),
                pl.BlockSpec((None, tm, pd), lambda i: (layer, i, 0)), _resident((pd, d))]
    args = [h, nw, w_gate, b_gate, p_all, w_proj]
    if final_norm_w is not None:
        in_specs.append(_resident((1, d)))
        args.append(final_norm_w)
    return pl.pallas_call(
        _ple_body, grid=(n // tm,), in_specs=in_specs,
        out_specs=pl.BlockSpec((tm, d), lambda i: (i, 0)),
        out_shape=jax.ShapeDtypeStruct((n, d), F32),
        compiler_params=_cparams("parallel"), name="ple_residual")(*args)


def _row(v):
    return v.reshape(1, -1).astype(F32)


def _tile(n, want):
    return min(n, want)


def _gla_layer(h, norm_mix, w_in, w_a2, b_a2, gla_norm, w_o, *, batch, seq):
    n = h.shape[0]
    main = w_in.shape[1] - GLA_GATE_RANK
    w_main = w_in[:, :main].astype(BF16)
    w_a = jnp.zeros((w_in.shape[0], LANES), BF16).at[:, :GLA_GATE_RANK].set(w_in[:, main:].astype(BF16))
    w_a2_pad = jnp.zeros((LANES, w_a2.shape[1]), BF16).at[:GLA_GATE_RANK].set(w_a2.astype(BF16))
    qkvg, a = norm_matmul_extra(h, _row(norm_mix), w_main, w_a, tm=_tile(n, 256))
    o = gla_core(qkvg, a, w_a2_pad, _row(b_a2), _row(gla_norm), batch=batch, seq=seq, tb=_tile(seq, 1024),
                 heads_per_step=GLA_HEADS_PER_STEP)
    return matmul_residual(o, w_o.astype(BF16), h, tm=_tile(n, 512))


def _diff_layer(h, positions, norm_mix, w_in, lq1, lk1, lq2, lk2, diff_norm, w_o, *, batch, seq, lambda_init):
    n = h.shape[0]
    tabs = rope_tables(positions, tm=_tile(n, 1024))
    qk_width = 2 * DIFF_HEADS * DIFF_HEAD_DIM
    qkv = norm_matmul_rope(h, _row(norm_mix), w_in.astype(BF16), tabs, tm=_tile(n, 256),
                           q_width=qk_width, k_width=qk_width, q_scale=DIFF_HEAD_DIM ** -0.5 * math.log2(math.e))
    o = diff_attention_core(qkv, _row(lq1), _row(lk1), _row(lq2), _row(lk2), diff_norm.reshape(-1, 1).astype(F32),
                            batch=batch, seq=seq, tq=_tile(seq, 512), lambda_init=lambda_init)
    return matmul_residual(o, w_o.astype(BF16), h, tm=_tile(n, 512))


def _diff_lambda_init(layer_idx):
    return 0.8 - 0.6 * math.exp(-0.3 * layer_idx)


def kernel(x, p, positions, l0_norm_mix, l0_gla_w_in, l0_gla_w_a2, l0_gla_b_a2, l0_gla_norm, l0_gla_w_o, l0_norm_ffn, l0_ffn_w_gu, l0_ffn_w_down, l0_ple_norm, l0_ple_w_gate, l0_ple_b_gate, l0_ple_w_proj, l1_norm_mix, l1_pool_w, l1_pool_scale, l1_norm_ffn, l1_moe_router, l1_moe_w_gu, l1_moe_w_down, l1_ple_norm, l1_ple_w_gate, l1_ple_b_gate, l1_ple_w_proj, l2_norm_mix, l2_diff_w_in, l2_diff_lq1, l2_diff_lk1, l2_diff_lq2, l2_diff_lk2, l2_diff_norm, l2_diff_w_o, l2_norm_ffn, l2_ffn_w_gu, l2_ffn_w_down, l2_ple_norm, l2_ple_w_gate, l2_ple_b_gate, l2_ple_w_proj, l3_norm_mix, l3_gla_w_in, l3_gla_w_a2, l3_gla_b_a2, l3_gla_norm, l3_gla_w_o, l3_norm_ffn, l3_moe_router, l3_moe_w_gu, l3_moe_w_down, l3_ple_norm, l3_ple_w_gate, l3_ple_b_gate, l3_ple_w_proj, final_norm):
    batch, seq, d = x.shape
    n = batch * seq
    h = x.reshape(n, d)
    pf = p.reshape(p.shape[0], n, p.shape[-1])
    tm = _tile(n, 512)

    def ffn(h, nw, w_gu, w_down):
        return ffn_residual(h, _row(nw), w_gu.astype(BF16), w_down.astype(BF16), tm=tm, tf=512)

    def moe(h, nw, router, w_gu, w_down):
        return moe_residual(h, _row(nw), router, w_gu, w_down, tm_route=tm, tr=_tile(n, 256), tm=_tile(n, 1024), tf=256)

    def ple(h, i, nw, w_gate, b_gate, w_proj, final_norm_w=None):
        return ple_residual(h, _row(nw), w_gate.astype(BF16), _row(b_gate), pf, i, w_proj.astype(BF16), tm=tm,
                            final_norm_w=final_norm_w)

    h = _gla_layer(h, l0_norm_mix, l0_gla_w_in, l0_gla_w_a2, l0_gla_b_a2, l0_gla_norm, l0_gla_w_o, batch=batch, seq=seq)
    h = ffn(h, l0_norm_ffn, l0_ffn_w_gu, l0_ffn_w_down)
    h = ple(h, 0, l0_ple_norm, l0_ple_w_gate, l0_ple_b_gate, l0_ple_w_proj)

    h = pool_mixer(h, _row(l1_norm_mix), l1_pool_w.astype(BF16), _row(l1_pool_scale), seq=seq, tt=_tile(seq, 512))
    h = moe(h, l1_norm_ffn, l1_moe_router, l1_moe_w_gu, l1_moe_w_down)
    h = ple(h, 1, l1_ple_norm, l1_ple_w_gate, l1_ple_b_gate, l1_ple_w_proj)

    h = _diff_layer(h, positions, l2_norm_mix, l2_diff_w_in, l2_diff_lq1, l2_diff_lk1, l2_diff_lq2, l2_diff_lk2,
                    l2_diff_norm, l2_diff_w_o, batch=batch, seq=seq, lambda_init=_diff_lambda_init(2))
    h = ffn(h, l2_norm_ffn, l2_ffn_w_gu, l2_ffn_w_down)
    h = ple(h, 2, l2_ple_norm, l2_ple_w_gate, l2_ple_b_gate, l2_ple_w_proj)

    h = _gla_layer(h, l3_norm_mix, l3_gla_w_in, l3_gla_w_a2, l3_gla_b_a2, l3_gla_norm, l3_gla_w_o, batch=batch, seq=seq)
    h = moe(h, l3_norm_ffn, l3_moe_router, l3_moe_w_gu, l3_moe_w_down)
    h = ple(h, 3, l3_ple_norm, l3_ple_w_gate, l3_ple_b_gate, l3_ple_w_proj, final_norm_w=_row(final_norm))
    return h.reshape(batch, seq, d)
```

```python
import functools
import math

import jax
import jax.numpy as jnp
from jax import lax
from jax.experimental import pallas as pl
from jax.experimental.pallas import tpu as pltpu

F32 = jnp.float32
BF16 = jnp.bfloat16

NORM_EPS = 1e-6
LANES = 128
V7X_VMEM_BYTES = 64 * 1024 * 1024
VMEM_LIMIT = V7X_VMEM_BYTES * 7 // 8

GLA_HEADS = 4
GLA_GATE_RANK = 16
GLA_GATE_NORMALIZER = 16.0
GLA_CHUNK = 64
GLA_BLOCK = 256
GLA_HEADS_PER_STEP = 4
POOL_WINDOWS = (2, 4, 8, 16)
POOL_HALO = 16
DIFF_HEADS = 8
DIFF_HEAD_DIM = 128
ROPE_THETA = 500000.0
ROPE_DIMS = 32
N_EXPERTS = 8
MOE_TILE_PARTS = 2
DMA_LOOP_UNROLL = 8

NT_DIMS = (((1,), (1,)), ((), ()))
TN_DIMS = (((0,), (0,)), ((), ()))


def _cparams(*sem):
    return pltpu.CompilerParams(dimension_semantics=sem, vmem_limit_bytes=VMEM_LIMIT)


def _rms(x, w):
    ms = jnp.mean(x * x, axis=-1, keepdims=True)
    return x * lax.rsqrt(ms + NORM_EPS) * w


def _split3(x):
    hi = x.astype(BF16)
    r1 = x - hi.astype(F32)
    mid = r1.astype(BF16)
    lo = (r1 - mid.astype(F32)).astype(BF16)
    return hi, mid, lo


def _resident(shape):
    return pl.BlockSpec(shape, lambda i: (0,) * len(shape), pipeline_mode=pl.Buffered(1))


def _norm_mm_extra_body(h_ref, nw_ref, w_ref, wx_ref, o_ref, ox_ref):
    xn = _rms(h_ref[...], nw_ref[...]).astype(BF16)
    ox_ref[...] = jnp.dot(xn, wx_ref[...], preferred_element_type=F32)
    o_ref[...] = jnp.dot(xn, w_ref[...], preferred_element_type=F32).astype(o_ref.dtype)


def norm_matmul_extra(h, nw, w, w_extra, *, tm):
    n, d = h.shape
    nout = w.shape[1]
    nx = w_extra.shape[1]
    return pl.pallas_call(
        _norm_mm_extra_body, grid=(n // tm,),
        in_specs=[pl.BlockSpec((tm, d), lambda i: (i, 0)), _resident((1, d)), _resident((d, nout)), _resident((d, nx))],
        out_specs=[pl.BlockSpec((tm, nout), lambda i: (i, 0)), pl.BlockSpec((tm, nx), lambda i: (i, 0))],
        out_shape=[jax.ShapeDtypeStruct((n, nout), BF16), jax.ShapeDtypeStruct((n, nx), F32)],
        compiler_params=_cparams("parallel"), name="norm_matmul_extra")(h, nw, w, w_extra)


def _rope_tab_body(pos_ref, invf_ref, cos_ref, sina_ref, sinb_ref):
    ang = pos_ref[...].astype(F32) * invf_ref[...]
    lane = lax.broadcasted_iota(jnp.int32, ang.shape, 1)
    half = ROPE_DIMS // 2
    s = jnp.sin(ang)
    cos_ref[...] = jnp.cos(ang)
    sina_ref[...] = jnp.where((lane >= half) & (lane < ROPE_DIMS), s, 0.0)
    sinb_ref[...] = jnp.where(lane < half, -s, 0.0)


def rope_tables(positions, *, tm):
    n = positions.size
    half = ROPE_DIMS // 2
    inv_freq = ROPE_THETA ** (-jnp.arange(half, dtype=F32) * 2.0 / ROPE_DIMS)
    invf = jnp.zeros((1, LANES), F32).at[0, :half].set(inv_freq).at[0, half:ROPE_DIMS].set(inv_freq)
    pos = positions.reshape(n, 1)
    tab = jax.ShapeDtypeStruct((n, LANES), F32)
    t_spec = pl.BlockSpec((tm, LANES), lambda i: (i, 0))
    return pl.pallas_call(
        _rope_tab_body, grid=(n // tm,),
        in_specs=[pl.BlockSpec((tm, 1), lambda i: (i, 0)), pl.BlockSpec((1, LANES), lambda i: (0, 0))],
        out_specs=[t_spec, t_spec, t_spec], out_shape=[tab, tab, tab],
        compiler_params=_cparams("parallel"), name="rope_tables")(pos, invf)


def _norm_mm_rope_body(h_ref, nw_ref, w_ref, cos_ref, sina_ref, sinb_ref, o_ref, *, q_heads, k_heads, q_scale):
    xn = _rms(h_ref[...], nw_ref[...]).astype(BF16)
    acc = jnp.dot(xn, w_ref[...], preferred_element_type=F32)
    half = ROPE_DIMS // 2
    c, sa, sb = cos_ref[...], sina_ref[...], sinb_ref[...]
    cq, saq, sbq = c * q_scale, sa * q_scale, sb * q_scale
    for t in range(q_heads + k_heads):
        xj = acc[:, t * LANES:(t + 1) * LANES]
        tc, tsa, tsb = (cq, saq, sbq) if t < q_heads else (c, sa, sb)
        r = xj * tc + pltpu.roll(xj, half, 1) * tsa + pltpu.roll(xj, LANES - half, 1) * tsb
        o_ref[:, t * LANES:(t + 1) * LANES] = r.astype(o_ref.dtype)
    rest = (q_heads + k_heads) * LANES
    o_ref[:, rest:] = acc[:, rest:].astype(o_ref.dtype)


def norm_matmul_rope(h, nw, w, tabs, *, tm, q_width, k_width, q_scale):
    n, d = h.shape
    nout = w.shape[1]
    body = functools.partial(_norm_mm_rope_body, q_heads=q_width // LANES, k_heads=k_width // LANES, q_scale=q_scale)
    t_spec = pl.BlockSpec((tm, LANES), lambda i: (i, 0))
    return pl.pallas_call(
        body, grid=(n // tm,),
        in_specs=[pl.BlockSpec((tm, d), lambda i: (i, 0)), _resident((1, d)), _resident((d, nout)),
                  t_spec, t_spec, t_spec],
        out_specs=pl.BlockSpec((tm, nout), lambda i: (i, 0)),
        out_shape=jax.ShapeDtypeStruct((n, nout), BF16),
        compiler_params=_cparams("parallel"), name="norm_matmul_rope")(h, nw, w, *tabs)


def _mm_res_body(a_ref, w_ref, h_ref, o_ref):
    o_ref[...] = h_ref[...] + jnp.dot(a_ref[...], w_ref[...], preferred_element_type=F32)


def matmul_residual(a, w, h, *, tm):
    n, k = a.shape
    d = w.shape[1]
    return pl.pallas_call(
        _mm_res_body, grid=(n // tm,),
        in_specs=[pl.BlockSpec((tm, k), lambda i: (i, 0)), _resident((k, d)), pl.BlockSpec((tm, d), lambda i: (i, 0))],
        out_specs=pl.BlockSpec((tm, d), lambda i: (i, 0)),
        out_shape=jax.ShapeDtypeStruct((n, d), F32),
        compiler_params=_cparams("parallel"), name="matmul_residual")(a, w, h)


def _gla_body(q_ref, k_ref, v_ref, g_ref, a_ref, wa_ref, ba_ref, nw_ref, o_ref, st_ref, gk_ref, *, heads, block, sub,
              q_scale):
    dk = q_ref.shape[1] // heads
    dv = v_ref.shape[1] // heads

    @pl.when(pl.program_id(2) == 0)
    def _():
        st_ref[...] = jnp.zeros_like(st_ref)

    z = jnp.dot(a_ref[...].astype(BF16), wa_ref[...], preferred_element_type=F32) + ba_ref[...]
    gk_ref[...] = (jnp.minimum(z, 0.0) - jnp.log1p(jnp.exp(-jnp.abs(z)))) * (1.0 / GLA_GATE_NORMALIZER)

    row = lax.broadcasted_iota(jnp.int32, (block, block), 0)
    col = lax.broadcasted_iota(jnp.int32, (block, block), 1)
    tri = (col <= row).astype(BF16)
    nw = nw_ref[...]

    def head_block(r, hh):
        kc = slice(hh * dk, (hh + 1) * dk)
        vc = slice(hh * dv, (hh + 1) * dv)
        g_hi, g_mid, g_lo = _split3(gk_ref[r, kc])
        b = (jnp.dot(tri, g_hi, preferred_element_type=F32) + jnp.dot(tri, g_mid, preferred_element_type=F32)
             + jnp.dot(tri, g_lo, preferred_element_type=F32))
        b_last = b[block - 1:block, :]
        q = q_ref[r, kc].astype(F32) * q_scale
        k = k_ref[r, kc].astype(F32)
        v = v_ref[r, vc]
        st = st_ref[hh]
        q_in = (q * jnp.exp(b)).astype(BF16)
        k_out = (k * jnp.exp(b_last - b)).astype(BF16)
        o_inter = lax.dot_general(q_in, st.astype(BF16), NT_DIMS, preferred_element_type=F32)
        st_ref[hh] = st * jnp.exp(b_last) + lax.dot_general(v, k_out, TN_DIMS, preferred_element_type=F32)
        o_intra = []
        for i in range(block // sub):
            lo, hi = i * sub, (i + 1) * sub
            base = b[lo - 1:lo, :] if i else jnp.zeros_like(b_last)
            q_rel = (q[lo:hi] * jnp.exp(b[lo:hi] - base)).astype(BF16)
            k_rel = (k[:hi] * jnp.exp(base - b[:hi])).astype(BF16)
            s = lax.dot_general(q_rel, k_rel, NT_DIMS, preferred_element_type=F32)
            key = lax.broadcasted_iota(jnp.int32, (sub, hi), 1)
            qry = lax.broadcasted_iota(jnp.int32, (sub, hi), 0) + lo
            s = jnp.where(key <= qry, s, 0.0).astype(BF16)
            o_intra.append(jnp.dot(s, v[:hi], preferred_element_type=F32))
        o = o_inter + jnp.concatenate(o_intra, axis=0)
        gate = g_ref[r, vc].astype(F32)
        o_ref[r, vc] = (_rms(o, nw) * (gate * jax.nn.sigmoid(gate))).astype(o_ref.dtype)

    def step(c, carry):
        r = pl.ds(pl.multiple_of(c * block, block), block)
        for hh in range(heads):
            head_block(r, hh)
        return carry

    lax.fori_loop(0, q_ref.shape[0] // block, step, 0)


def gla_core(qkvg, a, w_a2, b_a2, norm_w, *, batch, seq, tb, heads_per_step):
    n = qkvg.shape[0]
    kd_total = w_a2.shape[1]
    dk = kd_total // GLA_HEADS
    vd_total = (qkvg.shape[1] - 2 * kd_total) // 2
    dv = vd_total // GLA_HEADS
    nt = seq // tb
    wk, wv = heads_per_step * dk, heads_per_step * dv
    kq = kd_total // wk
    kv = 2 * kd_total // wv
    kg = kv + vd_total // wv
    body = functools.partial(_gla_body, heads=heads_per_step, block=min(tb, GLA_BLOCK), sub=GLA_CHUNK, q_scale=dk ** -0.5)
    rows = lambda b, h, t: b * nt + t
    return pl.pallas_call(
        body, grid=(batch, GLA_HEADS // heads_per_step, nt),
        in_specs=[pl.BlockSpec((tb, wk), lambda b, h, t: (rows(b, h, t), h)),
                  pl.BlockSpec((tb, wk), lambda b, h, t: (rows(b, h, t), kq + h)),
                  pl.BlockSpec((tb, wv), lambda b, h, t: (rows(b, h, t), kv + h)),
                  pl.BlockSpec((tb, wv), lambda b, h, t: (rows(b, h, t), kg + h)),
                  pl.BlockSpec((tb, LANES), lambda b, h, t: (rows(b, h, t), 0)),
                  pl.BlockSpec((LANES, wk), lambda b, h, t: (0, h)),
                  pl.BlockSpec((1, wk), lambda b, h, t: (0, h)),
                  pl.BlockSpec((1, dv), lambda b, h, t: (0, 0))],
        out_specs=pl.BlockSpec((tb, wv), lambda b, h, t: (rows(b, h, t), h)),
        out_shape=jax.ShapeDtypeStruct((n, vd_total), BF16),
        scratch_shapes=[pltpu.VMEM((heads_per_step, dv, dk), F32), pltpu.VMEM((tb, wk), F32)],
        compiler_params=_cparams("parallel", "parallel", "arbitrary"), name="gla_core")(
            qkvg, qkvg, qkvg, qkvg, a, w_a2, b_a2, norm_w)


def _pool_body(h_ref, hp_ref, nw_ref, w_ref, sc_ref, o_ref, x_ref, *, tiles_per_seq, group):
    i = pl.program_id(0)
    tt = h_ref.shape[0]
    h = h_ref[...]
    nw = nw_ref[...]
    first = (i % tiles_per_seq) == 0
    x_ref[0:POOL_HALO, :] = jnp.where(first, 0.0, _rms(hp_ref[...], nw))
    x_ref[POOL_HALO:, :] = _rms(h, nw)
    pos = (i % tiles_per_seq) * tt + lax.broadcasted_iota(jnp.int32, (tt, 1), 0)
    for gi, win in enumerate(POOL_WINDOWS):
        cols = slice(gi * group, (gi + 1) * group)
        acc = x_ref[POOL_HALO:POOL_HALO + tt, cols]
        cur = acc
        for back in range(1, win):
            acc = acc + x_ref[POOL_HALO - back:POOL_HALO - back + tt, cols]
        count = jnp.minimum(pos + 1, win).astype(F32)
        pooled = (acc / count - cur).astype(BF16)
        y = jnp.dot(pooled, w_ref[gi], preferred_element_type=F32)
        o_ref[:, cols] = h[:, cols] + y * sc_ref[:, cols]


def pool_mixer(h, nw, w_groups, scale, *, seq, tt):
    n, d = h.shape
    ng, group, _ = w_groups.shape
    tiles_per_seq = seq // tt
    halo_blocks = tt // POOL_HALO
    body = functools.partial(_pool_body, tiles_per_seq=tiles_per_seq, group=group)
    return pl.pallas_call(
        body, grid=(n // tt,),
        in_specs=[pl.BlockSpec((tt, d), lambda i: (i, 0)),
                  pl.BlockSpec((POOL_HALO, d), lambda i: (jnp.maximum(i * halo_blocks - 1, 0), 0)),
                  pl.BlockSpec((1, d), lambda i: (0, 0)),
                  pl.BlockSpec((ng, group, group), lambda i: (0, 0, 0)),
                  pl.BlockSpec((1, d), lambda i: (0, 0))],
        out_specs=pl.BlockSpec((tt, d), lambda i: (i, 0)),
        out_shape=jax.ShapeDtypeStruct((n, d), F32),
        scratch_shapes=[pltpu.VMEM((tt + POOL_HALO, d), F32)],
        compiler_params=_cparams("parallel"), name="pool_mixer")(h, h, nw, w_groups, scale)


def _diff_body(q_ref, k_ref, v_ref, lq1_ref, lk1_ref, lq2_ref, lk2_ref, nw_ref, o_ref,
               qt_ref, vt_ref, sa_ref, sb_ref, m_ref, l_ref, acc_ref, *, lambda_init):
    qi = pl.program_id(2)
    tq = q_ref.shape[0]
    dh = DIFF_HEAD_DIM
    nk = vt_ref.shape[0]

    @pl.when(qi == 0)
    def _():
        def transpose_v(j, carry):
            r = pl.ds(pl.multiple_of(j * tq, tq), tq)
            vt_ref[j] = v_ref[r, :].astype(F32).T.astype(BF16)
            return carry

        lax.fori_loop(0, nk, transpose_v, 0)

    qt_ref[...] = q_ref[...].astype(F32).T.astype(BF16)
    m_ref[...] = jnp.full_like(m_ref, -jnp.inf)
    l_ref[...] = jnp.zeros_like(l_ref)
    acc_ref[...] = jnp.zeros_like(acc_ref)

    def scores(j, s_ref):
        k = k_ref[pl.ds(pl.multiple_of(j * tq, tq), tq), :]
        for i in range(2):
            s_ref[i] = jnp.dot(k[:, i * dh:(i + 1) * dh], qt_ref[i * dh:(i + 1) * dh, :],
                               preferred_element_type=F32)

    def consume(j, s_ref, masked):
        vt = vt_ref[j]
        if masked:
            key = lax.broadcasted_iota(jnp.int32, (tq, tq), 0)
            qry = lax.broadcasted_iota(jnp.int32, (tq, tq), 1)
            keep = key <= qry
        for i in range(2):
            s = s_ref[i]
            if masked:
                s = jnp.where(keep, s, -jnp.inf)
            m_old = m_ref[i]
            m_new = jnp.maximum(m_old, jnp.max(s, axis=0, keepdims=True))
            alpha = jnp.exp2(m_old - m_new)
            p = jnp.exp2(s - m_new)
            l_ref[i] = alpha * l_ref[i] + jnp.sum(p, axis=0, keepdims=True)
            acc_ref[i] = alpha * acc_ref[i] + jnp.dot(vt, p.astype(BF16), preferred_element_type=F32)
            m_ref[i] = m_new

    def pipelined_pair(t, carry):
        j = 2 * t
        scores(j + 1, sb_ref)
        consume(j, sa_ref, False)
        scores(j + 2, sa_ref)
        consume(j + 1, sb_ref, False)
        return carry

    scores(0, sa_ref)
    lax.fori_loop(0, qi // 2, pipelined_pair, 0)

    @pl.when(qi % 2 == 0)
    def _():
        consume(qi, sa_ref, True)

    @pl.when(qi % 2 == 1)
    def _():
        scores(qi, sb_ref)
        consume(qi - 1, sa_ref, False)
        consume(qi, sb_ref, True)

    lam = (jnp.exp(jnp.sum(lq1_ref[...] * lk1_ref[...], axis=-1, keepdims=True))
           - jnp.exp(jnp.sum(lq2_ref[...] * lk2_ref[...], axis=-1, keepdims=True)) + lambda_init)
    ot = acc_ref[0] * (1.0 / l_ref[0]) - lam * (acc_ref[1] * (1.0 / l_ref[1]))
    ms = jnp.mean(ot * ot, axis=0, keepdims=True)
    ot = ot * lax.rsqrt(ms + NORM_EPS) * (nw_ref[...] * (1.0 - lambda_init))
    o_ref[...] = ot.T.astype(o_ref.dtype)


def diff_attention_core(qkv, lq1, lk1, lq2, lk2, norm_col, *, batch, seq, tq, lambda_init):
    n = qkv.shape[0]
    dh = DIFF_HEAD_DIM
    nq = seq // tq
    k_off = DIFF_HEADS
    v_off = 2 * DIFF_HEADS
    vec = pl.BlockSpec((1, dh), lambda b, h, q: (0, 0))
    return pl.pallas_call(
        functools.partial(_diff_body, lambda_init=lambda_init), grid=(batch, DIFF_HEADS, nq),
        in_specs=[pl.BlockSpec((tq, 2 * dh), lambda b, h, q: (b * nq + q, h)),
                  pl.BlockSpec((seq, 2 * dh), lambda b, h, q: (b, k_off + h)),
                  pl.BlockSpec((seq, 2 * dh), lambda b, h, q: (b, v_off + h)),
                  vec, vec, vec, vec,
                  pl.BlockSpec((2 * dh, 1), lambda b, h, q: (0, 0))],
        out_specs=pl.BlockSpec((tq, 2 * dh), lambda b, h, q: (b * nq + q, h)),
        out_shape=jax.ShapeDtypeStruct((n, DIFF_HEADS * 2 * dh), BF16),
        scratch_shapes=[pltpu.VMEM((2 * dh, tq), BF16), pltpu.VMEM((nq, 2 * dh, tq), BF16),
                        pltpu.VMEM((2, tq, tq), F32), pltpu.VMEM((2, tq, tq), F32), pltpu.VMEM((2, 1, tq), F32),
                        pltpu.VMEM((2, 1, tq), F32), pltpu.VMEM((2, 2 * dh, tq), F32)],
        compiler_params=_cparams("parallel", "parallel", "arbitrary"), name="diff_attention")(
            qkv, qkv, qkv, lq1, lk1, lq2, lk2, norm_col)


def _ffn_body(h_ref, nw_ref, wg_ref, wu_ref, wd_ref, o_ref, xn_ref):
    @pl.when(pl.program_id(1) == 0)
    def _():
        h = h_ref[...]
        xn_ref[...] = _rms(h, nw_ref[...]).astype(BF16)
        o_ref[...] = h

    xn = xn_ref[...]
    gate = jnp.dot(xn, wg_ref[...], preferred_element_type=F32)
    up = jnp.dot(xn, wu_ref[...], preferred_element_type=F32)
    act = (gate * jax.nn.sigmoid(gate) * up).astype(BF16)
    o_ref[...] += jnp.dot(act, wd_ref[...], preferred_element_type=F32)


def ffn_residual(h, nw, w_gu, w_down, *, tm, tf):
    n, d = h.shape
    f = w_down.shape[0]
    nf = f // tf
    return pl.pallas_call(
        _ffn_body, grid=(n // tm, nf),
        in_specs=[pl.BlockSpec((tm, d), lambda i, j: (i, 0)), pl.BlockSpec((1, d), lambda i, j: (0, 0)),
                  pl.BlockSpec((d, tf), lambda i, j: (0, j)), pl.BlockSpec((d, tf), lambda i, j: (0, nf + j)),
                  pl.BlockSpec((tf, d), lambda i, j: (j, 0))],
        out_specs=pl.BlockSpec((tm, d), lambda i, j: (i, 0)),
        out_shape=jax.ShapeDtypeStruct((n, d), F32),
        scratch_shapes=[pltpu.VMEM((tm, d), BF16)],
        compiler_params=_cparams("parallel", "arbitrary"), name="ffn_residual")(h, nw, w_gu, w_gu, w_down)


def _pack_bf16_pairs(x):
    w = x.shape[1] // 2
    lo = pltpu.bitcast(x[:, :w].astype(BF16).astype(F32), jnp.uint32) >> 16
    hi = pltpu.bitcast(x[:, w:].astype(BF16).astype(F32), jnp.uint32)
    return hi | lo


def _unpack_bf16_pairs(u):
    lo = pltpu.bitcast(u << 16, F32).astype(BF16)
    hi = pltpu.bitcast(u & jnp.uint32(0xFFFF0000), F32).astype(BF16)
    return jnp.concatenate([lo, hi], axis=1)


def _router_body(h_ref, nw_ref, r_ref, xn_ref, info_ref, cnt_ref, carry_ref):
    i = pl.program_id(0)

    @pl.when(i == 0)
    def _():
        carry_ref[...] = jnp.zeros_like(carry_ref)

    xn = _rms(h_ref[...], nw_ref[...])
    xn_ref[...] = _pack_bf16_pairs(xn)
    x_hi, x_mid, x_lo = _split3(xn)
    dot = functools.partial(jnp.dot, preferred_element_type=F32)
    r3 = r_ref[...]
    a, b, c = dot(x_hi, r3), dot(x_mid, r3), dot(x_lo, r3)
    e = N_EXPERTS
    logits = ((c + pltpu.roll(b, LANES - e, 1) + pltpu.roll(a, LANES - 2 * e, 1))
              + (b + pltpu.roll(a, LANES - e, 1))) + a
    tm = logits.shape[0]
    lane = lax.broadcasted_iota(jnp.int32, logits.shape, 1)
    lane_f = lane.astype(F32)
    lg = jnp.where(lane < N_EXPERTS, logits, -jnp.inf)
    m1 = jnp.max(lg, axis=-1, keepdims=True)
    e1 = jnp.min(jnp.where(lg == m1, lane_f, float(LANES)), axis=-1, keepdims=True)
    lg2 = jnp.where(lane_f == e1, -jnp.inf, lg)
    m2 = jnp.max(lg2, axis=-1, keepdims=True)
    e2 = jnp.min(jnp.where(lg2 == m2, lane_f, float(LANES)), axis=-1, keepdims=True)
    ex = jnp.exp(m2 - m1)
    g1 = 1.0 / (1.0 + ex)
    g2 = ex / (1.0 + ex)
    hot1 = lane_f == e1
    hot2 = lane_f == e2
    onehot = jnp.where(hot1 | hot2, 1.0, 0.0)
    row = lax.broadcasted_iota(jnp.int32, (tm, tm), 0)
    col = lax.broadcasted_iota(jnp.int32, (tm, tm), 1)
    before = jnp.where(col < row, 1.0, 0.0).astype(BF16)
    cum = dot(before, onehot.astype(BF16)) + carry_ref[...]
    rank1 = jnp.sum(jnp.where(hot1, cum, 0.0), axis=-1, keepdims=True)
    rank2 = jnp.sum(jnp.where(hot2, cum, 0.0), axis=-1, keepdims=True)
    info = jnp.zeros(logits.shape, F32)
    for idx, val in enumerate((e1, e2, rank1, rank2, g1, g2)):
        info = jnp.where(lane == idx, val, info)
    info_ref[...] = info
    carry_ref[...] += jnp.sum(onehot, axis=0, keepdims=True)
    cnt_ref[...] = jnp.broadcast_to(carry_ref[...], cnt_ref.shape)


def moe_router(h, nw, router_pad, *, tm):
    n, d = h.shape
    return pl.pallas_call(
        _router_body, grid=(n // tm,),
        in_specs=[pl.BlockSpec((tm, d), lambda i: (i, 0)), pl.BlockSpec((1, d), lambda i: (0, 0)),
                  pl.BlockSpec((d, LANES), lambda i: (0, 0))],
        out_specs=[pl.BlockSpec((tm, d // 2), lambda i: (i, 0)), pl.BlockSpec((tm, LANES), lambda i: (i, 0)),
                   pl.BlockSpec((8, LANES), lambda i: (0, 0))],
        out_shape=[jax.ShapeDtypeStruct((n, d // 2), jnp.uint32), jax.ShapeDtypeStruct((n, LANES), F32),
                   jax.ShapeDtypeStruct((8, LANES), F32)],
        scratch_shapes=[pltpu.VMEM((1, LANES), F32)],
        compiler_params=_cparams("arbitrary"), name="moe_router")(h, nw, router_pad)


def _dispatch_body(p1_ref, p2_ref, ps_ref, pn_ref, ts_ref, tn_ref, x_ref, xs_ref, zero_ref, sem, zsem):
    tr = x_ref.shape[0]
    zr = zero_ref.shape[0]
    base = pl.program_id(0) * tr

    @pl.when(pl.program_id(0) == 0)
    def _():
        zero_ref[...] = jnp.zeros_like(zero_ref)

        def pad_copy(dst):
            return pltpu.make_async_copy(zero_ref.at[pl.ds(0, 1)], xs_ref.at[pl.ds(dst, 1)], zsem)

        def tail_copy(c):
            return pltpu.make_async_copy(zero_ref, xs_ref.at[pl.ds(pl.multiple_of(ts_ref[0] + c * zr, zr), zr)], zsem)

        for e in range(N_EXPERTS):
            lax.fori_loop(0, pn_ref[e], lambda r, c, e=e: (pad_copy(ps_ref[e] + r).start(), c)[1], 0)
        lax.fori_loop(0, tn_ref[0], lambda c, carry: (tail_copy(c).start(), carry)[1], 0)
        for e in range(N_EXPERTS):
            lax.fori_loop(0, pn_ref[e], lambda r, c, e=e: (pad_copy(ps_ref[e] + r).wait(), c)[1], 0)
        lax.fori_loop(0, tn_ref[0], lambda c, carry: (tail_copy(c).wait(), carry)[1], 0)

    def row_copy(r, dst):
        return pltpu.make_async_copy(x_ref.at[pl.ds(r, 1)], xs_ref.at[pl.ds(dst, 1)], sem)

    def issue(r, carry):
        row_copy(r, p1_ref[base + r]).start(priority=0)
        row_copy(r, p2_ref[base + r]).start(priority=1)
        return carry

    def drain(r, carry):
        row_copy(r, p1_ref[base + r]).wait()
        row_copy(r, p2_ref[base + r]).wait()
        return carry

    lax.fori_loop(0, tr, issue, 0, unroll=DMA_LOOP_UNROLL)
    lax.fori_loop(0, tr, drain, 0, unroll=DMA_LOOP_UNROLL)


def moe_dispatch(x32, pos1, pos2, pad_start, pad_len, tail_start, tail_chunks, *, p_rows, tr, zr):
    n, w = x32.shape
    grid_spec = pltpu.PrefetchScalarGridSpec(
        num_scalar_prefetch=6, grid=(n // tr,),
        in_specs=[pl.BlockSpec((tr, w), lambda i, *_: (i, 0))],
        out_specs=pl.BlockSpec(memory_space=pl.ANY),
        scratch_shapes=[pltpu.VMEM((zr, w), x32.dtype), pltpu.SemaphoreType.DMA, pltpu.SemaphoreType.DMA])
    return pl.pallas_call(
        _dispatch_body, grid_spec=grid_spec, out_shape=jax.ShapeDtypeStruct((p_rows, w), x32.dtype),
        compiler_params=_cparams("arbitrary"), name="moe_dispatch")(
            pos1, pos2, pad_start, pad_len, tail_start, tail_chunks, x32)


def _moe_ffn_body(te_ref, nv_ref, na_ref, x_ref, wg_ref, wu_ref, wd_ref, o_ref, xb_ref, acc_ref, *, parts):
    i = pl.program_id(0)
    j = pl.program_id(1)
    part_rows = x_ref.shape[0] // parts

    @pl.when(j == 0)
    def _():
        acc_ref[...] = jnp.zeros_like(acc_ref)
        xb_ref[...] = _unpack_bf16_pairs(x_ref[...])

    def leading_rows(rows):
        x = xb_ref[:rows, :]
        gate = jnp.dot(x, wg_ref[0].astype(BF16), preferred_element_type=F32)
        up = jnp.dot(x, wu_ref[0].astype(BF16), preferred_element_type=F32)
        act = (gate * jax.nn.sigmoid(gate) * up).astype(BF16)
        acc_ref[:rows, :] += jnp.dot(act, wd_ref[0].astype(BF16), preferred_element_type=F32)

    for k in range(1, parts + 1):
        lo, hi = (k - 1) * part_rows, k * part_rows
        pl.when((nv_ref[i] > lo) & (nv_ref[i] <= hi))(functools.partial(leading_rows, hi))

    @pl.when(j == pl.num_programs(1) - 1)
    def _():
        o_ref[...] = _pack_bf16_pairs(acc_ref[...])


def moe_ffn(xs32, tile_expert, tile_valid, n_active, w_gu, w_down, *, tm, tf, parts):
    p, w = xs32.shape
    d = 2 * w
    f = w_down.shape[1]
    nf = f // tf

    def frozen(i, j, na):
        return jnp.where(i < na[0], j, nf - 1)

    grid_spec = pltpu.PrefetchScalarGridSpec(
        num_scalar_prefetch=3, grid=(p // tm, nf),
        in_specs=[pl.BlockSpec((tm, w), lambda i, j, te, nv, na: (jnp.minimum(i, na[0] - 1), 0),
                               pipeline_mode=pl.Buffered(1)),
                  pl.BlockSpec((1, d, tf), lambda i, j, te, nv, na: (te[i], 0, frozen(i, j, na))),
                  pl.BlockSpec((1, d, tf), lambda i, j, te, nv, na: (te[i], 0, nf + frozen(i, j, na))),
                  pl.BlockSpec((1, tf, d), lambda i, j, te, nv, na: (te[i], frozen(i, j, na), 0))],
        out_specs=pl.BlockSpec((tm, w), lambda i, j, te, nv, na: (i, 0)),
        scratch_shapes=[pltpu.VMEM((tm, d), BF16), pltpu.VMEM((tm, d), F32)])
    return pl.pallas_call(
        functools.partial(_moe_ffn_body, parts=parts), grid_spec=grid_spec,
        out_shape=jax.ShapeDtypeStruct((p, w), jnp.uint32),
        compiler_params=_cparams("parallel", "arbitrary"), name="moe_ffn")(
            tile_expert, tile_valid, n_active, xs32, w_gu, w_gu, w_down)


def _combine_body(p1_ref, p2_ref, h_ref, info_ref, y_ref, o_ref, buf1, buf2, sem):
    tr = h_ref.shape[0]
    base = pl.program_id(0) * tr

    def row_copy(src, buf, r):
        return pltpu.make_async_copy(y_ref.at[pl.ds(src, 1)], buf.at[pl.ds(r, 1)], sem)

    def issue(r, carry):
        row_copy(p1_ref[base + r], buf1, r).start(priority=0)
        row_copy(p2_ref[base + r], buf2, r).start(priority=1)
        return carry

    def drain(r, carry):
        row_copy(p1_ref[base + r], buf1, r).wait()
        row_copy(p2_ref[base + r], buf2, r).wait()
        return carry

    lax.fori_loop(0, tr, issue, 0, unroll=DMA_LOOP_UNROLL)
    lax.fori_loop(0, tr, drain, 0, unroll=DMA_LOOP_UNROLL)
    info = info_ref[...]
    g1 = info[:, 4:5]
    g2 = info[:, 5:6]
    y1 = _unpack_bf16_pairs(buf1[...]).astype(F32)
    y2 = _unpack_bf16_pairs(buf2[...]).astype(F32)
    o_ref[...] = h_ref[...] + (g1 * y1 + g2 * y2)


def moe_combine(h, info, y32, pos1, pos2, *, tr):
    n, d = h.shape
    w = y32.shape[1]
    grid_spec = pltpu.PrefetchScalarGridSpec(
        num_scalar_prefetch=2, grid=(n // tr,),
        in_specs=[pl.BlockSpec((tr, d), lambda i, p1, p2: (i, 0)), pl.BlockSpec((tr, LANES), lambda i, p1, p2: (i, 0)),
                  pl.BlockSpec(memory_space=pl.ANY)],
        out_specs=pl.BlockSpec((tr, d), lambda i, p1, p2: (i, 0)),
        scratch_shapes=[pltpu.VMEM((tr, w), y32.dtype), pltpu.VMEM((tr, w), y32.dtype), pltpu.SemaphoreType.DMA])
    return pl.pallas_call(
        _combine_body, grid_spec=grid_spec, out_shape=jax.ShapeDtypeStruct((n, d), F32),
        compiler_params=_cparams("arbitrary"), name="moe_combine")(pos1, pos2, h, info, y32)


def moe_residual(h, nw, router, w_gu, w_down, *, tm_route, tr, tm, tf):
    n, d = h.shape
    router_terms = jnp.concatenate(_split3(router.astype(F32)), axis=1)
    router_pad = jnp.zeros((d, LANES), BF16).at[:, :3 * N_EXPERTS].set(router_terms)
    x32, info, counts = moe_router(h, nw, router_pad, tm=tm_route)
    cnt = counts[0, :N_EXPERTS].astype(jnp.int32)
    padded = (cnt + tm - 1) // tm * tm
    ends = jnp.cumsum(padded)
    offs = ends - padded
    e1 = info[:, 0].astype(jnp.int32)
    e2 = info[:, 1].astype(jnp.int32)
    pos1 = offs[e1] + info[:, 2].astype(jnp.int32)
    pos2 = offs[e2] + info[:, 3].astype(jnp.int32)
    p_rows = 2 * n + N_EXPERTS * tm
    n_tiles = p_rows // tm
    tile_start = jnp.arange(n_tiles, dtype=jnp.int32) * tm
    tile_expert = jnp.minimum(jnp.sum(tile_start[:, None] >= ends[None, :], axis=1), N_EXPERTS - 1).astype(jnp.int32)
    tile_valid = jnp.clip((offs + cnt)[tile_expert] - tile_start, 0, tm).astype(jnp.int32)
    n_active = (ends[-1:] // tm).astype(jnp.int32)
    zr = min(tm, 256)
    xs32 = moe_dispatch(x32, pos1, pos2, offs + cnt, padded - cnt, ends[-1:], (p_rows - ends[-1:]) // zr,
                        p_rows=p_rows, tr=tr, zr=zr)
    y32 = moe_ffn(xs32, tile_expert, tile_valid, n_active, w_gu, w_down, tm=tm, tf=tf, parts=MOE_TILE_PARTS)
    return moe_combine(h, info, y32, pos1, pos2, tr=tr)


def _ple_body(h_ref, nw_ref, wg_ref, bg_ref, p_ref, wp_ref, *rest):
    o_ref = rest[-1]
    h = h_ref[...]
    hn = _rms(h, nw_ref[...]).astype(BF16)
    z = jnp.dot(hn, wg_ref[...], preferred_element_type=F32) + bg_ref[...]
    proj = jnp.dot(p_ref[...].astype(BF16), wp_ref[...], preferred_element_type=F32)
    out = h + jax.nn.sigmoid(z) * proj
    if len(rest) == 2:
        out = _rms(out, rest[0][...])
    o_ref[...] = out


def ple_residual(h, nw, w_gate, b_gate, p_all, layer, w_proj, *, tm, final_norm_w=None):
    n, d = h.shape
    pd = p_all.shape[-1]
    in_specs = [pl.BlockSpec((tm, d), lambda i: (i, 0)), _resident((1, d)), _resident((d, d)), _resident((1, d)),
                pl.BlockSpec((None, tm, pd), lambda i: (layer, i, 0)), _resident((pd, d))]
    args = [h, nw, w_gate, b_gate, p_all, w_proj]
    if final_norm_w is not None:
        in_specs.append(_resident((1, d)))
        args.append(final_norm_w)
    return pl.pallas_call(
        _ple_body, grid=(n // tm,), in_specs=in_specs,
        out_specs=pl.BlockSpec((tm, d), lambda i: (i, 0)),
        out_shape=jax.ShapeDtypeStruct((n, d), F32),
        compiler_params=_cparams("parallel"), name="ple_residual")(*args)


def _row(v):
    return v.reshape(1, -1).astype(F32)


def _tile(n, want):
    return min(n, want)


def _gla_layer(h, norm_mix, w_in, w_a2, b_a2, gla_norm, w_o, *, batch, seq):
    n = h.shape[0]
    main = w_in.shape[1] - GLA_GATE_RANK
    w_main = w_in[:, :main].astype(BF16)
    w_a = jnp.zeros((w_in.shape[0], LANES), BF16).at[:, :GLA_GATE_RANK].set(w_in[:, main:].astype(BF16))
    w_a2_pad = jnp.zeros((LANES, w_a2.shape[1]), BF16).at[:GLA_GATE_RANK].set(w_a2.astype(BF16))
    qkvg, a = norm_matmul_extra(h, _row(norm_mix), w_main, w_a, tm=_tile(n, 256))
    o = gla_core(qkvg, a, w_a2_pad, _row(b_a2), _row(gla_norm), batch=batch, seq=seq, tb=_tile(seq, 1024),
                 heads_per_step=GLA_HEADS_PER_STEP)
    return matmul_residual(o, w_o.astype(BF16), h, tm=_tile(n, 512))


def _diff_layer(h, positions, norm_mix, w_in, lq1, lk1, lq2, lk2, diff_norm, w_o, *, batch, seq, lambda_init):
    n = h.shape[0]
    tabs = rope_tables(positions, tm=_tile(n, 1024))
    qk_width = 2 * DIFF_HEADS * DIFF_HEAD_DIM
    qkv = norm_matmul_rope(h, _row(norm_mix), w_in.astype(BF16), tabs, tm=_tile(n, 256),
                           q_width=qk_width, k_width=qk_width, q_scale=DIFF_HEAD_DIM ** -0.5 * math.log2(math.e))
    o = diff_attention_core(qkv, _row(lq1), _row(lk1), _row(lq2), _row(lk2), diff_norm.reshape(-1, 1).astype(F32),
                            batch=batch, seq=seq, tq=_tile(seq, 512), lambda_init=lambda_init)
    return matmul_residual(o, w_o.astype(BF16), h, tm=_tile(n, 512))


def _diff_lambda_init(layer_idx):
    return 0.8 - 0.6 * math.exp(-0.3 * layer_idx)


def kernel(x, p, positions, l0_norm_mix, l0_gla_w_in, l0_gla_w_a2, l0_gla_b_a2, l0_gla_norm, l0_gla_w_o, l0_norm_ffn, l0_ffn_w_gu, l0_ffn_w_down, l0_ple_norm, l0_ple_w_gate, l0_ple_b_gate, l0_ple_w_proj, l1_norm_mix, l1_pool_w, l1_pool_scale, l1_norm_ffn, l1_moe_router, l1_moe_w_gu, l1_moe_w_down, l1_ple_norm, l1_ple_w_gate, l1_ple_b_gate, l1_ple_w_proj, l2_norm_mix, l2_diff_w_in, l2_diff_lq1, l2_diff_lk1, l2_diff_lq2, l2_diff_lk2, l2_diff_norm, l2_diff_w_o, l2_norm_ffn, l2_ffn_w_gu, l2_ffn_w_down, l2_ple_norm, l2_ple_w_gate, l2_ple_b_gate, l2_ple_w_proj, l3_norm_mix, l3_gla_w_in, l3_gla_w_a2, l3_gla_b_a2, l3_gla_norm, l3_gla_w_o, l3_norm_ffn, l3_moe_router, l3_moe_w_gu, l3_moe_w_down, l3_ple_norm, l3_ple_w_gate, l3_ple_b_gate, l3_ple_w_proj, final_norm):
    batch, seq, d = x.shape
    n = batch * seq
    h = x.reshape(n, d)
    pf = p.reshape(p.shape[0], n, p.shape[-1])
    tm = _tile(n, 512)

    def ffn(h, nw, w_gu, w_down):
        return ffn_residual(h, _row(nw), w_gu.astype(BF16), w_down.astype(BF16), tm=tm, tf=512)

    def moe(h, nw, router, w_gu, w_down):
        return moe_residual(h, _row(nw), router, w_gu, w_down, tm_route=tm, tr=_tile(n, 256), tm=_tile(n, 1024), tf=512)

    def ple(h, i, nw, w_gate, b_gate, w_proj, final_norm_w=None):
        return ple_residual(h, _row(nw), w_gate.astype(BF16), _row(b_gate), pf, i, w_proj.astype(BF16), tm=tm,
                            final_norm_w=final_norm_w)

    h = _gla_layer(h, l0_norm_mix, l0_gla_w_in, l0_gla_w_a2, l0_gla_b_a2, l0_gla_norm, l0_gla_w_o, batch=batch, seq=seq)
    h = ffn(h, l0_norm_ffn, l0_ffn_w_gu, l0_ffn_w_down)
    h = ple(h, 0, l0_ple_norm, l0_ple_w_gate, l0_ple_b_gate, l0_ple_w_proj)

    h = pool_mixer(h, _row(l1_norm_mix), l1_pool_w.astype(BF16), _row(l1_pool_scale), seq=seq, tt=_tile(seq, 512))
    h = moe(h, l1_norm_ffn, l1_moe_router, l1_moe_w_gu, l1_moe_w_down)
    h = ple(h, 1, l1_ple_norm, l1_ple_w_gate, l1_ple_b_gate, l1_ple_w_proj)

    h = _diff_layer(h, positions, l2_norm_mix, l2_diff_w_in, l2_diff_lq1, l2_diff_lk1, l2_diff_lq2, l2_diff_lk2,
                    l2_diff_norm, l2_diff_w_o, batch=batch, seq=seq, lambda_init=_diff_lambda_init(2))
    h = ffn(h, l2_norm_ffn, l2_ffn_w_gu, l2_ffn_w_down)
    h = ple(h, 2, l2_ple_norm, l2_ple_w_gate, l2_ple_b_gate, l2_ple_w_proj)

    h = _gla_layer(h, l3_norm_mix, l3_gla_w_in, l3_gla_w_a2, l3_gla_b_a2, l3_gla_norm, l3_gla_w_o, batch=batch, seq=seq)
    h = moe(h, l3_norm_ffn, l3_moe_router, l3_moe_w_gu, l3_moe_w_down)
    h = ple(h, 3, l3_ple_norm, l3_ple_w_gate, l3_ple_b_gate, l3_ple_w_proj, final_norm_w=_row(final_norm))
    return h.reshape(batch, seq, d)
```

```python
import functools
import math

import jax
import jax.numpy as jnp
from jax import lax
from jax.experimental import pallas as pl
from jax.experimental.pallas import tpu as pltpu

F32 = jnp.float32
BF16 = jnp.bfloat16

NORM_EPS = 1e-6
LANES = 128
V7X_VMEM_BYTES = 64 * 1024 * 1024
VMEM_LIMIT = V7X_VMEM_BYTES * 7 // 8

GLA_HEADS = 4
GLA_GATE_RANK = 16
GLA_GATE_NORMALIZER = 16.0
GLA_CHUNK = 64
GLA_BLOCK = 256
GLA_HEADS_PER_STEP = 4
POOL_WINDOWS = (2, 4, 8, 16)
POOL_HALO = 16
DIFF_HEADS = 8
DIFF_HEAD_DIM = 128
ROPE_THETA = 500000.0
ROPE_DIMS = 32
N_EXPERTS = 8
MOE_TILE_PARTS = 3
DMA_LOOP_UNROLL = 8

NT_DIMS = (((1,), (1,)), ((), ()))
TN_DIMS = (((0,), (0,)), ((), ()))


def _cparams(*sem):
    return pltpu.CompilerParams(dimension_semantics=sem, vmem_limit_bytes=VMEM_LIMIT)


def _rms(x, w):
    ms = jnp.mean(x * x, axis=-1, keepdims=True)
    return x * lax.rsqrt(ms + NORM_EPS) * w


def _split3(x):
    hi = x.astype(BF16)
    r1 = x - hi.astype(F32)
    mid = r1.astype(BF16)
    lo = (r1 - mid.astype(F32)).astype(BF16)
    return hi, mid, lo


def _resident(shape):
    return pl.BlockSpec(shape, lambda i: (0,) * len(shape), pipeline_mode=pl.Buffered(1))


def _norm_mm_extra_body(h_ref, nw_ref, w_ref, wx_ref, o_ref, ox_ref):
    xn = _rms(h_ref[...], nw_ref[...]).astype(BF16)
    ox_ref[...] = jnp.dot(xn, wx_ref[...], preferred_element_type=F32)
    o_ref[...] = jnp.dot(xn, w_ref[...], preferred_element_type=F32).astype(o_ref.dtype)


def norm_matmul_extra(h, nw, w, w_extra, *, tm):
    n, d = h.shape
    nout = w.shape[1]
    nx = w_extra.shape[1]
    return pl.pallas_call(
        _norm_mm_extra_body, grid=(n // tm,),
        in_specs=[pl.BlockSpec((tm, d), lambda i: (i, 0)), _resident((1, d)), _resident((d, nout)), _resident((d, nx))],
        out_specs=[pl.BlockSpec((tm, nout), lambda i: (i, 0)), pl.BlockSpec((tm, nx), lambda i: (i, 0))],
        out_shape=[jax.ShapeDtypeStruct((n, nout), BF16), jax.ShapeDtypeStruct((n, nx), F32)],
        compiler_params=_cparams("parallel"), name="norm_matmul_extra")(h, nw, w, w_extra)


def _rope_tab_body(pos_ref, invf_ref, cos_ref, sina_ref, sinb_ref):
    ang = pos_ref[...].astype(F32) * invf_ref[...]
    lane = lax.broadcasted_iota(jnp.int32, ang.shape, 1)
    half = ROPE_DIMS // 2
    s = jnp.sin(ang)
    cos_ref[...] = jnp.cos(ang)
    sina_ref[...] = jnp.where((lane >= half) & (lane < ROPE_DIMS), s, 0.0)
    sinb_ref[...] = jnp.where(lane < half, -s, 0.0)


def rope_tables(positions, *, tm):
    n = positions.size
    half = ROPE_DIMS // 2
    inv_freq = ROPE_THETA ** (-jnp.arange(half, dtype=F32) * 2.0 / ROPE_DIMS)
    invf = jnp.zeros((1, LANES), F32).at[0, :half].set(inv_freq).at[0, half:ROPE_DIMS].set(inv_freq)
    pos = positions.reshape(n, 1)
    tab = jax.ShapeDtypeStruct((n, LANES), F32)
    t_spec = pl.BlockSpec((tm, LANES), lambda i: (i, 0))
    return pl.pallas_call(
        _rope_tab_body, grid=(n // tm,),
        in_specs=[pl.BlockSpec((tm, 1), lambda i: (i, 0)), pl.BlockSpec((1, LANES), lambda i: (0, 0))],
        out_specs=[t_spec, t_spec, t_spec], out_shape=[tab, tab, tab],
        compiler_params=_cparams("parallel"), name="rope_tables")(pos, invf)


def _norm_mm_rope_body(h_ref, nw_ref, w_ref, cos_ref, sina_ref, sinb_ref, o_ref, *, q_heads, k_heads, q_scale):
    xn = _rms(h_ref[...], nw_ref[...]).astype(BF16)
    acc = jnp.dot(xn, w_ref[...], preferred_element_type=F32)
    half = ROPE_DIMS // 2
    c, sa, sb = cos_ref[...], sina_ref[...], sinb_ref[...]
    cq, saq, sbq = c * q_scale, sa * q_scale, sb * q_scale
    for t in range(q_heads + k_heads):
        xj = acc[:, t * LANES:(t + 1) * LANES]
        tc, tsa, tsb = (cq, saq, sbq) if t < q_heads else (c, sa, sb)
        r = xj * tc + pltpu.roll(xj, half, 1) * tsa + pltpu.roll(xj, LANES - half, 1) * tsb
        o_ref[:, t * LANES:(t + 1) * LANES] = r.astype(o_ref.dtype)
    rest = (q_heads + k_heads) * LANES
    o_ref[:, rest:] = acc[:, rest:].astype(o_ref.dtype)


def norm_matmul_rope(h, nw, w, tabs, *, tm, q_width, k_width, q_scale):
    n, d = h.shape
    nout = w.shape[1]
    body = functools.partial(_norm_mm_rope_body, q_heads=q_width // LANES, k_heads=k_width // LANES, q_scale=q_scale)
    t_spec = pl.BlockSpec((tm, LANES), lambda i: (i, 0))
    return pl.pallas_call(
        body, grid=(n // tm,),
        in_specs=[pl.BlockSpec((tm, d), lambda i: (i, 0)), _resident((1, d)), _resident((d, nout)),
                  t_spec, t_spec, t_spec],
        out_specs=pl.BlockSpec((tm, nout), lambda i: (i, 0)),
        out_shape=jax.ShapeDtypeStruct((n, nout), BF16),
        compiler_params=_cparams("parallel"), name="norm_matmul_rope")(h, nw, w, *tabs)


def _mm_res_body(a_ref, w_ref, h_ref, o_ref):
    o_ref[...] = h_ref[...] + jnp.dot(a_ref[...], w_ref[...], preferred_element_type=F32)


def matmul_residual(a, w, h, *, tm):
    n, k = a.shape
    d = w.shape[1]
    return pl.pallas_call(
        _mm_res_body, grid=(n // tm,),
        in_specs=[pl.BlockSpec((tm, k), lambda i: (i, 0)), _resident((k, d)), pl.BlockSpec((tm, d), lambda i: (i, 0))],
        out_specs=pl.BlockSpec((tm, d), lambda i: (i, 0)),
        out_shape=jax.ShapeDtypeStruct((n, d), F32),
        compiler_params=_cparams("parallel"), name="matmul_residual")(a, w, h)


def _gla_body(q_ref, k_ref, v_ref, g_ref, a_ref, wa_ref, ba_ref, nw_ref, o_ref, st_ref, gk_ref, *, heads, block, sub,
              q_scale):
    dk = q_ref.shape[1] // heads
    dv = v_ref.shape[1] // heads

    @pl.when(pl.program_id(2) == 0)
    def _():
        st_ref[...] = jnp.zeros_like(st_ref)

    z = jnp.dot(a_ref[...].astype(BF16), wa_ref[...], preferred_element_type=F32) + ba_ref[...]
    gk_ref[...] = (jnp.minimum(z, 0.0) - jnp.log1p(jnp.exp(-jnp.abs(z)))) * (1.0 / GLA_GATE_NORMALIZER)

    row = lax.broadcasted_iota(jnp.int32, (block, block), 0)
    col = lax.broadcasted_iota(jnp.int32, (block, block), 1)
    tri = (col <= row).astype(BF16)
    nw = nw_ref[...]

    def head_block(r, hh):
        kc = slice(hh * dk, (hh + 1) * dk)
        vc = slice(hh * dv, (hh + 1) * dv)
        g_hi, g_mid, g_lo = _split3(gk_ref[r, kc])
        b = (jnp.dot(tri, g_hi, preferred_element_type=F32) + jnp.dot(tri, g_mid, preferred_element_type=F32)
             + jnp.dot(tri, g_lo, preferred_element_type=F32))
        b_last = b[block - 1:block, :]
        q = q_ref[r, kc].astype(F32) * q_scale
        k = k_ref[r, kc].astype(F32)
        v = v_ref[r, vc]
        st = st_ref[hh]
        q_in = (q * jnp.exp(b)).astype(BF16)
        k_out = (k * jnp.exp(b_last - b)).astype(BF16)
        o_inter = lax.dot_general(q_in, st.astype(BF16), NT_DIMS, preferred_element_type=F32)
        st_ref[hh] = st * jnp.exp(b_last) + lax.dot_general(v, k_out, TN_DIMS, preferred_element_type=F32)
        o_intra = []
        for i in range(block // sub):
            lo, hi = i * sub, (i + 1) * sub
            base = b[lo - 1:lo, :] if i else jnp.zeros_like(b_last)
            q_rel = (q[lo:hi] * jnp.exp(b[lo:hi] - base)).astype(BF16)
            k_rel = (k[:hi] * jnp.exp(base - b[:hi])).astype(BF16)
            s = lax.dot_general(q_rel, k_rel, NT_DIMS, preferred_element_type=F32)
            key = lax.broadcasted_iota(jnp.int32, (sub, hi), 1)
            qry = lax.broadcasted_iota(jnp.int32, (sub, hi), 0) + lo
            s = jnp.where(key <= qry, s, 0.0).astype(BF16)
            o_intra.append(jnp.dot(s, v[:hi], preferred_element_type=F32))
        o = o_inter + jnp.concatenate(o_intra, axis=0)
        gate = g_ref[r, vc].astype(F32)
        o_ref[r, vc] = (_rms(o, nw) * (gate * jax.nn.sigmoid(gate))).astype(o_ref.dtype)

    def step(c, carry):
        r = pl.ds(pl.multiple_of(c * block, block), block)
        for hh in range(heads):
            head_block(r, hh)
        return carry

    lax.fori_loop(0, q_ref.shape[0] // block, step, 0)


def gla_core(qkvg, a, w_a2, b_a2, norm_w, *, batch, seq, tb, heads_per_step):
    n = qkvg.shape[0]
    kd_total = w_a2.shape[1]
    dk = kd_total // GLA_HEADS
    vd_total = (qkvg.shape[1] - 2 * kd_total) // 2
    dv = vd_total // GLA_HEADS
    nt = seq // tb
    wk, wv = heads_per_step * dk, heads_per_step * dv
    kq = kd_total // wk
    kv = 2 * kd_total // wv
    kg = kv + vd_total // wv
    body = functools.partial(_gla_body, heads=heads_per_step, block=min(tb, GLA_BLOCK), sub=GLA_CHUNK, q_scale=dk ** -0.5)
    rows = lambda b, h, t: b * nt + t
    return pl.pallas_call(
        body, grid=(batch, GLA_HEADS // heads_per_step, nt),
        in_specs=[pl.BlockSpec((tb, wk), lambda b, h, t: (rows(b, h, t), h)),
                  pl.BlockSpec((tb, wk), lambda b, h, t: (rows(b, h, t), kq + h)),
                  pl.BlockSpec((tb, wv), lambda b, h, t: (rows(b, h, t), kv + h)),
                  pl.BlockSpec((tb, wv), lambda b, h, t: (rows(b, h, t), kg + h)),
                  pl.BlockSpec((tb, LANES), lambda b, h, t: (rows(b, h, t), 0)),
                  pl.BlockSpec((LANES, wk), lambda b, h, t: (0, h)),
                  pl.BlockSpec((1, wk), lambda b, h, t: (0, h)),
                  pl.BlockSpec((1, dv), lambda b, h, t: (0, 0))],
        out_specs=pl.BlockSpec((tb, wv), lambda b, h, t: (rows(b, h, t), h)),
        out_shape=jax.ShapeDtypeStruct((n, vd_total), BF16),
        scratch_shapes=[pltpu.VMEM((heads_per_step, dv, dk), F32), pltpu.VMEM((tb, wk), F32)],
        compiler_params=_cparams("parallel", "parallel", "arbitrary"), name="gla_core")(
            qkvg, qkvg, qkvg, qkvg, a, w_a2, b_a2, norm_w)


def _pool_body(h_ref, hp_ref, nw_ref, w_ref, sc_ref, o_ref, x_ref, *, tiles_per_seq, group):
    i = pl.program_id(0)
    tt = h_ref.shape[0]
    h = h_ref[...]
    nw = nw_ref[...]
    first = (i % tiles_per_seq) == 0
    x_ref[0:POOL_HALO, :] = jnp.where(first, 0.0, _rms(hp_ref[...], nw))
    x_ref[POOL_HALO:, :] = _rms(h, nw)
    pos = (i % tiles_per_seq) * tt + lax.broadcasted_iota(jnp.int32, (tt, 1), 0)
    for gi, win in enumerate(POOL_WINDOWS):
        cols = slice(gi * group, (gi + 1) * group)
        acc = x_ref[POOL_HALO:POOL_HALO + tt, cols]
        cur = acc
        for back in range(1, win):
            acc = acc + x_ref[POOL_HALO - back:POOL_HALO - back + tt, cols]
        count = jnp.minimum(pos + 1, win).astype(F32)
        pooled = (acc / count - cur).astype(BF16)
        y = jnp.dot(pooled, w_ref[gi], preferred_element_type=F32)
        o_ref[:, cols] = h[:, cols] + y * sc_ref[:, cols]


def pool_mixer(h, nw, w_groups, scale, *, seq, tt):
    n, d = h.shape
    ng, group, _ = w_groups.shape
    tiles_per_seq = seq // tt
    halo_blocks = tt // POOL_HALO
    body = functools.partial(_pool_body, tiles_per_seq=tiles_per_seq, group=group)
    return pl.pallas_call(
        body, grid=(n // tt,),
        in_specs=[pl.BlockSpec((tt, d), lambda i: (i, 0)),
                  pl.BlockSpec((POOL_HALO, d), lambda i: (jnp.maximum(i * halo_blocks - 1, 0), 0)),
                  pl.BlockSpec((1, d), lambda i: (0, 0)),
                  pl.BlockSpec((ng, group, group), lambda i: (0, 0, 0)),
                  pl.BlockSpec((1, d), lambda i: (0, 0))],
        out_specs=pl.BlockSpec((tt, d), lambda i: (i, 0)),
        out_shape=jax.ShapeDtypeStruct((n, d), F32),
        scratch_shapes=[pltpu.VMEM((tt + POOL_HALO, d), F32)],
        compiler_params=_cparams("parallel"), name="pool_mixer")(h, h, nw, w_groups, scale)


def _diff_body(q_ref, k_ref, v_ref, lq1_ref, lk1_ref, lq2_ref, lk2_ref, nw_ref, o_ref,
               qt_ref, vt_ref, sa_ref, sb_ref, m_ref, l_ref, acc_ref, *, lambda_init):
    qi = pl.program_id(2)
    tq = q_ref.shape[0]
    dh = DIFF_HEAD_DIM
    nk = vt_ref.shape[0]

    @pl.when(qi == 0)
    def _():
        def transpose_v(j, carry):
            r = pl.ds(pl.multiple_of(j * tq, tq), tq)
            vt_ref[j] = v_ref[r, :].astype(F32).T.astype(BF16)
            return carry

        lax.fori_loop(0, nk, transpose_v, 0)

    qt_ref[...] = q_ref[...].astype(F32).T.astype(BF16)
    m_ref[...] = jnp.full_like(m_ref, -jnp.inf)
    l_ref[...] = jnp.zeros_like(l_ref)
    acc_ref[...] = jnp.zeros_like(acc_ref)

    def scores(j, s_ref):
        k = k_ref[pl.ds(pl.multiple_of(j * tq, tq), tq), :]
        for i in range(2):
            s_ref[i] = jnp.dot(k[:, i * dh:(i + 1) * dh], qt_ref[i * dh:(i + 1) * dh, :],
                               preferred_element_type=F32)

    def consume(j, s_ref, masked):
        vt = vt_ref[j]
        if masked:
            key = lax.broadcasted_iota(jnp.int32, (tq, tq), 0)
            qry = lax.broadcasted_iota(jnp.int32, (tq, tq), 1)
            keep = key <= qry
        for i in range(2):
            s = s_ref[i]
            if masked:
                s = jnp.where(keep, s, -jnp.inf)
            m_old = m_ref[i]
            m_new = jnp.maximum(m_old, jnp.max(s, axis=0, keepdims=True))
            alpha = jnp.exp2(m_old - m_new)
            p = jnp.exp2(s - m_new)
            l_ref[i] = alpha * l_ref[i] + jnp.sum(p, axis=0, keepdims=True)
            acc_ref[i] = alpha * acc_ref[i] + jnp.dot(vt, p.astype(BF16), preferred_element_type=F32)
            m_ref[i] = m_new

    def pipelined_pair(t, carry):
        j = 2 * t
        scores(j + 1, sb_ref)
        consume(j, sa_ref, False)
        scores(j + 2, sa_ref)
        consume(j + 1, sb_ref, False)
        return carry

    scores(0, sa_ref)
    lax.fori_loop(0, qi // 2, pipelined_pair, 0)

    @pl.when(qi % 2 == 0)
    def _():
        consume(qi, sa_ref, True)

    @pl.when(qi % 2 == 1)
    def _():
        scores(qi, sb_ref)
        consume(qi - 1, sa_ref, False)
        consume(qi, sb_ref, True)

    lam = (jnp.exp(jnp.sum(lq1_ref[...] * lk1_ref[...], axis=-1, keepdims=True))
           - jnp.exp(jnp.sum(lq2_ref[...] * lk2_ref[...], axis=-1, keepdims=True)) + lambda_init)
    ot = acc_ref[0] * (1.0 / l_ref[0]) - lam * (acc_ref[1] * (1.0 / l_ref[1]))
    ms = jnp.mean(ot * ot, axis=0, keepdims=True)
    ot = ot * lax.rsqrt(ms + NORM_EPS) * (nw_ref[...] * (1.0 - lambda_init))
    o_ref[...] = ot.T.astype(o_ref.dtype)


def diff_attention_core(qkv, lq1, lk1, lq2, lk2, norm_col, *, batch, seq, tq, lambda_init):
    n = qkv.shape[0]
    dh = DIFF_HEAD_DIM
    nq = seq // tq
    k_off = DIFF_HEADS
    v_off = 2 * DIFF_HEADS
    vec = pl.BlockSpec((1, dh), lambda b, h, q: (0, 0))
    return pl.pallas_call(
        functools.partial(_diff_body, lambda_init=lambda_init), grid=(batch, DIFF_HEADS, nq),
        in_specs=[pl.BlockSpec((tq, 2 * dh), lambda b, h, q: (b * nq + q, h)),
                  pl.BlockSpec((seq, 2 * dh), lambda b, h, q: (b, k_off + h)),
                  pl.BlockSpec((seq, 2 * dh), lambda b, h, q: (b, v_off + h)),
                  vec, vec, vec, vec,
                  pl.BlockSpec((2 * dh, 1), lambda b, h, q: (0, 0))],
        out_specs=pl.BlockSpec((tq, 2 * dh), lambda b, h, q: (b * nq + q, h)),
        out_shape=jax.ShapeDtypeStruct((n, DIFF_HEADS * 2 * dh), BF16),
        scratch_shapes=[pltpu.VMEM((2 * dh, tq), BF16), pltpu.VMEM((nq, 2 * dh, tq), BF16),
                        pltpu.VMEM((2, tq, tq), F32), pltpu.VMEM((2, tq, tq), F32), pltpu.VMEM((2, 1, tq), F32),
                        pltpu.VMEM((2, 1, tq), F32), pltpu.VMEM((2, 2 * dh, tq), F32)],
        compiler_params=_cparams("parallel", "parallel", "arbitrary"), name="diff_attention")(
            qkv, qkv, qkv, lq1, lk1, lq2, lk2, norm_col)


def _ffn_body(h_ref, nw_ref, wg_ref, wu_ref, wd_ref, o_ref, xn_ref):
    @pl.when(pl.program_id(1) == 0)
    def _():
        h = h_ref[...]
        xn_ref[...] = _rms(h, nw_ref[...]).astype(BF16)
        o_ref[...] = h

    xn = xn_ref[...]
    gate = jnp.dot(xn, wg_ref[...], preferred_element_type=F32)
    up = jnp.dot(xn, wu_ref[...], preferred_element_type=F32)
    act = (gate * jax.nn.sigmoid(gate) * up).astype(BF16)
    o_ref[...] += jnp.dot(act, wd_ref[...], preferred_element_type=F32)


def ffn_residual(h, nw, w_gu, w_down, *, tm, tf):
    n, d = h.shape
    f = w_down.shape[0]
    nf = f // tf
    return pl.pallas_call(
        _ffn_body, grid=(n // tm, nf),
        in_specs=[pl.BlockSpec((tm, d), lambda i, j: (i, 0)), pl.BlockSpec((1, d), lambda i, j: (0, 0)),
                  pl.BlockSpec((d, tf), lambda i, j: (0, j)), pl.BlockSpec((d, tf), lambda i, j: (0, nf + j)),
                  pl.BlockSpec((tf, d), lambda i, j: (j, 0))],
        out_specs=pl.BlockSpec((tm, d), lambda i, j: (i, 0)),
        out_shape=jax.ShapeDtypeStruct((n, d), F32),
        scratch_shapes=[pltpu.VMEM((tm, d), BF16)],
        compiler_params=_cparams("parallel", "arbitrary"), name="ffn_residual")(h, nw, w_gu, w_gu, w_down)


def _pack_bf16_pairs(x):
    w = x.shape[1] // 2
    lo = pltpu.bitcast(x[:, :w].astype(BF16).astype(F32), jnp.uint32) >> 16
    hi = pltpu.bitcast(x[:, w:].astype(BF16).astype(F32), jnp.uint32)
    return hi | lo


def _unpack_bf16_pairs(u):
    lo = pltpu.bitcast(u << 16, F32).astype(BF16)
    hi = pltpu.bitcast(u & jnp.uint32(0xFFFF0000), F32).astype(BF16)
    return jnp.concatenate([lo, hi], axis=1)


def _router_body(h_ref, nw_ref, r_ref, xn_ref, info_ref, cnt_ref, carry_ref):
    i = pl.program_id(0)

    @pl.when(i == 0)
    def _():
        carry_ref[...] = jnp.zeros_like(carry_ref)

    xn = _rms(h_ref[...], nw_ref[...])
    xn_ref[...] = _pack_bf16_pairs(xn)
    x_hi, x_mid, x_lo = _split3(xn)
    dot = functools.partial(jnp.dot, preferred_element_type=F32)
    r3 = r_ref[...]
    a, b, c = dot(x_hi, r3), dot(x_mid, r3), dot(x_lo, r3)
    e = N_EXPERTS
    logits = ((c + pltpu.roll(b, LANES - e, 1) + pltpu.roll(a, LANES - 2 * e, 1))
              + (b + pltpu.roll(a, LANES - e, 1))) + a
    tm = logits.shape[0]
    lane = lax.broadcasted_iota(jnp.int32, logits.shape, 1)
    lane_f = lane.astype(F32)
    lg = jnp.where(lane < N_EXPERTS, logits, -jnp.inf)
    m1 = jnp.max(lg, axis=-1, keepdims=True)
    e1 = jnp.min(jnp.where(lg == m1, lane_f, float(LANES)), axis=-1, keepdims=True)
    lg2 = jnp.where(lane_f == e1, -jnp.inf, lg)
    m2 = jnp.max(lg2, axis=-1, keepdims=True)
    e2 = jnp.min(jnp.where(lg2 == m2, lane_f, float(LANES)), axis=-1, keepdims=True)
    ex = jnp.exp(m2 - m1)
    g1 = 1.0 / (1.0 + ex)
    g2 = ex / (1.0 + ex)
    hot1 = lane_f == e1
    hot2 = lane_f == e2
    onehot = jnp.where(hot1 | hot2, 1.0, 0.0)
    row = lax.broadcasted_iota(jnp.int32, (tm, tm), 0)
    col = lax.broadcasted_iota(jnp.int32, (tm, tm), 1)
    before = jnp.where(col < row, 1.0, 0.0).astype(BF16)
    cum = dot(before, onehot.astype(BF16)) + carry_ref[...]
    rank1 = jnp.sum(jnp.where(hot1, cum, 0.0), axis=-1, keepdims=True)
    rank2 = jnp.sum(jnp.where(hot2, cum, 0.0), axis=-1, keepdims=True)
    info = jnp.zeros(logits.shape, F32)
    for idx, val in enumerate((e1, e2, rank1, rank2, g1, g2)):
        info = jnp.where(lane == idx, val, info)
    info_ref[...] = info
    carry_ref[...] += jnp.sum(onehot, axis=0, keepdims=True)
    cnt_ref[...] = jnp.broadcast_to(carry_ref[...], cnt_ref.shape)


def moe_router(h, nw, router_pad, *, tm):
    n, d = h.shape
    return pl.pallas_call(
        _router_body, grid=(n // tm,),
        in_specs=[pl.BlockSpec((tm, d), lambda i: (i, 0)), pl.BlockSpec((1, d), lambda i: (0, 0)),
                  pl.BlockSpec((d, LANES), lambda i: (0, 0))],
        out_specs=[pl.BlockSpec((tm, d // 2), lambda i: (i, 0)), pl.BlockSpec((tm, LANES), lambda i: (i, 0)),
                   pl.BlockSpec((8, LANES), lambda i: (0, 0))],
        out_shape=[jax.ShapeDtypeStruct((n, d // 2), jnp.uint32), jax.ShapeDtypeStruct((n, LANES), F32),
                   jax.ShapeDtypeStruct((8, LANES), F32)],
        scratch_shapes=[pltpu.VMEM((1, LANES), F32)],
        compiler_params=_cparams("arbitrary"), name="moe_router")(h, nw, router_pad)


def _dispatch_body(p1_ref, p2_ref, ps_ref, pn_ref, ts_ref, tn_ref, x_ref, xs_ref, zero_ref, sem, zsem):
    tr = x_ref.shape[0]
    zr = zero_ref.shape[0]
    base = pl.program_id(0) * tr

    @pl.when(pl.program_id(0) == 0)
    def _():
        zero_ref[...] = jnp.zeros_like(zero_ref)

        def pad_copy(dst):
            return pltpu.make_async_copy(zero_ref.at[pl.ds(0, 1)], xs_ref.at[pl.ds(dst, 1)], zsem)

        def tail_copy(c):
            return pltpu.make_async_copy(zero_ref, xs_ref.at[pl.ds(pl.multiple_of(ts_ref[0] + c * zr, zr), zr)], zsem)

        for e in range(N_EXPERTS):
            lax.fori_loop(0, pn_ref[e], lambda r, c, e=e: (pad_copy(ps_ref[e] + r).start(), c)[1], 0)
        lax.fori_loop(0, tn_ref[0], lambda c, carry: (tail_copy(c).start(), carry)[1], 0)
        for e in range(N_EXPERTS):
            lax.fori_loop(0, pn_ref[e], lambda r, c, e=e: (pad_copy(ps_ref[e] + r).wait(), c)[1], 0)
        lax.fori_loop(0, tn_ref[0], lambda c, carry: (tail_copy(c).wait(), carry)[1], 0)

    def row_copy(r, dst):
        return pltpu.make_async_copy(x_ref.at[pl.ds(r, 1)], xs_ref.at[pl.ds(dst, 1)], sem)

    def issue(r, carry):
        row_copy(r, p1_ref[base + r]).start(priority=0)
        row_copy(r, p2_ref[base + r]).start(priority=1)
        return carry

    def drain(r, carry):
        row_copy(r, p1_ref[base + r]).wait()
        row_copy(r, p2_ref[base + r]).wait()
        return carry

    lax.fori_loop(0, tr, issue, 0, unroll=DMA_LOOP_UNROLL)
    lax.fori_loop(0, tr, drain, 0, unroll=DMA_LOOP_UNROLL)


def moe_dispatch(x32, pos1, pos2, pad_start, pad_len, tail_start, tail_chunks, *, p_rows, tr, zr):
    n, w = x32.shape
    grid_spec = pltpu.PrefetchScalarGridSpec(
        num_scalar_prefetch=6, grid=(n // tr,),
        in_specs=[pl.BlockSpec((tr, w), lambda i, *_: (i, 0))],
        out_specs=pl.BlockSpec(memory_space=pl.ANY),
        scratch_shapes=[pltpu.VMEM((zr, w), x32.dtype), pltpu.SemaphoreType.DMA, pltpu.SemaphoreType.DMA])
    return pl.pallas_call(
        _dispatch_body, grid_spec=grid_spec, out_shape=jax.ShapeDtypeStruct((p_rows, w), x32.dtype),
        compiler_params=_cparams("arbitrary"), name="moe_dispatch")(
            pos1, pos2, pad_start, pad_len, tail_start, tail_chunks, x32)


def _moe_ffn_body(te_ref, nv_ref, na_ref, x_ref, wg_ref, wu_ref, wd_ref, o_ref, xb_ref, acc_ref, *, parts):
    i = pl.program_id(0)
    j = pl.program_id(1)
    part_rows = x_ref.shape[0] // parts

    @pl.when(j == 0)
    def _():
        acc_ref[...] = jnp.zeros_like(acc_ref)
        xb_ref[...] = _unpack_bf16_pairs(x_ref[...])

    def leading_rows(rows):
        x = xb_ref[:rows, :]
        gate = jnp.dot(x, wg_ref[0].astype(BF16), preferred_element_type=F32)
        up = jnp.dot(x, wu_ref[0].astype(BF16), preferred_element_type=F32)
        act = (gate * jax.nn.sigmoid(gate) * up).astype(BF16)
        acc_ref[:rows, :] += jnp.dot(act, wd_ref[0].astype(BF16), preferred_element_type=F32)

    for k in range(1, parts + 1):
        lo, hi = (k - 1) * part_rows, k * part_rows
        pl.when((nv_ref[i] > lo) & (nv_ref[i] <= hi))(functools.partial(leading_rows, hi))

    @pl.when(j == pl.num_programs(1) - 1)
    def _():
        o_ref[...] = _pack_bf16_pairs(acc_ref[...])


def moe_ffn(xs32, tile_expert, tile_valid, n_active, w_gu, w_down, *, tm, tf, parts):
    p, w = xs32.shape
    d = 2 * w
    f = w_down.shape[1]
    nf = f // tf

    def frozen(i, j, na):
        return jnp.where(i < na[0], j, nf - 1)

    grid_spec = pltpu.PrefetchScalarGridSpec(
        num_scalar_prefetch=3, grid=(p // tm, nf),
        in_specs=[pl.BlockSpec((tm, w), lambda i, j, te, nv, na: (jnp.minimum(i, na[0] - 1), 0),
                               pipeline_mode=pl.Buffered(1)),
                  pl.BlockSpec((1, d, tf), lambda i, j, te, nv, na: (te[i], 0, frozen(i, j, na))),
                  pl.BlockSpec((1, d, tf), lambda i, j, te, nv, na: (te[i], 0, nf + frozen(i, j, na))),
                  pl.BlockSpec((1, tf, d), lambda i, j, te, nv, na: (te[i], frozen(i, j, na), 0))],
        out_specs=pl.BlockSpec((tm, w), lambda i, j, te, nv, na: (i, 0), pipeline_mode=pl.Buffered(1)),
        scratch_shapes=[pltpu.VMEM((tm, d), BF16), pltpu.VMEM((tm, d), F32)])
    return pl.pallas_call(
        functools.partial(_moe_ffn_body, parts=parts), grid_spec=grid_spec,
        out_shape=jax.ShapeDtypeStruct((p, w), jnp.uint32),
        compiler_params=_cparams("parallel", "arbitrary"), name="moe_ffn")(
            tile_expert, tile_valid, n_active, xs32, w_gu, w_gu, w_down)


def _combine_body(p1_ref, p2_ref, h_ref, info_ref, y_ref, o_ref, buf1, buf2, sem):
    tr = h_ref.shape[0]
    base = pl.program_id(0) * tr

    def row_copy(src, buf, r):
        return pltpu.make_async_copy(y_ref.at[pl.ds(src, 1)], buf.at[pl.ds(r, 1)], sem)

    def issue(r, carry):
        row_copy(p1_ref[base + r], buf1, r).start(priority=0)
        row_copy(p2_ref[base + r], buf2, r).start(priority=1)
        return carry

    def drain(r, carry):
        row_copy(p1_ref[base + r], buf1, r).wait()
        row_copy(p2_ref[base + r], buf2, r).wait()
        return carry

    lax.fori_loop(0, tr, issue, 0, unroll=DMA_LOOP_UNROLL)
    lax.fori_loop(0, tr, drain, 0, unroll=DMA_LOOP_UNROLL)
    info = info_ref[...]
    g1 = info[:, 4:5]
    g2 = info[:, 5:6]
    y1 = _unpack_bf16_pairs(buf1[...]).astype(F32)
    y2 = _unpack_bf16_pairs(buf2[...]).astype(F32)
    o_ref[...] = h_ref[...] + (g1 * y1 + g2 * y2)


def moe_combine(h, info, y32, pos1, pos2, *, tr):
    n, d = h.shape
    w = y32.shape[1]
    grid_spec = pltpu.PrefetchScalarGridSpec(
        num_scalar_prefetch=2, grid=(n // tr,),
        in_specs=[pl.BlockSpec((tr, d), lambda i, p1, p2: (i, 0)), pl.BlockSpec((tr, LANES), lambda i, p1, p2: (i, 0)),
                  pl.BlockSpec(memory_space=pl.ANY)],
        out_specs=pl.BlockSpec((tr, d), lambda i, p1, p2: (i, 0)),
        scratch_shapes=[pltpu.VMEM((tr, w), y32.dtype), pltpu.VMEM((tr, w), y32.dtype), pltpu.SemaphoreType.DMA])
    return pl.pallas_call(
        _combine_body, grid_spec=grid_spec, out_shape=jax.ShapeDtypeStruct((n, d), F32),
        compiler_params=_cparams("arbitrary"), name="moe_combine")(pos1, pos2, h, info, y32)


def moe_residual(h, nw, router, w_gu, w_down, *, tm_route, tr, tm, tf):
    n, d = h.shape
    router_terms = jnp.concatenate(_split3(router.astype(F32)), axis=1)
    router_pad = jnp.zeros((d, LANES), BF16).at[:, :3 * N_EXPERTS].set(router_terms)
    x32, info, counts = moe_router(h, nw, router_pad, tm=tm_route)
    cnt = counts[0, :N_EXPERTS].astype(jnp.int32)
    padded = (cnt + tm - 1) // tm * tm
    ends = jnp.cumsum(padded)
    offs = ends - padded
    e1 = info[:, 0].astype(jnp.int32)
    e2 = info[:, 1].astype(jnp.int32)
    pos1 = offs[e1] + info[:, 2].astype(jnp.int32)
    pos2 = offs[e2] + info[:, 3].astype(jnp.int32)
    p_rows = -(-(2 * n + N_EXPERTS * tm) // tm) * tm
    n_tiles = p_rows // tm
    tile_start = jnp.arange(n_tiles, dtype=jnp.int32) * tm
    tile_expert = jnp.minimum(jnp.sum(tile_start[:, None] >= ends[None, :], axis=1), N_EXPERTS - 1).astype(jnp.int32)
    tile_valid = jnp.clip((offs + cnt)[tile_expert] - tile_start, 0, tm).astype(jnp.int32)
    n_active = (ends[-1:] // tm).astype(jnp.int32)
    zr = min(tm, 256)
    xs32 = moe_dispatch(x32, pos1, pos2, offs + cnt, padded - cnt, ends[-1:], (p_rows - ends[-1:]) // zr,
                        p_rows=p_rows, tr=tr, zr=zr)
    y32 = moe_ffn(xs32, tile_expert, tile_valid, n_active, w_gu, w_down, tm=tm, tf=tf, parts=MOE_TILE_PARTS)
    return moe_combine(h, info, y32, pos1, pos2, tr=tr)


def _ple_body(h_ref, nw_ref, wg_ref, bg_ref, p_ref, wp_ref, *rest):
    o_ref = rest[-1]
    h = h_ref[...]
    hn = _rms(h, nw_ref[...]).astype(BF16)
    z = jnp.dot(hn, wg_ref[...], preferred_element_type=F32) + bg_ref[...]
    proj = jnp.dot(p_ref[...].astype(BF16), wp_ref[...], preferred_element_type=F32)
    out = h + jax.nn.sigmoid(z) * proj
    if len(rest) == 2:
        out = _rms(out, rest[0][...])
    o_ref[...] = out


def ple_residual(h, nw, w_gate, b_gate, p_all, layer, w_proj, *, tm, final_norm_w=None):
    n, d = h.shape
    pd = p_all.shape[-1]
    in_specs = [pl.BlockSpec((tm, d), lambda i: (i, 0)), _resident((1, d)), _resident((d, d)), _resident((1, d)),
                pl.BlockSpec((None, tm, pd), lambda i: (layer, i, 0)), _resident((pd, d))]
    args = [h, nw, w_gate, b_gate, p_all, w_proj]
    if final_norm_w is not None:
        in_specs.append(_resident((1, d)))
        args.append(final_norm_w)
    return pl.pallas_call(
        _ple_body, grid=(n // tm,), in_specs=in_specs,
        out_specs=pl.BlockSpec((tm, d), lambda i: (i, 0)),
        out_shape=jax.ShapeDtypeStruct((n, d), F32),
        compiler_params=_cparams("parallel"), name="ple_residual")(*args)


def _row(v):
    return v.reshape(1, -1).astype(F32)


def _tile(n, want):
    return min(n, want)


def _gla_layer(h, norm_mix, w_in, w_a2, b_a2, gla_norm, w_o, *, batch, seq):
    n = h.shape[0]
    main = w_in.shape[1] - GLA_GATE_RANK
    w_main = w_in[:, :main].astype(BF16)
    w_a = jnp.zeros((w_in.shape[0], LANES), BF16).at[:, :GLA_GATE_RANK].set(w_in[:, main:].astype(BF16))
    w_a2_pad = jnp.zeros((LANES, w_a2.shape[1]), BF16).at[:GLA_GATE_RANK].set(w_a2.astype(BF16))
    qkvg, a = norm_matmul_extra(h, _row(norm_mix), w_main, w_a, tm=_tile(n, 256))
    o = gla_core(qkvg, a, w_a2_pad, _row(b_a2), _row(gla_norm), batch=batch, seq=seq, tb=_tile(seq, 1024),
                 heads_per_step=GLA_HEADS_PER_STEP)
    return matmul_residual(o, w_o.astype(BF16), h, tm=_tile(n, 512))


def _diff_layer(h, positions, norm_mix, w_in, lq1, lk1, lq2, lk2, diff_norm, w_o, *, batch, seq, lambda_init):
    n = h.shape[0]
    tabs = rope_tables(positions, tm=_tile(n, 1024))
    qk_width = 2 * DIFF_HEADS * DIFF_HEAD_DIM
    qkv = norm_matmul_rope(h, _row(norm_mix), w_in.astype(BF16), tabs, tm=_tile(n, 256),
                           q_width=qk_width, k_width=qk_width, q_scale=DIFF_HEAD_DIM ** -0.5 * math.log2(math.e))
    o = diff_attention_core(qkv, _row(lq1), _row(lk1), _row(lq2), _row(lk2), diff_norm.reshape(-1, 1).astype(F32),
                            batch=batch, seq=seq, tq=_tile(seq, 512), lambda_init=lambda_init)
    return matmul_residual(o, w_o.astype(BF16), h, tm=_tile(n, 512))


def _diff_lambda_init(layer_idx):
    return 0.8 - 0.6 * math.exp(-0.3 * layer_idx)


def kernel(x, p, positions, l0_norm_mix, l0_gla_w_in, l0_gla_w_a2, l0_gla_b_a2, l0_gla_norm, l0_gla_w_o, l0_norm_ffn, l0_ffn_w_gu, l0_ffn_w_down, l0_ple_norm, l0_ple_w_gate, l0_ple_b_gate, l0_ple_w_proj, l1_norm_mix, l1_pool_w, l1_pool_scale, l1_norm_ffn, l1_moe_router, l1_moe_w_gu, l1_moe_w_down, l1_ple_norm, l1_ple_w_gate, l1_ple_b_gate, l1_ple_w_proj, l2_norm_mix, l2_diff_w_in, l2_diff_lq1, l2_diff_lk1, l2_diff_lq2, l2_diff_lk2, l2_diff_norm, l2_diff_w_o, l2_norm_ffn, l2_ffn_w_gu, l2_ffn_w_down, l2_ple_norm, l2_ple_w_gate, l2_ple_b_gate, l2_ple_w_proj, l3_norm_mix, l3_gla_w_in, l3_gla_w_a2, l3_gla_b_a2, l3_gla_norm, l3_gla_w_o, l3_norm_ffn, l3_moe_router, l3_moe_w_gu, l3_moe_w_down, l3_ple_norm, l3_ple_w_gate, l3_ple_b_gate, l3_ple_w_proj, final_norm):
    batch, seq, d = x.shape
    n = batch * seq
    h = x.reshape(n, d)
    pf = p.reshape(p.shape[0], n, p.shape[-1])
    tm = _tile(n, 512)

    def ffn(h, nw, w_gu, w_down):
        return ffn_residual(h, _row(nw), w_gu.astype(BF16), w_down.astype(BF16), tm=_tile(n, 1024), tf=256)

    def moe(h, nw, router, w_gu, w_down):
        return moe_residual(h, _row(nw), router, w_gu, w_down, tm_route=tm, tr=_tile(n, 256), tm=_tile(n, 1536), tf=256)

    def ple(h, i, nw, w_gate, b_gate, w_proj, final_norm_w=None):
        return ple_residual(h, _row(nw), w_gate.astype(BF16), _row(b_gate), pf, i, w_proj.astype(BF16), tm=tm,
                            final_norm_w=final_norm_w)

    h = _gla_layer(h, l0_norm_mix, l0_gla_w_in, l0_gla_w_a2, l0_gla_b_a2, l0_gla_norm, l0_gla_w_o, batch=batch, seq=seq)
    h = ffn(h, l0_norm_ffn, l0_ffn_w_gu, l0_ffn_w_down)
    h = ple(h, 0, l0_ple_norm, l0_ple_w_gate, l0_ple_b_gate, l0_ple_w_proj)

    h = pool_mixer(h, _row(l1_norm_mix), l1_pool_w.astype(BF16), _row(l1_pool_scale), seq=seq, tt=_tile(seq, 512))
    h = moe(h, l1_norm_ffn, l1_moe_router, l1_moe_w_gu, l1_moe_w_down)
    h = ple(h, 1, l1_ple_norm, l1_ple_w_gate, l1_ple_b_gate, l1_ple_w_proj)

    h = _diff_layer(h, positions, l2_norm_mix, l2_diff_w_in, l2_diff_lq1, l2_diff_lk1, l2_diff_lq2, l2_diff_lk2,
                    l2_diff_norm, l2_diff_w_o, batch=batch, seq=seq, lambda_init=_diff_lambda_init(2))
    h = ffn(h, l2_norm_ffn, l2_ffn_w_gu, l2_ffn_w_down)
    h = ple(h, 2, l2_ple_norm, l2_ple_w_gate, l2_ple_b_gate, l2_ple_w_proj)

    h = _gla_layer(h, l3_norm_mix, l3_gla_w_in, l3_gla_w_a2, l3_gla_b_a2, l3_gla_norm, l3_gla_w_o, batch=batch, seq=seq)
    h = moe(h, l3_norm_ffn, l3_moe_router, l3_moe_w_gu, l3_moe_w_down)
    h = ple(h, 3, l3_ple_norm, l3_ple_w_gate, l3_ple_b_gate, l3_ple_w_proj, final_norm_w=_row(final_norm))
    return h.reshape(batch, seq, d)
```

```python
import functools
import math

import jax
import jax.numpy as jnp
from jax import lax
from jax.experimental import pallas as pl
from jax.experimental.pallas import tpu as pltpu

F32 = jnp.float32
BF16 = jnp.bfloat16

NORM_EPS = 1e-6
LANES = 128
V7X_VMEM_BYTES = 64 * 1024 * 1024
VMEM_LIMIT = V7X_VMEM_BYTES * 7 // 8

GLA_HEADS = 4
GLA_GATE_RANK = 16
GLA_GATE_NORMALIZER = 16.0
GLA_CHUNK = 64
GLA_BLOCK = 256
GLA_HEADS_PER_STEP = 4
POOL_WINDOWS = (2, 4, 8, 16)
POOL_HALO = 16
DIFF_HEADS = 8
DIFF_HEAD_DIM = 128
ROPE_THETA = 500000.0
ROPE_DIMS = 32
N_EXPERTS = 8
MOE_TILE_PARTS = 2
DMA_LOOP_UNROLL = 8

NT_DIMS = (((1,), (1,)), ((), ()))
TN_DIMS = (((0,), (0,)), ((), ()))


def _cparams(*sem):
    return pltpu.CompilerParams(dimension_semantics=sem, vmem_limit_bytes=VMEM_LIMIT)


def _rms(x, w):
    ms = jnp.mean(x * x, axis=-1, keepdims=True)
    return x * lax.rsqrt(ms + NORM_EPS) * w


def _split3(x):
    hi = x.astype(BF16)
    r1 = x - hi.astype(F32)
    mid = r1.astype(BF16)
    lo = (r1 - mid.astype(F32)).astype(BF16)
    return hi, mid, lo


def _resident(shape):
    return pl.BlockSpec(shape, lambda i: (0,) * len(shape), pipeline_mode=pl.Buffered(1))


def _norm_mm_extra_body(h_ref, nw_ref, w_ref, wx_ref, o_ref, ox_ref):
    xn = _rms(h_ref[...], nw_ref[...]).astype(BF16)
    ox_ref[...] = jnp.dot(xn, wx_ref[...], preferred_element_type=F32)
    o_ref[...] = jnp.dot(xn, w_ref[...], preferred_element_type=F32).astype(o_ref.dtype)


def norm_matmul_extra(h, nw, w, w_extra, *, tm):
    n, d = h.shape
    nout = w.shape[1]
    nx = w_extra.shape[1]
    return pl.pallas_call(
        _norm_mm_extra_body, grid=(n // tm,),
        in_specs=[pl.BlockSpec((tm, d), lambda i: (i, 0)), _resident((1, d)), _resident((d, nout)), _resident((d, nx))],
        out_specs=[pl.BlockSpec((tm, nout), lambda i: (i, 0)), pl.BlockSpec((tm, nx), lambda i: (i, 0))],
        out_shape=[jax.ShapeDtypeStruct((n, nout), BF16), jax.ShapeDtypeStruct((n, nx), F32)],
        compiler_params=_cparams("parallel"), name="norm_matmul_extra")(h, nw, w, w_extra)


def _rope_tab_body(pos_ref, invf_ref, cos_ref, sina_ref, sinb_ref):
    ang = pos_ref[...].astype(F32) * invf_ref[...]
    lane = lax.broadcasted_iota(jnp.int32, ang.shape, 1)
    half = ROPE_DIMS // 2
    s = jnp.sin(ang)
    cos_ref[...] = jnp.cos(ang)
    sina_ref[...] = jnp.where((lane >= half) & (lane < ROPE_DIMS), s, 0.0)
    sinb_ref[...] = jnp.where(lane < half, -s, 0.0)


def rope_tables(positions, *, tm):
    n = positions.size
    half = ROPE_DIMS // 2
    inv_freq = ROPE_THETA ** (-jnp.arange(half, dtype=F32) * 2.0 / ROPE_DIMS)
    invf = jnp.zeros((1, LANES), F32).at[0, :half].set(inv_freq).at[0, half:ROPE_DIMS].set(inv_freq)
    pos = positions.reshape(n, 1)
    tab = jax.ShapeDtypeStruct((n, LANES), F32)
    t_spec = pl.BlockSpec((tm, LANES), lambda i: (i, 0))
    return pl.pallas_call(
        _rope_tab_body, grid=(n // tm,),
        in_specs=[pl.BlockSpec((tm, 1), lambda i: (i, 0)), pl.BlockSpec((1, LANES), lambda i: (0, 0))],
        out_specs=[t_spec, t_spec, t_spec], out_shape=[tab, tab, tab],
        compiler_params=_cparams("parallel"), name="rope_tables")(pos, invf)


def _norm_mm_rope_body(h_ref, nw_ref, w_ref, cos_ref, sina_ref, sinb_ref, o_ref, *, q_heads, k_heads, q_scale):
    xn = _rms(h_ref[...], nw_ref[...]).astype(BF16)
    acc = jnp.dot(xn, w_ref[...], preferred_element_type=F32)
    half = ROPE_DIMS // 2
    c, sa, sb = cos_ref[...], sina_ref[...], sinb_ref[...]
    cq, saq, sbq = c * q_scale, sa * q_scale, sb * q_scale
    for t in range(q_heads + k_heads):
        xj = acc[:, t * LANES:(t + 1) * LANES]
        tc, tsa, tsb = (cq, saq, sbq) if t < q_heads else (c, sa, sb)
        r = xj * tc + pltpu.roll(xj, half, 1) * tsa + pltpu.roll(xj, LANES - half, 1) * tsb
        o_ref[:, t * LANES:(t + 1) * LANES] = r.astype(o_ref.dtype)
    rest = (q_heads + k_heads) * LANES
    o_ref[:, rest:] = acc[:, rest:].astype(o_ref.dtype)


def norm_matmul_rope(h, nw, w, tabs, *, tm, q_width, k_width, q_scale):
    n, d = h.shape
    nout = w.shape[1]
    body = functools.partial(_norm_mm_rope_body, q_heads=q_width // LANES, k_heads=k_width // LANES, q_scale=q_scale)
    t_spec = pl.BlockSpec((tm, LANES), lambda i: (i, 0))
    return pl.pallas_call(
        body, grid=(n // tm,),
        in_specs=[pl.BlockSpec((tm, d), lambda i: (i, 0)), _resident((1, d)), _resident((d, nout)),
                  t_spec, t_spec, t_spec],
        out_specs=pl.BlockSpec((tm, nout), lambda i: (i, 0)),
        out_shape=jax.ShapeDtypeStruct((n, nout), BF16),
        compiler_params=_cparams("parallel"), name="norm_matmul_rope")(h, nw, w, *tabs)


def _mm_res_body(a_ref, w_ref, h_ref, o_ref):
    o_ref[...] = h_ref[...] + jnp.dot(a_ref[...], w_ref[...], preferred_element_type=F32)


def matmul_residual(a, w, h, *, tm):
    n, k = a.shape
    d = w.shape[1]
    return pl.pallas_call(
        _mm_res_body, grid=(n // tm,),
        in_specs=[pl.BlockSpec((tm, k), lambda i: (i, 0)), _resident((k, d)), pl.BlockSpec((tm, d), lambda i: (i, 0))],
        out_specs=pl.BlockSpec((tm, d), lambda i: (i, 0)),
        out_shape=jax.ShapeDtypeStruct((n, d), F32),
        compiler_params=_cparams("parallel"), name="matmul_residual")(a, w, h)


def _gla_body(q_ref, k_ref, v_ref, g_ref, a_ref, wa_ref, ba_ref, nw_ref, o_ref, st_ref, gk_ref, *, heads, block, sub,
              q_scale):
    dk = q_ref.shape[1] // heads
    dv = v_ref.shape[1] // heads

    @pl.when(pl.program_id(2) == 0)
    def _():
        st_ref[...] = jnp.zeros_like(st_ref)

    z = jnp.dot(a_ref[...].astype(BF16), wa_ref[...], preferred_element_type=F32) + ba_ref[...]
    gk_ref[...] = (jnp.minimum(z, 0.0) - jnp.log1p(jnp.exp(-jnp.abs(z)))) * (1.0 / GLA_GATE_NORMALIZER)

    row = lax.broadcasted_iota(jnp.int32, (block, block), 0)
    col = lax.broadcasted_iota(jnp.int32, (block, block), 1)
    tri = (col <= row).astype(BF16)
    nw = nw_ref[...]

    def head_block(r, hh):
        kc = slice(hh * dk, (hh + 1) * dk)
        vc = slice(hh * dv, (hh + 1) * dv)
        g_hi, g_mid, g_lo = _split3(gk_ref[r, kc])
        b = (jnp.dot(tri, g_hi, preferred_element_type=F32) + jnp.dot(tri, g_mid, preferred_element_type=F32)
             + jnp.dot(tri, g_lo, preferred_element_type=F32))
        b_last = b[block - 1:block, :]
        q = q_ref[r, kc].astype(F32) * q_scale
        k = k_ref[r, kc].astype(F32)
        v = v_ref[r, vc]
        st = st_ref[hh]
        q_in = (q * jnp.exp(b)).astype(BF16)
        k_out = (k * jnp.exp(b_last - b)).astype(BF16)
        o_inter = lax.dot_general(q_in, st.astype(BF16), NT_DIMS, preferred_element_type=F32)
        st_ref[hh] = st * jnp.exp(b_last) + lax.dot_general(v, k_out, TN_DIMS, preferred_element_type=F32)
        o_intra = []
        for i in range(block // sub):
            lo, hi = i * sub, (i + 1) * sub
            base = b[lo - 1:lo, :] if i else jnp.zeros_like(b_last)
            q_rel = (q[lo:hi] * jnp.exp(b[lo:hi] - base)).astype(BF16)
            k_rel = (k[:hi] * jnp.exp(base - b[:hi])).astype(BF16)
            s = lax.dot_general(q_rel, k_rel, NT_DIMS, preferred_element_type=F32)
            key = lax.broadcasted_iota(jnp.int32, (sub, hi), 1)
            qry = lax.broadcasted_iota(jnp.int32, (sub, hi), 0) + lo
            s = jnp.where(key <= qry, s, 0.0).astype(BF16)
            o_intra.append(jnp.dot(s, v[:hi], preferred_element_type=F32))
        o = o_inter + jnp.concatenate(o_intra, axis=0)
        gate = g_ref[r, vc].astype(F32)
        o_ref[r, vc] = (_rms(o, nw) * (gate * jax.nn.sigmoid(gate))).astype(o_ref.dtype)

    def step(c, carry):
        r = pl.ds(pl.multiple_of(c * block, block), block)
        for hh in range(heads):
            head_block(r, hh)
        return carry

    lax.fori_loop(0, q_ref.shape[0] // block, step, 0)


def gla_core(qkvg, a, w_a2, b_a2, norm_w, *, batch, seq, tb, heads_per_step):
    n = qkvg.shape[0]
    kd_total = w_a2.shape[1]
    dk = kd_total // GLA_HEADS
    vd_total = (qkvg.shape[1] - 2 * kd_total) // 2
    dv = vd_total // GLA_HEADS
    nt = seq // tb
    wk, wv = heads_per_step * dk, heads_per_step * dv
    kq = kd_total // wk
    kv = 2 * kd_total // wv
    kg = kv + vd_total // wv
    body = functools.partial(_gla_body, heads=heads_per_step, block=min(tb, GLA_BLOCK), sub=GLA_CHUNK, q_scale=dk ** -0.5)
    rows = lambda b, h, t: b * nt + t
    return pl.pallas_call(
        body, grid=(batch, GLA_HEADS // heads_per_step, nt),
        in_specs=[pl.BlockSpec((tb, wk), lambda b, h, t: (rows(b, h, t), h)),
                  pl.BlockSpec((tb, wk), lambda b, h, t: (rows(b, h, t), kq + h)),
                  pl.BlockSpec((tb, wv), lambda b, h, t: (rows(b, h, t), kv + h)),
                  pl.BlockSpec((tb, wv), lambda b, h, t: (rows(b, h, t), kg + h)),
                  pl.BlockSpec((tb, LANES), lambda b, h, t: (rows(b, h, t), 0)),
                  pl.BlockSpec((LANES, wk), lambda b, h, t: (0, h)),
                  pl.BlockSpec((1, wk), lambda b, h, t: (0, h)),
                  pl.BlockSpec((1, dv), lambda b, h, t: (0, 0))],
        out_specs=pl.BlockSpec((tb, wv), lambda b, h, t: (rows(b, h, t), h)),
        out_shape=jax.ShapeDtypeStruct((n, vd_total), BF16),
        scratch_shapes=[pltpu.VMEM((heads_per_step, dv, dk), F32), pltpu.VMEM((tb, wk), F32)],
        compiler_params=_cparams("parallel", "parallel", "arbitrary"), name="gla_core")(
            qkvg, qkvg, qkvg, qkvg, a, w_a2, b_a2, norm_w)


def _pool_body(h_ref, hp_ref, nw_ref, w_ref, sc_ref, o_ref, x_ref, *, tiles_per_seq, group):
    i = pl.program_id(0)
    tt = h_ref.shape[0]
    h = h_ref[...]
    nw = nw_ref[...]
    first = (i % tiles_per_seq) == 0
    x_ref[0:POOL_HALO, :] = jnp.where(first, 0.0, _rms(hp_ref[...], nw))
    x_ref[POOL_HALO:, :] = _rms(h, nw)
    pos = (i % tiles_per_seq) * tt + lax.broadcasted_iota(jnp.int32, (tt, 1), 0)
    for gi, win in enumerate(POOL_WINDOWS):
        cols = slice(gi * group, (gi + 1) * group)
        acc = x_ref[POOL_HALO:POOL_HALO + tt, cols]
        cur = acc
        for back in range(1, win):
            acc = acc + x_ref[POOL_HALO - back:POOL_HALO - back + tt, cols]
        count = jnp.minimum(pos + 1, win).astype(F32)
        pooled = (acc / count - cur).astype(BF16)
        y = jnp.dot(pooled, w_ref[gi], preferred_element_type=F32)
        o_ref[:, cols] = h[:, cols] + y * sc_ref[:, cols]


def pool_mixer(h, nw, w_groups, scale, *, seq, tt):
    n, d = h.shape
    ng, group, _ = w_groups.shape
    tiles_per_seq = seq // tt
    halo_blocks = tt // POOL_HALO
    body = functools.partial(_pool_body, tiles_per_seq=tiles_per_seq, group=group)
    return pl.pallas_call(
        body, grid=(n // tt,),
        in_specs=[pl.BlockSpec((tt, d), lambda i: (i, 0)),
                  pl.BlockSpec((POOL_HALO, d), lambda i: (jnp.maximum(i * halo_blocks - 1, 0), 0)),
                  pl.BlockSpec((1, d), lambda i: (0, 0)),
                  pl.BlockSpec((ng, group, group), lambda i: (0, 0, 0)),
                  pl.BlockSpec((1, d), lambda i: (0, 0))],
        out_specs=pl.BlockSpec((tt, d), lambda i: (i, 0)),
        out_shape=jax.ShapeDtypeStruct((n, d), F32),
        scratch_shapes=[pltpu.VMEM((tt + POOL_HALO, d), F32)],
        compiler_params=_cparams("parallel"), name="pool_mixer")(h, h, nw, w_groups, scale)


def _diff_body(q_ref, k_ref, v_ref, lq1_ref, lk1_ref, lq2_ref, lk2_ref, nw_ref, o_ref,
               qt_ref, vt_ref, sa_ref, sb_ref, m_ref, l_ref, acc_ref, *, lambda_init):
    qi = pl.program_id(2)
    tq = q_ref.shape[0]
    dh = DIFF_HEAD_DIM
    nk = vt_ref.shape[0]

    @pl.when(qi == 0)
    def _():
        def transpose_v(j, carry):
            r = pl.ds(pl.multiple_of(j * tq, tq), tq)
            vt_ref[j] = v_ref[r, :].astype(F32).T.astype(BF16)
            return carry

        lax.fori_loop(0, nk, transpose_v, 0)

    qt_ref[...] = q_ref[...].astype(F32).T.astype(BF16)
    m_ref[...] = jnp.full_like(m_ref, -jnp.inf)
    l_ref[...] = jnp.zeros_like(l_ref)
    acc_ref[...] = jnp.zeros_like(acc_ref)

    def scores(j, s_ref):
        k = k_ref[pl.ds(pl.multiple_of(j * tq, tq), tq), :]
        for i in range(2):
            s_ref[i] = jnp.dot(k[:, i * dh:(i + 1) * dh], qt_ref[i * dh:(i + 1) * dh, :],
                               preferred_element_type=F32)

    def consume(j, s_ref, masked):
        vt = vt_ref[j]
        if masked:
            key = lax.broadcasted_iota(jnp.int32, (tq, tq), 0)
            qry = lax.broadcasted_iota(jnp.int32, (tq, tq), 1)
            keep = key <= qry
        for i in range(2):
            s = s_ref[i]
            if masked:
                s = jnp.where(keep, s, -jnp.inf)
            m_old = m_ref[i]
            m_new = jnp.maximum(m_old, jnp.max(s, axis=0, keepdims=True))
            alpha = jnp.exp2(m_old - m_new)
            p = jnp.exp2(s - m_new)
            l_ref[i] = alpha * l_ref[i] + jnp.sum(p, axis=0, keepdims=True)
            acc_ref[i] = alpha * acc_ref[i] + jnp.dot(vt, p.astype(BF16), preferred_element_type=F32)
            m_ref[i] = m_new

    def pipelined_pair(t, carry):
        j = 2 * t
        scores(j + 1, sb_ref)
        consume(j, sa_ref, False)
        scores(j + 2, sa_ref)
        consume(j + 1, sb_ref, False)
        return carry

    scores(0, sa_ref)
    lax.fori_loop(0, qi // 2, pipelined_pair, 0)

    @pl.when(qi % 2 == 0)
    def _():
        consume(qi, sa_ref, True)

    @pl.when(qi % 2 == 1)
    def _():
        scores(qi, sb_ref)
        consume(qi - 1, sa_ref, False)
        consume(qi, sb_ref, True)

    lam = (jnp.exp(jnp.sum(lq1_ref[...] * lk1_ref[...], axis=-1, keepdims=True))
           - jnp.exp(jnp.sum(lq2_ref[...] * lk2_ref[...], axis=-1, keepdims=True)) + lambda_init)
    ot = acc_ref[0] * (1.0 / l_ref[0]) - lam * (acc_ref[1] * (1.0 / l_ref[1]))
    ms = jnp.mean(ot * ot, axis=0, keepdims=True)
    ot = ot * lax.rsqrt(ms + NORM_EPS) * (nw_ref[...] * (1.0 - lambda_init))
    o_ref[...] = ot.T.astype(o_ref.dtype)


def diff_attention_core(qkv, lq1, lk1, lq2, lk2, norm_col, *, batch, seq, tq, lambda_init):
    n = qkv.shape[0]
    dh = DIFF_HEAD_DIM
    nq = seq // tq
    k_off = DIFF_HEADS
    v_off = 2 * DIFF_HEADS
    vec = pl.BlockSpec((1, dh), lambda b, h, q: (0, 0))
    return pl.pallas_call(
        functools.partial(_diff_body, lambda_init=lambda_init), grid=(batch, DIFF_HEADS, nq),
        in_specs=[pl.BlockSpec((tq, 2 * dh), lambda b, h, q: (b * nq + q, h)),
                  pl.BlockSpec((seq, 2 * dh), lambda b, h, q: (b, k_off + h)),
                  pl.BlockSpec((seq, 2 * dh), lambda b, h, q: (b, v_off + h)),
                  vec, vec, vec, vec,
                  pl.BlockSpec((2 * dh, 1), lambda b, h, q: (0, 0))],
        out_specs=pl.BlockSpec((tq, 2 * dh), lambda b, h, q: (b * nq + q, h)),
        out_shape=jax.ShapeDtypeStruct((n, DIFF_HEADS * 2 * dh), BF16),
        scratch_shapes=[pltpu.VMEM((2 * dh, tq), BF16), pltpu.VMEM((nq, 2 * dh, tq), BF16),
                        pltpu.VMEM((2, tq, tq), F32), pltpu.VMEM((2, tq, tq), F32), pltpu.VMEM((2, 1, tq), F32),
                        pltpu.VMEM((2, 1, tq), F32), pltpu.VMEM((2, 2 * dh, tq), F32)],
        compiler_params=_cparams("parallel", "parallel", "arbitrary"), name="diff_attention")(
            qkv, qkv, qkv, lq1, lk1, lq2, lk2, norm_col)


def _ffn_body(h_ref, nw_ref, wg_ref, wu_ref, wd_ref, o_ref, xn_ref):
    @pl.when(pl.program_id(1) == 0)
    def _():
        h = h_ref[...]
        xn_ref[...] = _rms(h, nw_ref[...]).astype(BF16)
        o_ref[...] = h

    xn = xn_ref[...]
    gate = jnp.dot(xn, wg_ref[...], preferred_element_type=F32)
    up = jnp.dot(xn, wu_ref[...], preferred_element_type=F32)
    act = (gate * jax.nn.sigmoid(gate) * up).astype(BF16)
    o_ref[...] += jnp.dot(act, wd_ref[...], preferred_element_type=F32)


def ffn_residual(h, nw, w_gu, w_down, *, tm, tf):
    n, d = h.shape
    f = w_down.shape[0]
    nf = f // tf
    return pl.pallas_call(
        _ffn_body, grid=(n // tm, nf),
        in_specs=[pl.BlockSpec((tm, d), lambda i, j: (i, 0)), pl.BlockSpec((1, d), lambda i, j: (0, 0)),
                  pl.BlockSpec((d, tf), lambda i, j: (0, j)), pl.BlockSpec((d, tf), lambda i, j: (0, nf + j)),
                  pl.BlockSpec((tf, d), lambda i, j: (j, 0))],
        out_specs=pl.BlockSpec((tm, d), lambda i, j: (i, 0)),
        out_shape=jax.ShapeDtypeStruct((n, d), F32),
        scratch_shapes=[pltpu.VMEM((tm, d), BF16)],
        compiler_params=_cparams("parallel", "arbitrary"), name="ffn_residual")(h, nw, w_gu, w_gu, w_down)


def _pack_bf16_pairs(x):
    w = x.shape[1] // 2
    lo = pltpu.bitcast(x[:, :w].astype(BF16).astype(F32), jnp.uint32) >> 16
    hi = pltpu.bitcast(x[:, w:].astype(BF16).astype(F32), jnp.uint32)
    return hi | lo


def _unpack_bf16_pairs(u):
    lo = pltpu.bitcast(u << 16, F32).astype(BF16)
    hi = pltpu.bitcast(u & jnp.uint32(0xFFFF0000), F32).astype(BF16)
    return jnp.concatenate([lo, hi], axis=1)


def _router_body(h_ref, nw_ref, r_ref, xn_ref, info_ref, cnt_ref, carry_ref):
    i = pl.program_id(0)

    @pl.when(i == 0)
    def _():
        carry_ref[...] = jnp.zeros_like(carry_ref)

    xn = _rms(h_ref[...], nw_ref[...])
    xn_ref[...] = _pack_bf16_pairs(xn)
    x_hi, x_mid, x_lo = _split3(xn)
    dot = functools.partial(jnp.dot, preferred_element_type=F32)
    r3 = r_ref[...]
    a, b, c = dot(x_hi, r3), dot(x_mid, r3), dot(x_lo, r3)
    e = N_EXPERTS
    logits = ((c + pltpu.roll(b, LANES - e, 1) + pltpu.roll(a, LANES - 2 * e, 1))
              + (b + pltpu.roll(a, LANES - e, 1))) + a
    tm = logits.shape[0]
    lane = lax.broadcasted_iota(jnp.int32, logits.shape, 1)
    lane_f = lane.astype(F32)
    lg = jnp.where(lane < N_EXPERTS, logits, -jnp.inf)
    m1 = jnp.max(lg, axis=-1, keepdims=True)
    e1 = jnp.min(jnp.where(lg == m1, lane_f, float(LANES)), axis=-1, keepdims=True)
    lg2 = jnp.where(lane_f == e1, -jnp.inf, lg)
    m2 = jnp.max(lg2, axis=-1, keepdims=True)
    e2 = jnp.min(jnp.where(lg2 == m2, lane_f, float(LANES)), axis=-1, keepdims=True)
    ex = jnp.exp(m2 - m1)
    g1 = 1.0 / (1.0 + ex)
    g2 = ex / (1.0 + ex)
    hot1 = lane_f == e1
    hot2 = lane_f == e2
    onehot = jnp.where(hot1 | hot2, 1.0, 0.0)
    row = lax.broadcasted_iota(jnp.int32, (tm, tm), 0)
    col = lax.broadcasted_iota(jnp.int32, (tm, tm), 1)
    before = jnp.where(col < row, 1.0, 0.0).astype(BF16)
    cum = dot(before, onehot.astype(BF16)) + carry_ref[...]
    rank1 = jnp.sum(jnp.where(hot1, cum, 0.0), axis=-1, keepdims=True)
    rank2 = jnp.sum(jnp.where(hot2, cum, 0.0), axis=-1, keepdims=True)
    info = jnp.zeros(logits.shape, F32)
    for idx, val in enumerate((e1, e2, rank1, rank2, g1, g2)):
        info = jnp.where(lane == idx, val, info)
    info_ref[...] = info
    carry_ref[...] += jnp.sum(onehot, axis=0, keepdims=True)
    cnt_ref[...] = jnp.broadcast_to(carry_ref[...], cnt_ref.shape)


def moe_router(h, nw, router_pad, *, tm):
    n, d = h.shape
    return pl.pallas_call(
        _router_body, grid=(n // tm,),
        in_specs=[pl.BlockSpec((tm, d), lambda i: (i, 0)), pl.BlockSpec((1, d), lambda i: (0, 0)),
                  pl.BlockSpec((d, LANES), lambda i: (0, 0))],
        out_specs=[pl.BlockSpec((tm, d // 2), lambda i: (i, 0)), pl.BlockSpec((tm, LANES), lambda i: (i, 0)),
                   pl.BlockSpec((8, LANES), lambda i: (0, 0))],
        out_shape=[jax.ShapeDtypeStruct((n, d // 2), jnp.uint32), jax.ShapeDtypeStruct((n, LANES), F32),
                   jax.ShapeDtypeStruct((8, LANES), F32)],
        scratch_shapes=[pltpu.VMEM((1, LANES), F32)],
        compiler_params=_cparams("arbitrary"), name="moe_router")(h, nw, router_pad)


def _dispatch_body(p1_ref, p2_ref, ps_ref, pn_ref, ts_ref, tn_ref, x_ref, xs_ref, zero_ref, sem, zsem):
    tr = x_ref.shape[0]
    zr = zero_ref.shape[0]
    base = pl.program_id(0) * tr

    @pl.when(pl.program_id(0) == 0)
    def _():
        zero_ref[...] = jnp.zeros_like(zero_ref)

        def pad_copy(dst):
            return pltpu.make_async_copy(zero_ref.at[pl.ds(0, 1)], xs_ref.at[pl.ds(dst, 1)], zsem)

        def tail_copy(c):
            return pltpu.make_async_copy(zero_ref, xs_ref.at[pl.ds(pl.multiple_of(ts_ref[0] + c * zr, zr), zr)], zsem)

        for e in range(N_EXPERTS):
            lax.fori_loop(0, pn_ref[e], lambda r, c, e=e: (pad_copy(ps_ref[e] + r).start(), c)[1], 0)
        lax.fori_loop(0, tn_ref[0], lambda c, carry: (tail_copy(c).start(), carry)[1], 0)
        for e in range(N_EXPERTS):
            lax.fori_loop(0, pn_ref[e], lambda r, c, e=e: (pad_copy(ps_ref[e] + r).wait(), c)[1], 0)
        lax.fori_loop(0, tn_ref[0], lambda c, carry: (tail_copy(c).wait(), carry)[1], 0)

    def row_copy(r, dst):
        return pltpu.make_async_copy(x_ref.at[pl.ds(r, 1)], xs_ref.at[pl.ds(dst, 1)], sem)

    def issue(r, carry):
        row_copy(r, p1_ref[base + r]).start(priority=0)
        row_copy(r, p2_ref[base + r]).start(priority=1)
        return carry

    def drain(r, carry):
        row_copy(r, p1_ref[base + r]).wait()
        row_copy(r, p2_ref[base + r]).wait()
        return carry

    lax.fori_loop(0, tr, issue, 0, unroll=DMA_LOOP_UNROLL)
    lax.fori_loop(0, tr, drain, 0, unroll=DMA_LOOP_UNROLL)


def moe_dispatch(x32, pos1, pos2, pad_start, pad_len, tail_start, tail_chunks, *, p_rows, tr, zr):
    n, w = x32.shape
    grid_spec = pltpu.PrefetchScalarGridSpec(
        num_scalar_prefetch=6, grid=(n // tr,),
        in_specs=[pl.BlockSpec((tr, w), lambda i, *_: (i, 0))],
        out_specs=pl.BlockSpec(memory_space=pl.ANY),
        scratch_shapes=[pltpu.VMEM((zr, w), x32.dtype), pltpu.SemaphoreType.DMA, pltpu.SemaphoreType.DMA])
    return pl.pallas_call(
        _dispatch_body, grid_spec=grid_spec, out_shape=jax.ShapeDtypeStruct((p_rows, w), x32.dtype),
        compiler_params=_cparams("arbitrary"), name="moe_dispatch")(
            pos1, pos2, pad_start, pad_len, tail_start, tail_chunks, x32)


def _moe_ffn_body(te_ref, nv_ref, na_ref, x_ref, wg_ref, wu_ref, wd_ref, o_ref, xb_ref, acc_ref, *, parts):
    i = pl.program_id(0)
    j = pl.program_id(1)
    part_rows = x_ref.shape[0] // parts

    @pl.when(j == 0)
    def _():
        acc_ref[...] = jnp.zeros_like(acc_ref)
        xb_ref[...] = _unpack_bf16_pairs(x_ref[...])

    def leading_rows(rows):
        x = xb_ref[:rows, :]
        gate = jnp.dot(x, wg_ref[0].astype(BF16), preferred_element_type=F32)
        up = jnp.dot(x, wu_ref[0].astype(BF16), preferred_element_type=F32)
        act = (gate * jax.nn.sigmoid(gate) * up).astype(BF16)
        acc_ref[:rows, :] += jnp.dot(act, wd_ref[0].astype(BF16), preferred_element_type=F32)

    for k in range(1, parts + 1):
        lo, hi = (k - 1) * part_rows, k * part_rows
        pl.when((nv_ref[i] > lo) & (nv_ref[i] <= hi))(functools.partial(leading_rows, hi))

    @pl.when(j == pl.num_programs(1) - 1)
    def _():
        o_ref[...] = _pack_bf16_pairs(acc_ref[...])


def moe_ffn(xs32, tile_expert, tile_valid, n_active, w_gu, w_down, *, tm, tf, parts):
    p, w = xs32.shape
    d = 2 * w
    f = w_down.shape[1]
    nf = f // tf

    def frozen(i, j, na):
        return jnp.where(i < na[0], j, nf - 1)

    grid_spec = pltpu.PrefetchScalarGridSpec(
        num_scalar_prefetch=3, grid=(p // tm, nf),
        in_specs=[pl.BlockSpec((tm, w), lambda i, j, te, nv, na: (jnp.minimum(i, na[0] - 1), 0)),
                  pl.BlockSpec((1, d, tf), lambda i, j, te, nv, na: (te[i], 0, frozen(i, j, na))),
                  pl.BlockSpec((1, d, tf), lambda i, j, te, nv, na: (te[i], 0, nf + frozen(i, j, na))),
                  pl.BlockSpec((1, tf, d), lambda i, j, te, nv, na: (te[i], frozen(i, j, na), 0))],
        out_specs=pl.BlockSpec((tm, w), lambda i, j, te, nv, na: (i, 0)),
        scratch_shapes=[pltpu.VMEM((tm, d), BF16), pltpu.VMEM((tm, d), F32)])
    return pl.pallas_call(
        functools.partial(_moe_ffn_body, parts=parts), grid_spec=grid_spec,
        out_shape=jax.ShapeDtypeStruct((p, w), jnp.uint32),
        compiler_params=_cparams("parallel", "arbitrary"), name="moe_ffn")(
            tile_expert, tile_valid, n_active, xs32, w_gu, w_gu, w_down)


def _combine_body(p1_ref, p2_ref, h_ref, info_ref, y_ref, o_ref, buf1, buf2, sem):
    tr = h_ref.shape[0]
    half = tr // 2
    base = pl.program_id(0) * tr

    def row_copy(src, buf, r, s):
        return pltpu.make_async_copy(y_ref.at[pl.ds(src, 1)], buf.at[pl.ds(r, 1)], sem.at[s])

    def issue(s):
        def body(r, carry):
            row_copy(p1_ref[base + r], buf1, r, s).start(priority=0)
            row_copy(p2_ref[base + r], buf2, r, s).start(priority=1)
            return carry

        lax.fori_loop(s * half, (s + 1) * half, body, 0, unroll=DMA_LOOP_UNROLL)

    def drain_and_combine(s):
        def body(r, carry):
            row_copy(p1_ref[base + r], buf1, r, s).wait()
            row_copy(p2_ref[base + r], buf2, r, s).wait()
            return carry

        lax.fori_loop(s * half, (s + 1) * half, body, 0, unroll=DMA_LOOP_UNROLL)
        rows = slice(s * half, (s + 1) * half)
        info = info_ref[rows, :]
        g1 = info[:, 4:5]
        g2 = info[:, 5:6]
        y1 = _unpack_bf16_pairs(buf1[rows, :]).astype(F32)
        y2 = _unpack_bf16_pairs(buf2[rows, :]).astype(F32)
        o_ref[rows, :] = h_ref[rows, :] + (g1 * y1 + g2 * y2)

    issue(0)
    issue(1)
    drain_and_combine(0)
    drain_and_combine(1)


def moe_combine(h, info, y32, pos1, pos2, *, tr):
    n, d = h.shape
    w = y32.shape[1]
    grid_spec = pltpu.PrefetchScalarGridSpec(
        num_scalar_prefetch=2, grid=(n // tr,),
        in_specs=[pl.BlockSpec((tr, d), lambda i, p1, p2: (i, 0)), pl.BlockSpec((tr, LANES), lambda i, p1, p2: (i, 0)),
                  pl.BlockSpec(memory_space=pl.ANY)],
        out_specs=pl.BlockSpec((tr, d), lambda i, p1, p2: (i, 0)),
        scratch_shapes=[pltpu.VMEM((tr, w), y32.dtype), pltpu.VMEM((tr, w), y32.dtype), pltpu.SemaphoreType.DMA((2,))])
    return pl.pallas_call(
        _combine_body, grid_spec=grid_spec, out_shape=jax.ShapeDtypeStruct((n, d), F32),
        compiler_params=_cparams("arbitrary"), name="moe_combine")(pos1, pos2, h, info, y32)


def moe_residual(h, nw, router, w_gu, w_down, *, tm_route, tr, tm, tf):
    n, d = h.shape
    router_terms = jnp.concatenate(_split3(router.astype(F32)), axis=1)
    router_pad = jnp.zeros((d, LANES), BF16).at[:, :3 * N_EXPERTS].set(router_terms)
    x32, info, counts = moe_router(h, nw, router_pad, tm=tm_route)
    cnt = counts[0, :N_EXPERTS].astype(jnp.int32)
    padded = (cnt + tm - 1) // tm * tm
    ends = jnp.cumsum(padded)
    offs = ends - padded
    e1 = info[:, 0].astype(jnp.int32)
    e2 = info[:, 1].astype(jnp.int32)
    pos1 = offs[e1] + info[:, 2].astype(jnp.int32)
    pos2 = offs[e2] + info[:, 3].astype(jnp.int32)
    p_rows = 2 * n + N_EXPERTS * tm
    n_tiles = p_rows // tm
    tile_start = jnp.arange(n_tiles, dtype=jnp.int32) * tm
    tile_expert = jnp.minimum(jnp.sum(tile_start[:, None] >= ends[None, :], axis=1), N_EXPERTS - 1).astype(jnp.int32)
    tile_valid = jnp.clip((offs + cnt)[tile_expert] - tile_start, 0, tm).astype(jnp.int32)
    n_active = (ends[-1:] // tm).astype(jnp.int32)
    zr = min(tm, 256)
    xs32 = moe_dispatch(x32, pos1, pos2, offs + cnt, padded - cnt, ends[-1:], (p_rows - ends[-1:]) // zr,
                        p_rows=p_rows, tr=tr, zr=zr)
    y32 = moe_ffn(xs32, tile_expert, tile_valid, n_active, w_gu, w_down, tm=tm, tf=tf, parts=MOE_TILE_PARTS)
    return moe_combine(h, info, y32, pos1, pos2, tr=tr)


def _ple_body(h_ref, nw_ref, wg_ref, bg_ref, p_ref, wp_ref, *rest):
    o_ref = rest[-1]
    h = h_ref[...]
    hn = _rms(h, nw_ref[...]).astype(BF16)
    z = jnp.dot(hn, wg_ref[...], preferred_element_type=F32) + bg_ref[...]
    proj = jnp.dot(p_ref[...].astype(BF16), wp_ref[...], preferred_element_type=F32)
    out = h + jax.nn.sigmoid(z) * proj
    if len(rest) == 2:
        out = _rms(out, rest[0][...])
    o_ref[...] = out


def ple_residual(h, nw, w_gate, b_gate, p_all, layer, w_proj, *, tm, final_norm_w=None):
    n, d = h.shape
    pd = p_all.shape[-1]
    in_specs = [pl.BlockSpec((tm, d), lambda i: (i, 0)), _resident((1, d)), _resident((d, d)), _resident((1, d)),
                pl.BlockSpec((None, tm, pd), lambda i: (layer, i, 0)), _resident((pd, d))]
    args = [h, nw, w_gate, b_gate, p_all, w_proj]
    if final_norm_w is not None:
        in_specs.append(_resident((1, d)))
        args.append(final_norm_w)
    return pl.pallas_call(
        _ple_body, grid=(n // tm,), in_specs=in_specs,
        out_specs=pl.BlockSpec((tm, d), lambda i: (i, 0)),
        out_shape=jax.ShapeDtypeStruct((n, d), F32),
        compiler_params=_cparams("parallel"), name="ple_residual")(*args)


def _row(v):
    return v.reshape(1, -1).astype(F32)


def _tile(n, want):
    return min(n, want)


def _gla_layer(h, norm_mix, w_in, w_a2, b_a2, gla_norm, w_o, *, batch, seq):
    n = h.shape[0]
    main = w_in.shape[1] - GLA_GATE_RANK
    w_main = w_in[:, :main].astype(BF16)
    w_a = jnp.zeros((w_in.shape[0], LANES), BF16).at[:, :GLA_GATE_RANK].set(w_in[:, main:].astype(BF16))
    w_a2_pad = jnp.zeros((LANES, w_a2.shape[1]), BF16).at[:GLA_GATE_RANK].set(w_a2.astype(BF16))
    qkvg, a = norm_matmul_extra(h, _row(norm_mix), w_main, w_a, tm=_tile(n, 256))
    o = gla_core(qkvg, a, w_a2_pad, _row(b_a2), _row(gla_norm), batch=batch, seq=seq, tb=_tile(seq, 1024),
                 heads_per_step=GLA_HEADS_PER_STEP)
    return matmul_residual(o, w_o.astype(BF16), h, tm=_tile(n, 512))


def _diff_layer(h, positions, norm_mix, w_in, lq1, lk1, lq2, lk2, diff_norm, w_o, *, batch, seq, lambda_init):
    n = h.shape[0]
    tabs = rope_tables(positions, tm=_tile(n, 1024))
    qk_width = 2 * DIFF_HEADS * DIFF_HEAD_DIM
    qkv = norm_matmul_rope(h, _row(norm_mix), w_in.astype(BF16), tabs, tm=_tile(n, 256),
                           q_width=qk_width, k_width=qk_width, q_scale=DIFF_HEAD_DIM ** -0.5 * math.log2(math.e))
    o = diff_attention_core(qkv, _row(lq1), _row(lk1), _row(lq2), _row(lk2), diff_norm.reshape(-1, 1).astype(F32),
                            batch=batch, seq=seq, tq=_tile(seq, 512), lambda_init=lambda_init)
    return matmul_residual(o, w_o.astype(BF16), h, tm=_tile(n, 512))


def _diff_lambda_init(layer_idx):
    return 0.8 - 0.6 * math.exp(-0.3 * layer_idx)


def kernel(x, p, positions, l0_norm_mix, l0_gla_w_in, l0_gla_w_a2, l0_gla_b_a2, l0_gla_norm, l0_gla_w_o, l0_norm_ffn, l0_ffn_w_gu, l0_ffn_w_down, l0_ple_norm, l0_ple_w_gate, l0_ple_b_gate, l0_ple_w_proj, l1_norm_mix, l1_pool_w, l1_pool_scale, l1_norm_ffn, l1_moe_router, l1_moe_w_gu, l1_moe_w_down, l1_ple_norm, l1_ple_w_gate, l1_ple_b_gate, l1_ple_w_proj, l2_norm_mix, l2_diff_w_in, l2_diff_lq1, l2_diff_lk1, l2_diff_lq2, l2_diff_lk2, l2_diff_norm, l2_diff_w_o, l2_norm_ffn, l2_ffn_w_gu, l2_ffn_w_down, l2_ple_norm, l2_ple_w_gate, l2_ple_b_gate, l2_ple_w_proj, l3_norm_mix, l3_gla_w_in, l3_gla_w_a2, l3_gla_b_a2, l3_gla_norm, l3_gla_w_o, l3_norm_ffn, l3_moe_router, l3_moe_w_gu, l3_moe_w_down, l3_ple_norm, l3_ple_w_gate, l3_ple_b_gate, l3_ple_w_proj, final_norm):
    batch, seq, d = x.shape
    n = batch * seq
    h = x.reshape(n, d)
    pf = p.reshape(p.shape[0], n, p.shape[-1])
    tm = _tile(n, 512)

    def ffn(h, nw, w_gu, w_down):
        return ffn_residual(h, _row(nw), w_gu.astype(BF16), w_down.astype(BF16), tm=tm, tf=512)

    def moe(h, nw, router, w_gu, w_down):
        return moe_residual(h, _row(nw), router, w_gu, w_down, tm_route=tm, tr=_tile(n, 512), tm=_tile(n, 1024), tf=256)

    def ple(h, i, nw, w_gate, b_gate, w_proj, final_norm_w=None):
        return ple_residual(h, _row(nw), w_gate.astype(BF16), _row(b_gate), pf, i, w_proj.astype(BF16), tm=tm,
                            final_norm_w=final_norm_w)

    h = _gla_layer(h, l0_norm_mix, l0_gla_w_in, l0_gla_w_a2, l0_gla_b_a2, l0_gla_norm, l0_gla_w_o, batch=batch, seq=seq)
    h = ffn(h, l0_norm_ffn, l0_ffn_w_gu, l0_ffn_w_down)
    h = ple(h, 0, l0_ple_norm, l0_ple_w_gate, l0_ple_b_gate, l0_ple_w_proj)

    h = pool_mixer(h, _row(l1_norm_mix), l1_pool_w.astype(BF16), _row(l1_pool_scale), seq=seq, tt=_tile(seq, 512))
    h = moe(h, l1_norm_ffn, l1_moe_router, l1_moe_w_gu, l1_moe_w_down)
    h = ple(h, 1, l1_ple_norm, l1_ple_w_gate, l1_ple_b_gate, l1_ple_w_proj)

    h = _diff_layer(h, positions, l2_norm_mix, l2_diff_w_in, l2_diff_lq1, l2_diff_lk1, l2_diff_lq2, l2_diff_lk2,
                    l2_diff_norm, l2_diff_w_o, batch=batch, seq=seq, lambda_init=_diff_lambda_init(2))
    h = ffn(h, l2_norm_ffn, l2_ffn_w_gu, l2_ffn_w_down)
    h = ple(h, 2, l2_ple_norm, l2_ple_w_gate, l2_ple_b_gate, l2_ple_w_proj)

    h = _gla_layer(h, l3_norm_mix, l3_gla_w_in, l3_gla_w_a2, l3_gla_b_a2, l3_gla_norm, l3_gla_w_o, batch=batch, seq=seq)
    h = moe(h, l3_norm_ffn, l3_moe_router, l3_moe_w_gu, l3_moe_w_down)
    h = ple(h, 3, l3_ple_norm, l3_ple_w_gate, l3_ple_b_gate, l3_ple_w_proj, final_norm_w=_row(final_norm))
    return h.reshape(batch, seq, d)
```
